```python
import math
import jax, jax.numpy as jnp
from jax import lax
import numpy as np

D_MODEL = 2048
BATCH = 2
SEQ = 16384
DEPTH = 2

CTX_LEN = 256
GRID_W = 64
EPS = 1e-6
GDN_HEADS = 8
GDN_DK = 128
GDN_DV = 128
GDN_QK = GDN_HEADS * GDN_DK
GDN_V = GDN_HEADS * GDN_DV
CONV_W = 5
CHUNK = 64
MLA_HEADS = 8
Q_LORA = 512
KV_LORA = 256
NOPE_DIM = 128
ROPE_DIM = 64
MLA_V = 128
QK_HEAD = NOPE_DIM + ROPE_DIM
ROPE_AXIS_PAIRS = ROPE_DIM // 4
ROPE_THETA = 10000.0
Q_BLOCK = 128
OFF_Q = 0
OFF_K = OFF_Q + GDN_QK
OFF_V = OFF_K + GDN_QK
OFF_Z = OFF_V + GDN_V
OFF_A_FWD = OFF_Z + GDN_V
OFF_A_BWD = OFF_A_FWD + GDN_HEADS
OFF_B_FWD = OFF_A_BWD + GDN_HEADS
OFF_B_BWD = OFF_B_FWD + GDN_HEADS
OFF_CQ = OFF_B_BWD + GDN_HEADS
OFF_CKV = OFF_CQ + Q_LORA
OFF_KR = OFF_CKV + KV_LORA
IN_DIM = OFF_KR + ROPE_DIM
MIX_DIM = GDN_V + MLA_HEADS * MLA_V
POOL_WINDOWS = (2, 4, 8, 16)
POOL_GROUP = D_MODEL // len(POOL_WINDOWS)
N_EXPERTS = 64
N_GROUPS = 8
EXPERTS_PER_GROUP = N_EXPERTS // N_GROUPS
TOP_K = 2
D_EXPERT = 512
MOE_BLOCK = 128

kernel_name = 'hybrid_gdn_mla_pool_moe_dit'


def rms_norm(x, g):
    xf = x.astype(jnp.float32)
    y = xf * lax.rsqrt(jnp.mean(xf * xf, axis=-1, keepdims=True) + EPS)
    return (y * g.astype(jnp.float32)).astype(x.dtype)


def l2_normalize(x):
    return x * lax.rsqrt(jnp.sum(x * x, axis=-1, keepdims=True) + EPS)


def modulate(h, shift, scale):
    return h * (1.0 + scale) + shift


def axial_rope_tables(rows):
    row = jnp.repeat(jnp.arange(rows, dtype=jnp.float32), GRID_W)
    col = jnp.tile(jnp.arange(GRID_W, dtype=jnp.float32), rows)
    inv_freq = ROPE_THETA ** (-jnp.arange(ROPE_AXIS_PAIRS, dtype=jnp.float32) / ROPE_AXIS_PAIRS)
    ang_r = row[:, None] * inv_freq
    ang_c = col[:, None] * inv_freq
    ang = jnp.concatenate([ang_r, ang_r, ang_c, ang_c], axis=-1)
    return jnp.cos(ang), jnp.sin(ang)


def apply_axial_rope(x, cos, sin):
    xn = x[..., :NOPE_DIM]
    xr = x[..., NOPE_DIM:].astype(jnp.float32)
    parts = xr.reshape(xr.shape[:-1] + (2, 2, ROPE_AXIS_PAIRS))
    rot = jnp.concatenate([-parts[..., 1:, :], parts[..., :1, :]], axis=-2).reshape(xr.shape)
    xr = xr * cos[None, :, None, :] + rot * sin[None, :, None, :]
    return jnp.concatenate([xn, xr.astype(x.dtype)], axis=-1)


def short_conv(x, w):
    return lax.conv_general_dilated(
        x, w[:, None, :].astype(x.dtype), window_strides=(1,),
        padding=[(CONV_W // 2, CONV_W // 2)],
        dimension_numbers=('NWC', 'WIO', 'NWC'), feature_group_count=x.shape[-1])


def chunk_gated_delta(q, k, v, g, beta, state):
    bsz, heads, seq, _ = k.shape
    dv = v.shape[-1]
    n = seq // CHUNK

    def chunks(a):
        return a.reshape(a.shape[:2] + (n, CHUNK) + a.shape[3:])

    k, v, g, beta = chunks(k), chunks(v), chunks(g), chunks(beta)
    g_cum = jnp.cumsum(g, axis=-1)
    causal = jnp.tril(jnp.ones((CHUNK, CHUNK), bool))
    strict = jnp.tril(jnp.ones((CHUNK, CHUNK), bool), -1)
    diff = g_cum[..., :, None] - g_cum[..., None, :]
    decay = jnp.where(causal, jnp.exp(jnp.where(causal, diff, 0.0)), 0.0)
    k_beta = k * beta[..., None]
    lower = jnp.where(strict, jnp.einsum('bhnid,bhnjd->bhnij', k_beta, k) * decay, 0.0)
    a_mat = lower + jnp.eye(CHUNK, dtype=jnp.float32)
    rhs = jnp.concatenate([v * beta[..., None], k_beta * jnp.exp(g_cum)[..., None]], axis=-1)
    sol = lax.linalg.triangular_solve(a_mat, rhs, left_side=True, lower=True, unit_diagonal=True)
    g_last = g_cum[..., -1]
    xs = {'u': sol[..., :dv], 'w': sol[..., dv:],
          'kd': k * jnp.exp(g_last[..., None] - g_cum)[..., None],
          'cd': jnp.exp(g_last)}
    if q is not None:
        q = chunks(q)
        xs['qk'] = jnp.einsum('bhnid,bhnjd->bhnij', q, k) * decay
        xs['qd'] = q * jnp.exp(g_cum)[..., None]
    xs = jax.tree_util.tree_map(lambda a: jnp.moveaxis(a, 2, 0), xs)

    def step(s, inp):
        v_new = inp['u'] - jnp.einsum('bhck,bhkv->bhcv', inp['w'], s)
        out = None
        if 'qk' in inp:
            out = (jnp.einsum('bhck,bhkv->bhcv', inp['qd'], s)
                   + jnp.einsum('bhij,bhjv->bhiv', inp['qk'], v_new))
        s = s * inp['cd'][..., None, None] + jnp.einsum('bhck,bhcv->bhkv', inp['kd'], v_new)
        return s, out

    state, o = lax.scan(step, state, xs)
    if q is None:
        return None, state
    return jnp.moveaxis(o, 0, 2).reshape(bsz, heads, seq, dv), state


def gdn_section(p, conv_w, lo, hi, unit):
    bsz, seq, _ = p.shape
    y = jax.nn.silu(short_conv(p[..., lo:hi], conv_w[:, lo:hi]).astype(jnp.float32))
    y = y.reshape(bsz, seq, GDN_HEADS, -1).transpose(0, 2, 1, 3)
    return l2_normalize(y) if unit else y


def decay_and_beta(p, off_a, off_b, a_log, dt_bias):
    a = p[..., off_a:off_a + GDN_HEADS].astype(jnp.float32)
    b = p[..., off_b:off_b + GDN_HEADS].astype(jnp.float32)
    g = -jnp.exp(a_log.astype(jnp.float32)) * jax.nn.softplus(a + dt_bias.astype(jnp.float32))
    return g.transpose(0, 2, 1), jax.nn.sigmoid(b).transpose(0, 2, 1)


def seq_flip(t, rev):
    return jnp.flip(t, axis=2) if rev else t


def gated_head_norm(o, z, g):
    bsz, heads, seq, dv = o.shape
    y = rms_norm(o.transpose(0, 2, 1, 3), g)
    y = y * jax.nn.silu(z.astype(jnp.float32)).reshape(bsz, seq, heads, dv)
    return y.reshape(bsz, seq, heads * dv).astype(z.dtype)


def gdn_mixer(p, pc, conv_w, a_log_fwd, a_log_bwd, dt_bias_fwd, dt_bias_bwd, out_g, with_ctx_out):
    bsz = p.shape[0]
    q = gdn_section(p, conv_w, OFF_Q, OFF_K, True) * GDN_DK ** -0.5
    k = gdn_section(p, conv_w, OFF_K, OFF_V, True)
    v = gdn_section(p, conv_w, OFF_V, OFF_Z, False)
    qc = gdn_section(pc, conv_w, OFF_Q, OFF_K, True) * GDN_DK ** -0.5 if with_ctx_out else None
    kc = gdn_section(pc, conv_w, OFF_K, OFF_V, True)
    vc = gdn_section(pc, conv_w, OFF_V, OFF_Z, False)
    s0 = jnp.zeros((bsz, GDN_HEADS, GDN_DK, GDN_DV), jnp.float32)
    o = 0.0
    oc = 0.0
    directions = ((a_log_fwd, dt_bias_fwd, OFF_A_FWD, OFF_B_FWD, False),
                  (a_log_bwd, dt_bias_bwd, OFF_A_BWD, OFF_B_BWD, True))
    for a_log, dt_bias, off_a, off_b, rev in directions:
        gc, bc = decay_and_beta(pc, off_a, off_b, a_log, dt_bias)
        o_c, s_ctx = chunk_gated_delta(seq_flip(qc, rev) if with_ctx_out else None,
                                       seq_flip(kc, rev), seq_flip(vc, rev),
                                       seq_flip(gc, rev), seq_flip(bc, rev), s0)
        g, bt = decay_and_beta(p, off_a, off_b, a_log, dt_bias)
        o_l, _ = chunk_gated_delta(seq_flip(q, rev), seq_flip(k, rev), seq_flip(v, rev),
                                   seq_flip(g, rev), seq_flip(bt, rev), s_ctx)
        o = o + seq_flip(o_l, rev)
        if with_ctx_out:
            oc = oc + seq_flip(o_c, rev)
    y = gated_head_norm(o, p[..., OFF_Z:OFF_A_FWD], out_g)
    yc = gated_head_norm(oc, pc[..., OFF_Z:OFF_A_FWD], out_g) if with_ctx_out else None
    return y, yc


def mla_queries(c_q, q_a_norm_g, w_uq, q_norm_g, cos, sin):
    bsz, seq, _ = c_q.shape
    q = jnp.dot(rms_norm(c_q, q_a_norm_g), w_uq).reshape(bsz, seq, MLA_HEADS, QK_HEAD)
    q = rms_norm(q, q_norm_g)
    return apply_axial_rope(q, cos, sin) if cos is not None else q


def mla_keys_values(c_kv, k_rope, kv_a_norm_g, w_ukv, k_norm_g, cos, sin):
    bsz, seq, _ = c_kv.shape
    kv = jnp.dot(rms_norm(c_kv, kv_a_norm_g), w_ukv).reshape(bsz, seq, MLA_HEADS, NOPE_DIM + MLA_V)
    k = jnp.concatenate([kv[..., :NOPE_DIM],
                         jnp.broadcast_to(k_rope[:, :, None, :], (bsz, seq, MLA_HEADS, ROPE_DIM))], axis=-1)
    k = rms_norm(k, k_norm_g)
    if cos is not None:
        k = apply_axial_rope(k, cos, sin)
    return k, kv[..., NOPE_DIM:]


def attend_blocks(q, k, v):
    bsz, tq, heads, dq = q.shape
    nq = tq // Q_BLOCK
    kt = k.transpose(0, 2, 1, 3)
    vt = v.transpose(0, 2, 1, 3)
    qb = q.reshape(bsz, nq, Q_BLOCK, heads, dq).transpose(1, 0, 3, 2, 4)
    scale = dq ** -0.5

    def block(qblk):
        s = jnp.einsum('bhqd,bhkd->bhqk', qblk, kt, preferred_element_type=jnp.float32) * scale
        pr = jax.nn.softmax(s, axis=-1)
        return jnp.einsum('bhqk,bhkd->bhqd', pr.astype(vt.dtype), vt)

    o = lax.map(block, qb)
    return o.transpose(1, 0, 3, 2, 4).reshape(bsz, tq, heads * vt.shape[-1])


def hybrid_mixer(h, hc, w_in, conv_qkv, a_log_fwd, a_log_bwd, dt_bias_fwd, dt_bias_bwd, gdn_out_g,
                 q_a_norm_g, w_uq, kv_a_norm_g, w_ukv, q_norm_g, k_norm_g, w_out, cos, sin, with_ctx_out):
    p = jnp.dot(h, w_in)
    pc = jnp.dot(hc, w_in)
    y_gdn, yc_gdn = gdn_mixer(p, pc, conv_qkv, a_log_fwd, a_log_bwd, dt_bias_fwd, dt_bias_bwd,
                              gdn_out_g, with_ctx_out)
    q = mla_queries(p[..., OFF_CQ:OFF_CKV], q_a_norm_g, w_uq, q_norm_g, cos, sin)
    k, v = mla_keys_values(p[..., OFF_CKV:OFF_KR], p[..., OFF_KR:IN_DIM], kv_a_norm_g, w_ukv,
                           k_norm_g, cos, sin)
    kc, vc = mla_keys_values(pc[..., OFF_CKV:OFF_KR], pc[..., OFF_KR:IN_DIM], kv_a_norm_g, w_ukv,
                             k_norm_g, None, None)
    y_mla = attend_blocks(q, jnp.concatenate([k, kc], axis=1), jnp.concatenate([v, vc], axis=1))
    y = jnp.dot(jnp.concatenate([y_gdn, y_mla.astype(y_gdn.dtype)], axis=-1), w_out)
    yc = None
    if with_ctx_out:
        qc = mla_queries(pc[..., OFF_CQ:OFF_CKV], q_a_norm_g, w_uq, q_norm_g, None, None)
        yc_mla = attend_blocks(qc, kc, vc)
        yc = jnp.dot(jnp.concatenate([yc_gdn, yc_mla.astype(yc_gdn.dtype)], axis=-1), w_out)
    return y, yc


def multiscale_pool(h, w_pool, pool_scale):
    bsz, seq, d = h.shape
    hf = h.astype(jnp.float32)
    csum = jnp.concatenate([jnp.zeros((bsz, 1, d), jnp.float32), jnp.cumsum(hf, axis=1)], axis=1)
    t = jnp.arange(seq)
    groups = []
    for gi, win in enumerate(POOL_WINDOWS):
        c0, c1 = gi * POOL_GROUP, (gi + 1) * POOL_GROUP
        lo = jnp.clip(t - win // 2, 0, seq)
        hi = jnp.clip(t - win // 2 + win, 0, seq)
        cs = csum[..., c0:c1]
        mean = (jnp.take(cs, hi, axis=1) - jnp.take(cs, lo, axis=1)) / (hi - lo).astype(jnp.float32)[None, :, None]
        groups.append(mean - hf[..., c0:c1])
    pooled = jnp.stack(groups, axis=2)
    y = jnp.einsum('btgc,gcd->btgd', pooled, w_pool.astype(jnp.float32)).reshape(bsz, seq, d)
    return (y * pool_scale.astype(jnp.float32)).astype(h.dtype)


def moe_ffn(h, w_router, router_bias, w_gate_up, w_down):
    n_tok, d = h.shape
    scores = jax.nn.sigmoid(jnp.dot(h, w_router).astype(jnp.float32))
    biased = scores + router_bias.astype(jnp.float32)
    grouped = biased.reshape(n_tok, N_GROUPS, EXPERTS_PER_GROUP)
    group_score = lax.top_k(grouped, 2)[0].sum(-1)
    group = jnp.argmax(group_score, axis=-1)
    in_group = jnp.take_along_axis(grouped, group[:, None, None], axis=1)[:, 0]
    local = lax.top_k(in_group, TOP_K)[1]
    expert = group[:, None] * EXPERTS_PER_GROUP + local
    gate = jnp.take_along_axis(scores, expert, axis=1)
    gate = gate / jnp.sum(gate, axis=-1, keepdims=True)
    n_assign = n_tok * TOP_K
    flat_e = expert.reshape(n_assign).astype(jnp.int32)
    order = jnp.argsort(flat_e, stable=True).astype(jnp.int32)
    e_sorted = flat_e[order]
    counts = jnp.zeros((N_EXPERTS,), jnp.int32).at[flat_e].add(1)
    padded = (counts + MOE_BLOCK - 1) // MOE_BLOCK * MOE_BLOCK
    pad_end = jnp.cumsum(padded)
    pad_start = pad_end - padded
    start = jnp.cumsum(counts) - counts
    dest = pad_start[e_sorted] + jnp.arange(n_assign, dtype=jnp.int32) - start[e_sorted]
    n_rows = -(-(n_assign + N_EXPERTS * (MOE_BLOCK - 1)) // MOE_BLOCK) * MOE_BLOCK
    n_blocks = n_rows // MOE_BLOCK
    row_tok = jnp.zeros((n_rows,), jnp.int32).at[dest].set(order // TOP_K)
    row_gate = jnp.zeros((n_rows,), jnp.float32).at[dest].set(gate.reshape(n_assign)[order])
    block_expert = jnp.minimum(
        jnp.searchsorted(pad_end, jnp.arange(n_blocks, dtype=jnp.int32) * MOE_BLOCK, side='right'),
        N_EXPERTS - 1)

    def expert_block(acc, blk):
        tok, gw, e = blk
        gu = jnp.dot(h[tok], w_gate_up[e])
        hid = jax.nn.silu(gu[:, :D_EXPERT]) * gu[:, D_EXPERT:]
        y = jnp.dot(hid, w_down[e]).astype(jnp.float32)
        return acc.at[tok].add(y * gw[:, None]), None

    acc, _ = lax.scan(expert_block, jnp.zeros((n_tok, d), jnp.float32),
                      (row_tok.reshape(n_blocks, MOE_BLOCK), row_gate.reshape(n_blocks, MOE_BLOCK),
                       block_expert))
    return acc.astype(h.dtype)


def setup_inputs(seed: int = 0) -> dict:
    key = jax.random.key(seed)
    ks = jax.random.split(key, 32)
    f32 = jnp.float32
    n_even, n_odd = (DEPTH + 1) // 2, DEPTH // 2

    def nrm(i, shape, scale):
        return jax.random.normal(ks[i], shape, f32) * scale

    def gain(i, shape):
        return 1.0 + nrm(i, shape, 0.02)

    dt = jnp.exp(jax.random.uniform(ks[9], (2, n_even, GDN_HEADS), f32, math.log(1e-3), math.log(1e-1)))
    dt_bias = jnp.log(jnp.expm1(dt))
    a_log = jnp.log(jax.random.uniform(ks[10], (2, n_even, GDN_HEADS), f32, 1.0, 16.0))
    return {
        'x': nrm(0, (BATCH, SEQ, D_MODEL), 1.0),
        'c': nrm(1, (BATCH, D_MODEL), 1.0),
        'ctx': nrm(2, (BATCH, CTX_LEN, D_MODEL), 1.0),
        'c_ctx': nrm(3, (D_MODEL,), 1.0),
        'w_mod': nrm(4, (DEPTH, D_MODEL, 6 * D_MODEL), 0.5 * D_MODEL ** -0.5),
        'b_mod': nrm(5, (DEPTH, 6 * D_MODEL), 0.02),
        'norm_mix_g': gain(6, (DEPTH, D_MODEL)),
        'norm_ffn_g': gain(7, (DEPTH, D_MODEL)),
        'w_in': nrm(8, (n_even, D_MODEL, IN_DIM), D_MODEL ** -0.5),
        'conv_qkv': nrm(11, (n_even, CONV_W, OFF_Z), CONV_W ** -0.5),
        'a_log_fwd': a_log[0],
        'a_log_bwd': a_log[1],
        'dt_bias_fwd': dt_bias[0],
        'dt_bias_bwd': dt_bias[1],
        'gdn_out_g': gain(12, (n_even, GDN_DV)),
        'q_a_norm_g': gain(13, (n_even, Q_LORA)),
        'w_uq': nrm(14, (n_even, Q_LORA, MLA_HEADS * QK_HEAD), Q_LORA ** -0.5),
        'kv_a_norm_g': gain(15, (n_even, KV_LORA)),
        'w_ukv': nrm(16, (n_even, KV_LORA, MLA_HEADS * (NOPE_DIM + MLA_V)), KV_LORA ** -0.5),
        'q_norm_g': gain(17, (n_even, QK_HEAD)),
        'k_norm_g': gain(18, (n_even, QK_HEAD)),
        'w_out': nrm(19, (n_even, MIX_DIM, D_MODEL), MIX_DIM ** -0.5),
        'w_pool': nrm(20, (n_odd, len(POOL_WINDOWS), POOL_GROUP, POOL_GROUP), POOL_GROUP ** -0.5),
        'pool_scale': 1.0 + nrm(21, (n_odd, D_MODEL), 0.1),
        'w_router': nrm(22, (D_MODEL, N_EXPERTS), D_MODEL ** -0.5),
        'router_bias': nrm(23, (N_EXPERTS,), 0.01),
        'w_gate_up': nrm(24, (DEPTH, N_EXPERTS, D_MODEL, 2 * D_EXPERT), D_MODEL ** -0.5),
        'w_down': nrm(25, (DEPTH, N_EXPERTS, D_EXPERT, D_MODEL), D_EXPERT ** -0.5),
    }


def reference(x, c, ctx, c_ctx, w_mod, b_mod, norm_mix_g, norm_ffn_g, w_in, conv_qkv,
              a_log_fwd, a_log_bwd, dt_bias_fwd, dt_bias_bwd, gdn_out_g, q_a_norm_g, w_uq,
              kv_a_norm_g, w_ukv, q_norm_g, k_norm_g, w_out, w_pool, pool_scale,
              w_router, router_bias, w_gate_up, w_down):
    bsz, seq, d = x.shape
    rows = seq // GRID_W
    cos, sin = axial_rope_tables(rows)
    silu_c = jax.nn.silu(c.astype(jnp.float32))
    silu_cc = jax.nn.silu(c_ctx.astype(jnp.float32))
    for i in range(DEPTH):
        j = i // 2
        reads_ctx = i % 2 == 0
        ctx_update = any(l % 2 == 0 for l in range(i + 1, DEPTH))
        mod = (jnp.dot(silu_c, w_mod[i].astype(jnp.float32)) + b_mod[i]).astype(x.dtype)[:, None, :]
        sh1, sc1, g1, sh2, sc2, g2 = jnp.split(mod, 6, axis=-1)
        if reads_ctx or ctx_update:
            mod_c = (jnp.dot(silu_cc, w_mod[i].astype(jnp.float32)) + b_mod[i]).astype(ctx.dtype)
            csh1, csc1, cg1, csh2, csc2, cg2 = jnp.split(mod_c, 6)
            hc = modulate(rms_norm(ctx, norm_mix_g[i]), csh1, csc1)
        h = modulate(rms_norm(x, norm_mix_g[i]), sh1, sc1)
        if i % 2 == 0:
            y, yc = hybrid_mixer(h, hc, w_in[j], conv_qkv[j], a_log_fwd[j], a_log_bwd[j],
                                 dt_bias_fwd[j], dt_bias_bwd[j], gdn_out_g[j], q_a_norm_g[j], w_uq[j],
                                 kv_a_norm_g[j], w_ukv[j], q_norm_g[j], k_norm_g[j], w_out[j],
                                 cos, sin, ctx_update)
        else:
            y = multiscale_pool(h, w_pool[j], pool_scale[j])
            yc = multiscale_pool(hc, w_pool[j], pool_scale[j]) if ctx_update else None
        x = x + g1 * y
        h2 = modulate(rms_norm(x, norm_ffn_g[i]), sh2, sc2)
        x = x + g2 * moe_ffn(h2.reshape(bsz * seq, d), w_router, router_bias,
                             w_gate_up[i], w_down[i]).reshape(bsz, seq, d)
        if ctx_update:
            ctx = ctx + cg1 * yc
            hc2 = modulate(rms_norm(ctx, norm_ffn_g[i]), csh2, csc2)
            ctx = ctx + cg2 * moe_ffn(hc2.reshape(-1, d), w_router, router_bias,
                                      w_gate_up[i], w_down[i]).reshape(ctx.shape)
    return x
```

```python
import functools
import math

import jax
import jax.numpy as jnp
from jax import lax
from jax.experimental import pallas as pl
from jax.experimental.pallas import tpu as pltpu

F32 = jnp.float32
BF16 = jnp.bfloat16
U32 = jnp.uint32
I32 = jnp.int32

EPS = 1e-6
GRID_W = 64
GDN_HEADS = 8
GDN_DK = 128
MLA_HEADS = 8
NOPE_DIM = 128
ROPE_DIM = 64
QK_HEAD = NOPE_DIM + ROPE_DIM
QK_PAD = 256
ROPE_THETA = 10000.0
Q_LORA = 512
KV_LORA = 256
POOL_WINDOWS = (2, 4, 8, 16)
N_EXPERTS = 64
EXPERTS_PER_GROUP = 8
N_GROUPS = N_EXPERTS // EXPERTS_PER_GROUP
TOP_K = 2

LANE = 128
V7X_VMEM_LIMIT = 56 * 1024 * 1024

SEQ_TILE = 256
GDN_CHUNK = 128
MOE_ROWS = 256
ROUTE_TILE = 512

P_QKV = 0
P_Z = 3072
P_CQ = 4096
P_CKV = 4608
P_KR = 4864
P_AB = 4992
P_WIDTH = 5376
P_NTILE = 1792


def _params(sem):
    return pltpu.CompilerParams(dimension_semantics=sem, vmem_limit_bytes=V7X_VMEM_LIMIT)


def _dot(a, b):
    return jnp.dot(a, b, preferred_element_type=F32)


def _dot_nt(a, b):
    return lax.dot_general(a, b, (((1,), (1,)), ((), ())), preferred_element_type=F32)


def _dot_tn(a, b):
    return lax.dot_general(a, b, (((0,), (0,)), ((), ())), preferred_element_type=F32)


def _silu(x):
    return x * jax.nn.sigmoid(x)


def _rms(x, g):
    return x * lax.rsqrt(jnp.mean(x * x, axis=-1, keepdims=True) + EPS) * g


def _split3(x):
    hi = x.astype(BF16)
    r = x - hi.astype(F32)
    mid = r.astype(BF16)
    lo = (r - mid.astype(F32)).astype(BF16)
    return hi, mid, lo


def _mod_kernel(c_ref, w_ref, b_ref, o_ref):
    s = _silu(c_ref[...]).astype(BF16)
    o_ref[...] = _dot(s, w_ref[...].astype(BF16)) + b_ref[...]


def _modulation(cc, w_mod, b_mod):
    depth, d, n6 = w_mod.shape
    tn = 1024
    return pl.pallas_call(
        _mod_kernel,
        grid=(depth, n6 // tn),
        in_specs=[pl.BlockSpec((8, d), lambda l, j: (0, 0)),
                  pl.BlockSpec((None, d, tn), lambda l, j: (l, 0, j)),
                  pl.BlockSpec((None, 1, tn), lambda l, j: (l, 0, j))],
        out_specs=pl.BlockSpec((None, 8, tn), lambda l, j: (l, 0, j)),
        out_shape=jax.ShapeDtypeStruct((depth, 8, n6), F32),
        compiler_params=_params(("parallel", "parallel")),
    )(cc, w_mod, b_mod.reshape(depth, 1, n6))


def _inproj_kernel(nct, xc_ref, x_ref, g_ref, shc_ref, scc_ref, sh_ref, sc_ref, w_ref, o_ref):
    i = pl.program_id(2)

    def run(xr, shr, scr):
        h = _rms(xr[...], g_ref[...]) * (1.0 + scr[...]) + shr[...]
        o_ref[...] = _dot(h.astype(BF16), w_ref[...]).astype(o_ref.dtype)

    @pl.when(i < nct)
    def _():
        run(xc_ref, shc_ref, scc_ref)

    @pl.when(i >= nct)
    def _():
        run(x_ref, sh_ref, sc_ref)


def _in_projection(ctx, x, g, shc, scc, sh, sc, w):
    bsz, seq, d = x.shape
    tm = SEQ_TILE
    nct = ctx.shape[1] // tm
    nt = nct + seq // tm
    width = w.shape[1]
    tn = P_NTILE
    return pl.pallas_call(
        functools.partial(_inproj_kernel, nct),
        grid=(width // tn, bsz, nt),
        in_specs=[pl.BlockSpec((None, tm, d), lambda j, b, i: (b, jnp.minimum(i, nct - 1), 0)),
                  pl.BlockSpec((None, tm, d), lambda j, b, i: (b, jnp.maximum(i - nct, 0), 0)),
                  pl.BlockSpec((1, d), lambda j, b, i: (0, 0)),
                  pl.BlockSpec((1, d), lambda j, b, i: (0, 0)),
                  pl.BlockSpec((1, d), lambda j, b, i: (0, 0)),
                  pl.BlockSpec((None, 1, d), lambda j, b, i: (b, 0, 0)),
                  pl.BlockSpec((None, 1, d), lambda j, b, i: (b, 0, 0)),
                  pl.BlockSpec((d, tn), lambda j, b, i: (0, j))],
        out_specs=pl.BlockSpec((None, tm, tn), lambda j, b, i: (b, i, j)),
        out_shape=jax.ShapeDtypeStruct((bsz, nt * tm, width), BF16),
        compiler_params=_params(("parallel", "parallel", "parallel")),
    )(ctx, x, g, shc, scc, sh, sc, w)


GDN_HALO = 16


def _gdn_front_kernel(nct, pm_ref, pp_ref, pn_ref, cw_ref, ab0_ref, ab1_ref, gp_ref,
                      qkv_ref, gb_ref, ext):
    i = pl.program_id(1)
    n = pl.num_programs(1)
    tm = pm_ref.shape[0]
    hl = GDN_HALO
    zero_prev = jnp.logical_or(i == 0, i == nct)
    zero_next = jnp.logical_or(i == nct - 1, i == n - 1)
    ext[hl:hl + tm, :] = pm_ref[...].astype(F32)
    ext[0:hl, :] = jnp.where(zero_prev, 0.0, pp_ref[...].astype(F32))
    ext[hl + tm:2 * hl + tm, :] = jnp.where(zero_next, 0.0, pn_ref[...].astype(F32))
    taps = cw_ref.shape[0]
    nqk = 2 * GDN_HEADS
    for cb in range(qkv_ref.shape[1] // LANE):
        cs = slice(cb * LANE, (cb + 1) * LANE)
        acc = None
        for j in range(taps):
            off = hl - taps // 2 + j
            term = cw_ref[j:j + 1, cs] * ext[off:off + tm, cs]
            acc = term if acc is None else acc + term
        y = _silu(acc)
        if cb < nqk:
            inv = lax.rsqrt(jnp.sum(y * y, axis=-1, keepdims=True) + EPS)
            if cb < GDN_HEADS:
                inv = inv * (GDN_DK ** -0.5)
            y = y * inv
        qkv_ref[:, cs] = y.astype(BF16)
    for d, ab_ref in enumerate((ab0_ref, ab1_ref)):
        a = ab_ref[...].astype(F32)
        lane = lax.broadcasted_iota(I32, a.shape, 1)
        neg_decay = -jnp.exp(gp_ref[d, 0:1, :])
        xx = a + gp_ref[d, 1:2, :]
        softplus = jnp.maximum(xx, 0.0) + jnp.log1p(jnp.exp(-jnp.abs(xx)))
        gate = jnp.where(lane < GDN_HEADS, neg_decay * softplus, jax.nn.sigmoid(a))
        gb_ref[d] = jnp.where(lane < 2 * GDN_HEADS, gate, 0.0)


def _gdn_front(p_all, conv_w, gate_params, nct):
    bsz, ttot, _ = p_all.shape
    tm = SEQ_TILE
    nt = ttot // tm
    cq = P_Z
    hl = GDN_HALO
    r = tm // hl
    nh = ttot // hl
    return pl.pallas_call(
        functools.partial(_gdn_front_kernel, nct),
        grid=(bsz, nt),
        in_specs=[pl.BlockSpec((None, tm, cq), lambda b, i: (b, i, 0)),
                  pl.BlockSpec((None, hl, cq), lambda b, i: (b, jnp.maximum(i * r - 1, 0), 0)),
                  pl.BlockSpec((None, hl, cq), lambda b, i: (b, jnp.minimum((i + 1) * r, nh - 1), 0)),
                  pl.BlockSpec(conv_w.shape, lambda b, i: (0, 0)),
                  pl.BlockSpec((None, tm, LANE), lambda b, i: (b, i, P_AB // LANE)),
                  pl.BlockSpec((None, tm, LANE), lambda b, i: (b, i, P_AB // LANE + 1)),
                  pl.BlockSpec(gate_params.shape, lambda b, i: (0, 0, 0))],
        out_specs=[pl.BlockSpec((None, tm, cq), lambda b, i: (b, i, 0)),
                   pl.BlockSpec((2, None, tm, LANE), lambda b, i: (0, b, i, 0))],
        out_shape=[jax.ShapeDtypeStruct((bsz, ttot, cq), BF16),
                   jax.ShapeDtypeStruct((2, bsz, ttot, LANE), F32)],
        scratch_shapes=[pltpu.VMEM((tm + 2 * hl, cq), F32)],
        compiler_params=_params(("parallel", "parallel")),
    )(p_all, p_all, p_all, conv_w, p_all, p_all, gate_params)


def _gdn_scan_kernel(nct, q_ref, k_ref, v_ref, gb_ref, o_ref, s_scr):
    d = pl.program_id(1)
    s = pl.program_id(2)
    c = q_ref.shape[0]
    dk = GDN_DK

    @pl.when(s == 0)
    def _():
        s_scr[...] = jnp.zeros_like(s_scr)

    ri = lax.broadcasted_iota(I32, (c, c), 0)
    ci = lax.broadcasted_iota(I32, (c, c), 1)
    incl = jnp.where(d == 0, ri - ci, ci - ri) >= 0
    strict = jnp.logical_and(incl, ci != ri)
    gb = gb_ref[...]
    lane = lax.broadcasted_iota(I32, gb.shape, 1)
    g_hi, g_mid, g_lo = _split3(jnp.where(lane < GDN_HEADS, gb, 0.0))
    m_incl = incl.astype(BF16)
    ones = jnp.ones((c, c), BF16)
    gc = _dot(m_incl, g_hi) + _dot(m_incl, g_mid) + _dot(m_incl, g_lo)
    gl = _dot(ones, g_hi) + _dot(ones, g_mid) + _dot(ones, g_lo)
    gc_t = gc.T
    e_gc = jnp.exp(gc)
    e_rest = jnp.exp(gl - gc)
    e_all = jnp.exp(gl)
    eye = (ri == ci).astype(F32)
    pair_masks = []
    for lb in range(int(math.log2(c))):
        same_pair = jnp.right_shift(ri, lb + 1) == jnp.right_shift(ci, lb + 1)
        same_block = jnp.right_shift(ri, lb) == jnp.right_shift(ci, lb)
        pair_masks.append(jnp.logical_and(same_pair, jnp.logical_not(same_block)))
    outs = []
    for h in range(GDN_HEADS):
        hs = slice(h * dk, (h + 1) * dk)
        qb = q_ref[:, hs]
        kb16 = k_ref[:, hs]
        k = kb16.astype(F32)
        v = v_ref[:, hs].astype(F32)
        beta = gb[:, GDN_HEADS + h:GDN_HEADS + h + 1]
        diff = gc[:, h:h + 1] - gc_t[h:h + 1, :]
        dec = jnp.where(incl, jnp.exp(jnp.where(incl, diff, 0.0)), 0.0)
        kbeta = k * beta
        low = jnp.where(strict, _dot_nt(kbeta.astype(BF16), kb16) * dec, 0.0)
        r = -jnp.where(pair_masks[0], low, 0.0)
        for mask in pair_masks[1:]:
            dmat = (r + eye).astype(BF16)
            cd = _dot(jnp.where(mask, low, 0.0).astype(BF16), dmat)
            r = r - _dot(dmat, cd.astype(BF16))
        rhs = jnp.concatenate([v * beta, kbeta * e_gc[:, h:h + 1]], axis=1)
        uw = rhs + _dot(r.astype(BF16), rhs.astype(BF16))
        uw16 = uw.astype(BF16)
        kd = k * e_rest[:, h:h + 1]
        kt = _dot_tn(kd.astype(BF16), uw16)
        qk = _dot_nt(qb, kb16) * dec
        qu = _dot(qk.astype(BF16), uw16)
        st = s_scr[h]
        st16 = st.astype(BF16)
        qeff = qb.astype(F32) * e_gc[:, h:h + 1] - qu[:, dk:]
        outs.append(_dot(qeff.astype(BF16), st16) + qu[:, :dk])
        s_scr[h] = st * e_all[0:1, h:h + 1] + kt[:, :dk] - _dot(kt[:, dk:].astype(BF16), st16)

    @pl.when(s >= nct)
    def _():
        for h in range(GDN_HEADS):
            o_ref[:, h * dk:(h + 1) * dk] = outs[h]


def _gdn_scan(qkv_all, gb_all, nct_tokens):
    bsz, ttot, _ = qkv_all.shape
    c = GDN_CHUNK
    nct = nct_tokens // c
    ns = ttot // c
    nlat = ns - nct
    hv = GDN_HEADS * GDN_DK

    def tmap(d, s):
        rev = jnp.where(s < nct, nct - 1 - s, 2 * nct + nlat - 1 - s)
        return jnp.where(d == 0, s, rev)

    def omap(d, s):
        first = jnp.where(d == 0, 0, nlat - 1)
        return jnp.where(s < nct, first, tmap(d, s) - nct)

    return pl.pallas_call(
        functools.partial(_gdn_scan_kernel, nct),
        grid=(bsz, 2, ns),
        in_specs=[pl.BlockSpec((None, c, hv), lambda b, d, s: (b, tmap(d, s), 0)),
                  pl.BlockSpec((None, c, hv), lambda b, d, s: (b, tmap(d, s), 1)),
                  pl.BlockSpec((None, c, hv), lambda b, d, s: (b, tmap(d, s), 2)),
                  pl.BlockSpec((None, None, c, LANE), lambda b, d, s: (d, b, tmap(d, s), 0))],
        out_specs=pl.BlockSpec((None, None, c, hv), lambda b, d, s: (d, b, omap(d, s), 0)),
        out_shape=jax.ShapeDtypeStruct((2, bsz, nlat * c, hv), F32),
        scratch_shapes=[pltpu.VMEM((GDN_HEADS, GDN_DK, GDN_DK), F32)],
        compiler_params=_params(("parallel", "parallel", "arbitrary")),
    )(qkv_all, qkv_all, qkv_all, gb_all)


def _rope_tile(xr, cos, sin):
    lane = lax.broadcasted_iota(I32, xr.shape, 1)
    first_half = (lane % 32) < 16
    rot = jnp.where(first_half, -pltpu.roll(xr, LANE - 16, 1), pltpu.roll(xr, 16, 1))
    return xr * cos + rot * sin


def _mla_q_kernel(c_ref, ga_ref, w_ref, gn_ref, gr_ref, cos_ref, sin_ref, q_ref):
    cn = _rms(c_ref[...].astype(F32), ga_ref[...]).astype(BF16)
    qf = _dot(cn, w_ref[...])
    tm = qf.shape[0]
    lane = lax.broadcasted_iota(I32, (tm, LANE), 1)
    left = lane < ROPE_DIM
    rope_base = MLA_HEADS * NOPE_DIM
    scale = QK_HEAD ** -0.5
    cos = cos_ref[...]
    sin = sin_ref[...]
    for hp in range(MLA_HEADS // 2):
        rt = qf[:, rope_base + hp * LANE:rope_base + (hp + 1) * LANE]
        rsq = rt * rt
        ss_left = jnp.sum(jnp.where(left, rsq, 0.0), axis=-1, keepdims=True)
        ss_right = jnp.sum(jnp.where(left, 0.0, rsq), axis=-1, keepdims=True)
        invs = []
        for par, ss_r in ((0, ss_left), (1, ss_right)):
            h = 2 * hp + par
            nope = qf[:, h * NOPE_DIM:(h + 1) * NOPE_DIM]
            ss = jnp.sum(nope * nope, axis=-1, keepdims=True) + ss_r
            inv = lax.rsqrt(ss * (1.0 / QK_HEAD) + EPS)
            invs.append(inv)
            q_ref[h, :, 0:NOPE_DIM] = (nope * inv * gn_ref[...] * scale).astype(BF16)
        inv_lane = jnp.where(left, invs[0], invs[1])
        xr = _rope_tile(rt * inv_lane * gr_ref[...], cos, sin) * scale
        q_ref[2 * hp, :, NOPE_DIM:QK_PAD] = jnp.where(left, xr, 0.0).astype(BF16)
        q_ref[2 * hp + 1, :, NOPE_DIM:QK_PAD] = jnp.where(left, pltpu.roll(xr, ROPE_DIM, 1), 0.0).astype(BF16)


def _mla_queries(p_all, ga, w, gn, gr, cos, sin, nct_tiles, seq):
    bsz = p_all.shape[0]
    tm = SEQ_TILE
    return pl.pallas_call(
        _mla_q_kernel,
        grid=(bsz, seq // tm),
        in_specs=[pl.BlockSpec((None, tm, Q_LORA), lambda b, i: (b, i + nct_tiles, P_CQ // Q_LORA)),
                  pl.BlockSpec(ga.shape, lambda b, i: (0, 0)),
                  pl.BlockSpec(w.shape, lambda b, i: (0, 0)),
                  pl.BlockSpec(gn.shape, lambda b, i: (0, 0)),
                  pl.BlockSpec(gr.shape, lambda b, i: (0, 0)),
                  pl.BlockSpec((tm, LANE), lambda b, i: (i + nct_tiles, 0)),
                  pl.BlockSpec((tm, LANE), lambda b, i: (i + nct_tiles, 0))],
        out_specs=pl.BlockSpec((None, MLA_HEADS, tm, QK_PAD), lambda b, i: (b, 0, i, 0)),
        out_shape=jax.ShapeDtypeStruct((bsz, MLA_HEADS, seq, QK_PAD), BF16),
        compiler_params=_params(("parallel", "parallel")),
    )(p_all, ga, w, gn, gr, cos, sin)


def _mla_kv_kernel(c_ref, kr_ref, ga_ref, w_ref, gn_ref, gr_ref, cos_ref, sin_ref, k_ref, v_ref):
    cn = _rms(c_ref[...].astype(F32), ga_ref[...]).astype(BF16)
    kv = _dot(cn, w_ref[...])
    kr = kr_ref[...].astype(F32)
    ss_r = jnp.sum(kr * kr, axis=-1, keepdims=True)
    kr_rot = _rope_tile(kr * gr_ref[...], cos_ref[...], sin_ref[...])
    width = NOPE_DIM + LANE
    for h in range(MLA_HEADS):
        nope = kv[:, h * width:h * width + NOPE_DIM]
        ss = jnp.sum(nope * nope, axis=-1, keepdims=True) + ss_r
        inv = lax.rsqrt(ss * (1.0 / QK_HEAD) + EPS)
        k_ref[h, :, 0:NOPE_DIM] = (nope * inv * gn_ref[...]).astype(BF16)
        k_ref[h, :, NOPE_DIM:QK_PAD] = (kr_rot * inv).astype(BF16)
        v_ref[h] = kv[:, h * width + NOPE_DIM:(h + 1) * width].astype(BF16)


def _mla_keys_values(p_all, ga, w, gn, gr, cos, sin):
    bsz, ttot, _ = p_all.shape
    tm = SEQ_TILE
    return pl.pallas_call(
        _mla_kv_kernel,
        grid=(bsz, ttot // tm),
        in_specs=[pl.BlockSpec((None, tm, KV_LORA), lambda b, i: (b, i, P_CKV // KV_LORA)),
                  pl.BlockSpec((None, tm, LANE), lambda b, i: (b, i, P_KR // LANE)),
                  pl.BlockSpec(ga.shape, lambda b, i: (0, 0)),
                  pl.BlockSpec(w.shape, lambda b, i: (0, 0)),
                  pl.BlockSpec(gn.shape, lambda b, i: (0, 0)),
                  pl.BlockSpec(gr.shape, lambda b, i: (0, 0)),
                  pl.BlockSpec((tm, LANE), lambda b, i: (i, 0)),
                  pl.BlockSpec((tm, LANE), lambda b, i: (i, 0))],
        out_specs=[pl.BlockSpec((None, MLA_HEADS, tm, QK_PAD), lambda b, i: (b, 0, i, 0)),
                   pl.BlockSpec((None, MLA_HEADS, tm, LANE), lambda b, i: (b, 0, i, 0))],
        out_shape=[jax.ShapeDtypeStruct((bsz, MLA_HEADS, ttot, QK_PAD), BF16),
                   jax.ShapeDtypeStruct((bsz, MLA_HEADS, ttot, LANE), BF16)],
        compiler_params=_params(("parallel", "parallel")),
    )(p_all, p_all, ga, w, gn, gr, cos, sin)


def _flash_kernel(tk, q_ref, k_ref, v_ref, o_ref, m_scr, l_scr, acc_scr):
    nk = k_ref.shape[0] // tk
    q = q_ref[...]
    m_scr[...] = jnp.full_like(m_scr, -jnp.inf)
    l_scr[...] = jnp.zeros_like(l_scr)
    acc_scr[...] = jnp.zeros_like(acc_scr)

    def body(j, carry):
        start = pl.multiple_of(j * tk, tk)
        s = _dot_nt(q, k_ref[pl.ds(start, tk), :])
        m_prev = m_scr[...]
        m_new = jnp.maximum(m_prev, jnp.max(s, axis=-1, keepdims=True))
        p = jnp.exp(s - m_new)
        alpha = jnp.exp(m_prev - m_new)
        l_scr[...] = alpha * l_scr[...] + jnp.sum(p, axis=-1, keepdims=True)
        acc_scr[...] = alpha * acc_scr[...] + _dot(p.astype(BF16), v_ref[pl.ds(start, tk), :])
        m_scr[...] = m_new
        return carry

    lax.fori_loop(0, nk, body, 0)
    o_ref[...] = (acc_scr[...] / l_scr[...]).astype(o_ref.dtype)


def _key_tile(ttot):
    for cand in (1280, 1024, 768, 512, 256, 128):
        if ttot % cand == 0:
            return cand
    raise ValueError("key length must be a multiple of 128")


def _attention(q, k, v):
    bsz, heads, seq, _ = q.shape
    ttot = k.shape[2]
    tq = 256
    tk = _key_tile(ttot)
    return pl.pallas_call(
        functools.partial(_flash_kernel, tk),
        grid=(bsz, heads, seq // tq),
        in_specs=[pl.BlockSpec((None, None, tq, QK_PAD), lambda b, h, i: (b, h, i, 0)),
                  pl.BlockSpec((None, None, ttot, QK_PAD), lambda b, h, i: (b, h, 0, 0)),
                  pl.BlockSpec((None, None, ttot, LANE), lambda b, h, i: (b, h, 0, 0))],
        out_specs=pl.BlockSpec((None, tq, LANE), lambda b, h, i: (b, i, h)),
        out_shape=jax.ShapeDtypeStruct((bsz, seq, heads * LANE), BF16),
        scratch_shapes=[pltpu.VMEM((tq, 1), F32), pltpu.VMEM((tq, 1), F32), pltpu.VMEM((tq, LANE), F32)],
        compiler_params=_params(("parallel", "parallel", "arbitrary")),
    )(q, k, v)


def _pack_halves(x):
    w = x.shape[1] // 2
    bits = lax.bitcast_convert_type(x.astype(BF16).astype(F32), U32)
    return jnp.bitwise_or(jnp.right_shift(bits[:, :w], jnp.uint32(16)), bits[:, w:])


def _unpack_halves(wd):
    lo = lax.bitcast_convert_type(jnp.left_shift(wd, jnp.uint32(16)), F32)
    hi = lax.bitcast_convert_type(jnp.bitwise_and(wd, jnp.uint32(0xFFFF0000)), F32)
    return lo, hi


def _ffn_front(xn, gf_ref, sh_ref, sc_ref, wrh_ref, wrl_ref, hp_ref, lg_ref):
    h2 = _rms(xn, gf_ref[...]) * (1.0 + sc_ref[...]) + sh_ref[...]
    hh = h2.astype(BF16)
    hl = (h2 - hh.astype(F32)).astype(BF16)
    hp_ref[...] = _pack_halves(h2)
    lg_ref[...] = _dot(hh, wrh_ref[...]) + _dot(hl, wrh_ref[...]) + _dot(hh, wrl_ref[...])


def _mix_out_kernel(of_ref, ob_ref, z_ref, ym_ref, x_ref, gog_ref, wout_ref, g1_ref,
                    gf_ref, sh_ref, sc_ref, wrh_ref, wrl_ref, xo_ref, hp_ref, lg_ref, mix):
    dv = GDN_DK
    o = of_ref[...] + ob_ref[...]
    for h in range(GDN_HEADS):
        hs = slice(h * dv, (h + 1) * dv)
        y = _rms(o[:, hs], gog_ref[...])
        mix[:, hs] = (y * _silu(z_ref[:, hs].astype(F32))).astype(BF16)
    hv = GDN_HEADS * dv
    mix[:, hv:] = ym_ref[...]
    xn = x_ref[...] + g1_ref[...] * _dot(mix[...], wout_ref[...])
    xo_ref[...] = xn
    _ffn_front(xn, gf_ref, sh_ref, sc_ref, wrh_ref, wrl_ref, hp_ref, lg_ref)


def _mix_out(o2, p_all, ymla, x, gog, wout, g1, gf, sh2, sc2, wrh, wrl, nct_tiles):
    bsz, seq, d = x.shape
    tm = SEQ_TILE
    hv = GDN_HEADS * GDN_DK
    vec = lambda: pl.BlockSpec((None, 1, d), lambda b, i: (b, 0, 0))
    full = lambda a: pl.BlockSpec(a.shape, lambda b, i: (0,) * a.ndim)
    return pl.pallas_call(
        _mix_out_kernel,
        grid=(bsz, seq // tm),
        in_specs=[pl.BlockSpec((None, None, tm, hv), lambda b, i: (0, b, i, 0)),
                  pl.BlockSpec((None, None, tm, hv), lambda b, i: (1, b, i, 0)),
                  pl.BlockSpec((None, tm, hv), lambda b, i: (b, i + nct_tiles, P_Z // hv)),
                  pl.BlockSpec((None, tm, hv), lambda b, i: (b, i, 0)),
                  pl.BlockSpec((None, tm, d), lambda b, i: (b, i, 0)),
                  full(gog), full(wout), vec(), full(gf), vec(), vec(), full(wrh), full(wrl)],
        out_specs=[pl.BlockSpec((None, tm, d), lambda b, i: (b, i, 0)),
                   pl.BlockSpec((None, tm, d // 2), lambda b, i: (b, i, 0)),
                   pl.BlockSpec((None, tm, LANE), lambda b, i: (b, i, 0))],
        out_shape=[jax.ShapeDtypeStruct((bsz, seq, d), F32),
                   jax.ShapeDtypeStruct((bsz, seq, d // 2), U32),
                   jax.ShapeDtypeStruct((bsz, seq, LANE), F32)],
        scratch_shapes=[pltpu.VMEM((tm, wout.shape[0]), BF16)],
        compiler_params=_params(("parallel", "parallel")),
    )(o2, o2, p_all, ymla, x, gog, wout, g1, gf, sh2, sc2, wrh, wrl)


POOL_HALO = 8


def _pool_kernel(seq, xm_ref, xp_ref, xn_ref, gm_ref, sh1_ref, sc1_ref, wp_ref, ps_ref, g1_ref,
                 gf_ref, sh_ref, sc_ref, wrh_ref, wrl_ref, xo_ref, hp_ref, lg_ref, ext):
    i = pl.program_id(1)
    n = pl.num_programs(1)
    tm = xm_ref.shape[0]
    hl = POOL_HALO

    def normed(ref):
        return _rms(ref[...], gm_ref[...]) * (1.0 + sc1_ref[...]) + sh1_ref[...]

    ext[hl:hl + tm, :] = normed(xm_ref)
    ext[0:hl, :] = jnp.where(i == 0, 0.0, normed(xp_ref))
    ext[hl + tm:2 * hl + tm, :] = jnp.where(i == n - 1, 0.0, normed(xn_ref))
    t = i * tm + lax.broadcasted_iota(I32, (tm, 1), 0)
    gw = xm_ref.shape[1] // len(POOL_WINDOWS)
    for gi, win in enumerate(POOL_WINDOWS):
        cs = slice(gi * gw, (gi + 1) * gw)
        half = win // 2
        acc = None
        for off in range(-half, win - half):
            term = ext[hl + off:hl + off + tm, cs]
            acc = term if acc is None else acc + term
        lo = jnp.clip(t - half, 0, seq)
        hi = jnp.clip(t - half + win, 0, seq)
        pooled = acc / (hi - lo).astype(F32) - ext[hl:hl + tm, cs]
        y = _dot(pooled.astype(BF16), wp_ref[gi]) * ps_ref[:, cs]
        xo_ref[:, cs] = xm_ref[:, cs] + g1_ref[:, cs] * y
    _ffn_front(xo_ref[...], gf_ref, sh_ref, sc_ref, wrh_ref, wrl_ref, hp_ref, lg_ref)


def _pool_mixer(x, gm, sh1, sc1, wp, ps, g1, gf, sh2, sc2, wrh, wrl):
    bsz, seq, d = x.shape
    tm = SEQ_TILE
    hl = POOL_HALO
    r = tm // hl
    nh = seq // hl
    vec = lambda: pl.BlockSpec((None, 1, d), lambda b, i: (b, 0, 0))
    full = lambda a: pl.BlockSpec(a.shape, lambda b, i: (0,) * a.ndim)
    return pl.pallas_call(
        functools.partial(_pool_kernel, seq),
        grid=(bsz, seq // tm),
        in_specs=[pl.BlockSpec((None, tm, d), lambda b, i: (b, i, 0)),
                  pl.BlockSpec((None, hl, d), lambda b, i: (b, jnp.maximum(i * r - 1, 0), 0)),
                  pl.BlockSpec((None, hl, d), lambda b, i: (b, jnp.minimum((i + 1) * r, nh - 1), 0)),
                  full(gm), vec(), vec(), full(wp), full(ps), vec(),
                  full(gf), vec(), vec(), full(wrh), full(wrl)],
        out_specs=[pl.BlockSpec((None, tm, d), lambda b, i: (b, i, 0)),
                   pl.BlockSpec((None, tm, d // 2), lambda b, i: (b, i, 0)),
                   pl.BlockSpec((None, tm, LANE), lambda b, i: (b, i, 0))],
        out_shape=[jax.ShapeDtypeStruct((bsz, seq, d), F32),
                   jax.ShapeDtypeStruct((bsz, seq, d // 2), U32),
                   jax.ShapeDtypeStruct((bsz, seq, LANE), F32)],
        scratch_shapes=[pltpu.VMEM((tm + 2 * hl, d), F32)],
        compiler_params=_params(("parallel", "parallel")),
    )(x, x, x, gm, sh1, sc1, wp, ps, g1, gf, sh2, sc2, wrh, wrl)


def _first_max(vals, idx, sentinel):
    m = jnp.max(vals, axis=0, keepdims=True)
    first = jnp.min(jnp.where(vals == m, idx, sentinel), axis=0, keepdims=True)
    return m, first


def _route_kernel(lg_ref, bias_ref, ids_ref, gates_ref, cnt_ref, carry):
    step = pl.program_id(0)

    @pl.when(step == 0)
    def _():
        carry[...] = jnp.zeros_like(carry)

    tm = lg_ref.shape[0]
    epg = EXPERTS_PER_GROUP
    scores = jax.nn.sigmoid(lg_ref[...].T[0:N_EXPERTS, :])
    biased = scores + bias_ref[:, 0:1]
    eidx = lax.broadcasted_iota(I32, (epg, tm), 0)
    best_score = best_grp = best_i1 = best_i2 = None
    for g in range(N_GROUPS):
        blk = biased[g * epg:(g + 1) * epg, :]
        m1, i1 = _first_max(blk, eidx, epg)
        m2, i2 = _first_max(jnp.where(eidx == i1, -jnp.inf, blk), eidx, epg)
        gs = m1 + m2
        if g == 0:
            best_score, best_grp, best_i1, best_i2 = gs, jnp.zeros_like(i1), i1, i2
        else:
            better = gs > best_score
            best_score = jnp.where(better, gs, best_score)
            best_grp = jnp.where(better, g, best_grp)
            best_i1 = jnp.where(better, i1, best_i1)
            best_i2 = jnp.where(better, i2, best_i2)
    e1 = best_grp * epg + best_i1
    e2 = best_grp * epg + best_i2
    eall = lax.broadcasted_iota(I32, (N_EXPERTS, tm), 0)
    oh1 = eall == e1
    oh2 = eall == e2
    s1 = jnp.sum(jnp.where(oh1, scores, 0.0), axis=0, keepdims=True)
    s2 = jnp.sum(jnp.where(oh2, scores, 0.0), axis=0, keepdims=True)
    denom = s1 + s2
    oh = jnp.logical_or(oh1, oh2).astype(BF16)
    earlier = (lax.broadcasted_iota(I32, (tm, tm), 0) < lax.broadcasted_iota(I32, (tm, tm), 1)).astype(BF16)
    before = _dot(oh, earlier) + carry[:, 0:1]
    r1 = jnp.sum(jnp.where(oh1, before, 0.0), axis=0, keepdims=True)
    r2 = jnp.sum(jnp.where(oh2, before, 0.0), axis=0, keepdims=True)
    carry[...] = carry[...] + jnp.sum(oh.astype(F32), axis=1, keepdims=True)
    cnt_ref[...] = carry[...]
    zi = jnp.zeros((4, tm), I32)
    ids_ref[...] = jnp.concatenate([e1, e2, r1.astype(I32), r2.astype(I32), zi], axis=0)
    gates_ref[...] = jnp.concatenate([s1 / denom, s2 / denom, jnp.zeros((6, tm), F32)], axis=0)


def _route(logits, bias_col):
    n = logits.shape[0]
    tm = ROUTE_TILE
    return pl.pallas_call(
        _route_kernel,
        grid=(n // tm,),
        in_specs=[pl.BlockSpec((tm, LANE), lambda t: (t, 0)),
                  pl.BlockSpec(bias_col.shape, lambda t: (0, 0))],
        out_specs=[pl.BlockSpec((8, tm), lambda t: (0, t)),
                   pl.BlockSpec((8, tm), lambda t: (0, t)),
                   pl.BlockSpec((N_EXPERTS, LANE), lambda t: (0, 0))],
        out_shape=[jax.ShapeDtypeStruct((8, n), I32),
                   jax.ShapeDtypeStruct((8, n), F32),
                   jax.ShapeDtypeStruct((N_EXPERTS, LANE), F32)],
        scratch_shapes=[pltpu.VMEM((N_EXPERTS, LANE), F32)],
        compiler_params=_params(("arbitrary",)),
    )(logits, bias_col)


def _row_copy(src_hbm, dst_hbm, sem, src_row, dst_row):
    return pltpu.make_async_copy(src_hbm.at[pl.ds(src_row, 1), :], dst_hbm.at[pl.ds(dst_row, 1), :], sem)


def _dispatch_kernel(rows, idx_ref, src_hbm, dst_hbm, sem):
    base = pl.program_id(0) * rows

    def issue(j, carry):
        _row_copy(src_hbm, dst_hbm, sem, idx_ref[0, j], base + j).start()
        return carry

    lax.fori_loop(0, rows, issue, 0)

    def drain(j, carry):
        _row_copy(src_hbm, dst_hbm, sem, 0, base + j).wait()
        return carry

    lax.fori_loop(0, rows, drain, 0)


def _dispatch(src, row_idx):
    n_rows = row_idx.shape[0]
    rows = MOE_ROWS
    nb = n_rows // rows
    return pl.pallas_call(
        functools.partial(_dispatch_kernel, rows),
        grid=(nb,),
        in_specs=[pl.BlockSpec((None, 1, rows), lambda i: (i, 0, 0), memory_space=pltpu.SMEM),
                  pl.BlockSpec(memory_space=pl.ANY)],
        out_specs=pl.BlockSpec(memory_space=pl.ANY),
        out_shape=jax.ShapeDtypeStruct((n_rows, src.shape[1]), src.dtype),
        scratch_shapes=[pltpu.SemaphoreType.DMA(())],
        compiler_params=_params(("arbitrary",)),
    )(row_idx.reshape(nb, 1, rows), src)


def _experts_kernel(be_ref, xs_ref, gate_ref, wgu_ref, wd_ref, y_ref, wgu16, wd16):
    i = pl.program_id(0)
    prev = be_ref[jnp.maximum(i - 1, 0)]

    @pl.when(jnp.logical_or(i == 0, be_ref[i] != prev))
    def _():
        wgu16[...] = wgu_ref[...].astype(BF16)
        wd16[...] = wd_ref[...].astype(BF16)

    lo, hi = _unpack_halves(xs_ref[...])
    half = lo.shape[1]
    gu = _dot(lo.astype(BF16), wgu16[0:half, :]) + _dot(hi.astype(BF16), wgu16[half:, :])
    f = gu.shape[1] // 2
    hid = _silu(gu[:, :f]) * gu[:, f:]
    y = _dot(hid.astype(BF16), wd16[...]) * gate_ref[...]
    y_ref[...] = _pack_halves(y)


def _experts(layer, block_expert, xs, row_gate, w_gate_up, w_down):
    n_rows, half = xs.shape
    rows = MOE_ROWS
    d = 2 * half
    f2 = w_gate_up.shape[3]
    grid_spec = pltpu.PrefetchScalarGridSpec(
        num_scalar_prefetch=1,
        grid=(n_rows // rows,),
        in_specs=[pl.BlockSpec((rows, half), lambda i, be: (i, 0)),
                  pl.BlockSpec((rows, 1), lambda i, be: (i, 0)),
                  pl.BlockSpec((None, None, d, f2), lambda i, be: (layer, be[i], 0, 0)),
                  pl.BlockSpec((None, None, f2 // 2, d), lambda i, be: (layer, be[i], 0, 0))],
        out_specs=pl.BlockSpec((rows, half), lambda i, be: (i, 0)),
        scratch_shapes=[pltpu.VMEM((d, f2), BF16), pltpu.VMEM((f2 // 2, d), BF16)],
    )
    return pl.pallas_call(
        _experts_kernel,
        grid_spec=grid_spec,
        out_shape=jax.ShapeDtypeStruct((n_rows, half), U32),
        compiler_params=_params(("arbitrary",)),
    )(block_expert, xs, row_gate, w_gate_up, w_down)


def _combine_kernel(tm, d0_ref, d1_ref, y_hbm, x_ref, g2_ref, xo_ref, buf, sem):
    def issue(j, carry):
        pltpu.make_async_copy(y_hbm.at[pl.ds(d0_ref[0, j], 1), :], buf.at[0, pl.ds(j, 1), :], sem).start()
        pltpu.make_async_copy(y_hbm.at[pl.ds(d1_ref[0, j], 1), :], buf.at[1, pl.ds(j, 1), :], sem).start()
        return carry

    lax.fori_loop(0, tm, issue, 0)

    def drain(j, carry):
        pltpu.make_async_copy(y_hbm.at[pl.ds(0, 1), :], buf.at[0, pl.ds(j, 1), :], sem).wait()
        pltpu.make_async_copy(y_hbm.at[pl.ds(0, 1), :], buf.at[1, pl.ds(j, 1), :], sem).wait()
        return carry

    lax.fori_loop(0, tm, drain, 0)
    lo0, hi0 = _unpack_halves(buf[0])
    lo1, hi1 = _unpack_halves(buf[1])
    half = lo0.shape[1]
    xo_ref[:, 0:half] = x_ref[:, 0:half] + g2_ref[:, 0:half] * (lo0 + lo1)
    xo_ref[:, half:] = x_ref[:, half:] + g2_ref[:, half:] * (hi0 + hi1)


def _combine(y, dest, x, g2):
    bsz, seq, d = x.shape
    tm = SEQ_TILE
    nt = seq // tm
    half = d // 2
    idx = lambda k: pl.BlockSpec((None, None, 1, tm), lambda b, i: (k, b * nt + i, 0, 0),
                                 memory_space=pltpu.SMEM)
    dest4 = dest.reshape(2, bsz * nt, 1, tm)
    return pl.pallas_call(
        functools.partial(_combine_kernel, tm),
        grid=(bsz, nt),
        in_specs=[idx(0), idx(1),
                  pl.BlockSpec(memory_space=pl.ANY),
                  pl.BlockSpec((None, tm, d), lambda b, i: (b, i, 0)),
                  pl.BlockSpec((None, 1, d), lambda b, i: (b, 0, 0))],
        out_specs=pl.BlockSpec((None, tm, d), lambda b, i: (b, i, 0)),
        out_shape=jax.ShapeDtypeStruct((bsz, seq, d), F32),
        scratch_shapes=[pltpu.VMEM((2, tm, half), U32), pltpu.SemaphoreType.DMA(())],
        compiler_params=_params(("arbitrary", "arbitrary")),
    )(dest4, dest4, y, x, g2)


def _moe(layer, x, hp, logits, g2, bias_col, w_gate_up, w_down):
    bsz, seq, d = x.shape
    n = bsz * seq
    rows = MOE_ROWS
    ids, gates, cnt = _route(logits.reshape(n, LANE), bias_col)
    counts = cnt[:, 0].astype(I32)
    padded = (counts + rows - 1) // rows * rows
    pad_end = jnp.cumsum(padded)
    pad_start = pad_end - padded
    expert = ids[0:TOP_K]
    dest = pad_start[expert] + ids[TOP_K:2 * TOP_K]
    n_rows = -(-(n * TOP_K + N_EXPERTS * (rows - 1)) // rows) * rows
    tok = jnp.broadcast_to(jnp.arange(n, dtype=I32), (TOP_K, n))
    row_tok = jnp.zeros((n_rows,), I32).at[dest.reshape(-1)].set(tok.reshape(-1))
    row_gate = jnp.zeros((n_rows,), F32).at[dest.reshape(-1)].set(gates[0:TOP_K].reshape(-1))
    block_expert = jnp.minimum(
        jnp.searchsorted(pad_end, jnp.arange(n_rows // rows, dtype=I32) * rows, side='right'),
        N_EXPERTS - 1).astype(I32)
    xs = _dispatch(hp.reshape(n, d // 2), row_tok)
    y = _experts(layer, block_expert, xs, row_gate.reshape(n_rows, 1), w_gate_up, w_down)
    return _combine(y, dest, x, g2)


def _permute_w_in(w_in):
    d = w_in.shape[0]
    hq = GDN_HEADS * GDN_DK
    off_a = 4 * hq
    off_cq = off_a + 4 * GDN_HEADS
    off_ckv = off_cq + Q_LORA
    off_kr = off_ckv + KV_LORA
    z = lambda n: jnp.zeros((d, n), w_in.dtype)
    a = lambda k: w_in[:, off_a + k * GDN_HEADS:off_a + (k + 1) * GDN_HEADS]
    pad_ab = LANE - 2 * GDN_HEADS
    cols = [w_in[:, :off_a], w_in[:, off_cq:off_kr + ROPE_DIM], z(LANE - ROPE_DIM),
            a(0), a(2), z(pad_ab), a(1), a(3), z(pad_ab)]
    w = jnp.concatenate(cols, axis=1)
    return jnp.concatenate([w, z(P_WIDTH - w.shape[1])], axis=1).astype(BF16)


def _rope_tables(rows, n_ctx):
    row = jnp.repeat(jnp.arange(rows, dtype=F32), GRID_W)
    col = jnp.tile(jnp.arange(GRID_W, dtype=F32), rows)
    pairs = ROPE_DIM // 4
    inv_freq = ROPE_THETA ** (-jnp.arange(pairs, dtype=F32) / pairs)
    ang_r = row[:, None] * inv_freq
    ang_c = col[:, None] * inv_freq
    ang = jnp.concatenate([ang_r, ang_r, ang_c, ang_c], axis=-1)
    ang = jnp.concatenate([jnp.zeros((n_ctx, ROPE_DIM), F32), ang], axis=0)
    cos, sin = jnp.cos(ang), jnp.sin(ang)
    return jnp.concatenate([cos, cos], axis=1), jnp.concatenate([sin, sin], axis=1)


def kernel(x, c, ctx, c_ctx, w_mod, b_mod, norm_mix_g, norm_ffn_g, w_in, conv_qkv, a_log_fwd, a_log_bwd, dt_bias_fwd, dt_bias_bwd, gdn_out_g, q_a_norm_g, w_uq, kv_a_norm_g, w_ukv, q_norm_g, k_norm_g, w_out, w_pool, pool_scale, w_router, router_bias, w_gate_up, w_down):
    bsz, seq, d = x.shape
    n_ctx = ctx.shape[1]
    depth = w_mod.shape[0]
    assert depth == 2, "the context stream is only read: no layer after the first even layer reads it"
    nct_tiles = n_ctx // SEQ_TILE

    cc = jnp.concatenate([c, c_ctx[None, :], jnp.zeros((8 - bsz - 1, d), F32)], axis=0)
    mod = _modulation(cc, w_mod, b_mod).reshape(depth, 8, 6, d)

    def mods(layer):
        m = mod[layer]
        return [m[:bsz, k][:, None, :] for k in range(6)], [m[bsz:bsz + 1, k] for k in range(6)]

    wr = jnp.pad(w_router, ((0, 0), (0, LANE - N_EXPERTS)))
    wrh = wr.astype(BF16)
    wrl = (wr - wrh.astype(F32)).astype(BF16)
    bias_col = jnp.broadcast_to(router_bias.astype(F32)[:, None], (N_EXPERTS, LANE))
    row = lambda v: v.astype(F32)[None, :]

    for layer in range(depth):
        j = layer // 2
        (sh1, sc1, g1, sh2, sc2, g2), (csh1, csc1, _, _, _, _) = mods(layer)
        gm = row(norm_mix_g[layer])
        gf = row(norm_ffn_g[layer])
        if layer % 2 == 0:
            p_all = _in_projection(ctx, x, gm, csh1, csc1, sh1, sc1, _permute_w_in(w_in[j]))
            conv_w = conv_qkv[j].astype(F32)
            gate_params = jnp.zeros((2, 8, LANE), F32)
            gate_params = gate_params.at[0, 0, :GDN_HEADS].set(a_log_fwd[j]).at[0, 1, :GDN_HEADS].set(dt_bias_fwd[j])
            gate_params = gate_params.at[1, 0, :GDN_HEADS].set(a_log_bwd[j]).at[1, 1, :GDN_HEADS].set(dt_bias_bwd[j])
            qkv_all, gb_all = _gdn_front(p_all, conv_w, gate_params, nct_tiles)
            o2 = _gdn_scan(qkv_all, gb_all, n_ctx)

            cos, sin = _rope_tables(seq // GRID_W, n_ctx)
            gq = q_norm_g[j].astype(F32)
            gk = k_norm_g[j].astype(F32)
            wq = w_uq[j].reshape(Q_LORA, MLA_HEADS, QK_HEAD)
            wq = jnp.concatenate([wq[:, :, :NOPE_DIM].reshape(Q_LORA, -1),
                                  wq[:, :, NOPE_DIM:].reshape(Q_LORA, -1)], axis=1).astype(BF16)
            q = _mla_queries(p_all, row(q_a_norm_g[j]), wq, gq[None, :NOPE_DIM],
                             jnp.tile(gq[NOPE_DIM:], 2)[None, :], cos, sin, nct_tiles, seq)
            gk_rope = jnp.concatenate([gk[NOPE_DIM:], jnp.zeros((LANE - ROPE_DIM,), F32)])[None, :]
            k_all, v_all = _mla_keys_values(p_all, row(kv_a_norm_g[j]), w_ukv[j].astype(BF16),
                                            gk[None, :NOPE_DIM], gk_rope, cos, sin)
            ymla = _attention(q, k_all, v_all)
            x, hp, logits = _mix_out(o2, p_all, ymla, x, row(gdn_out_g[j]), w_out[j].astype(BF16), g1,
                                     gf, sh2, sc2, wrh, wrl, nct_tiles)
        else:
            x, hp, logits = _pool_mixer(x, gm, sh1, sc1, w_pool[j].astype(BF16), row(pool_scale[j]), g1,
                                        gf, sh2, sc2, wrh, wrl)
        x = _moe(layer, x, hp, logits, g2, bias_col, w_gate_up, w_down)
    return x
```

```python
import functools
import math

import jax
import jax.numpy as jnp
from jax import lax
from jax.experimental import pallas as pl
from jax.experimental.pallas import tpu as pltpu

F32 = jnp.float32
BF16 = jnp.bfloat16
U32 = jnp.uint32
I32 = jnp.int32

EPS = 1e-6
GRID_W = 64
GDN_HEADS = 8
GDN_DK = 128
MLA_HEADS = 8
NOPE_DIM = 128
ROPE_DIM = 64
QK_HEAD = NOPE_DIM + ROPE_DIM
QK_PAD = 256
ROPE_THETA = 10000.0
Q_LORA = 512
KV_LORA = 256
POOL_WINDOWS = (2, 4, 8, 16)
N_EXPERTS = 64
EXPERTS_PER_GROUP = 8
N_GROUPS = N_EXPERTS // EXPERTS_PER_GROUP
TOP_K = 2

LANE = 128
V7X_VMEM_LIMIT = 56 * 1024 * 1024

SEQ_TILE = 256
GDN_CHUNK = 128
MOE_ROWS = 256
ROUTE_TILE = 512

P_QKV = 0
P_Z = 3072
P_CQ = 4096
P_CKV = 4608
P_KR = 4864
P_AB = 4992
P_WIDTH = 5376
P_NTILE = 1792


def _params(sem):
    return pltpu.CompilerParams(dimension_semantics=sem, vmem_limit_bytes=V7X_VMEM_LIMIT)


def _dot(a, b):
    return jnp.dot(a, b, preferred_element_type=F32)


def _dot_nt(a, b):
    return lax.dot_general(a, b, (((1,), (1,)), ((), ())), preferred_element_type=F32)


def _dot_tn(a, b):
    return lax.dot_general(a, b, (((0,), (0,)), ((), ())), preferred_element_type=F32)


def _silu(x):
    return x * jax.nn.sigmoid(x)


def _rms(x, g):
    return x * lax.rsqrt(jnp.mean(x * x, axis=-1, keepdims=True) + EPS) * g


def _split3(x):
    hi = x.astype(BF16)
    r = x - hi.astype(F32)
    mid = r.astype(BF16)
    lo = (r - mid.astype(F32)).astype(BF16)
    return hi, mid, lo


def _mod_kernel(c_ref, w_ref, b_ref, o_ref):
    s = _silu(c_ref[...]).astype(BF16)
    o_ref[...] = _dot(s, w_ref[...].astype(BF16)) + b_ref[...]


def _modulation(cc, w_mod, b_mod):
    depth, d, n6 = w_mod.shape
    tn = 1024
    return pl.pallas_call(
        _mod_kernel,
        grid=(depth, n6 // tn),
        in_specs=[pl.BlockSpec((8, d), lambda l, j: (0, 0)),
                  pl.BlockSpec((None, d, tn), lambda l, j: (l, 0, j)),
                  pl.BlockSpec((None, 1, tn), lambda l, j: (l, 0, j))],
        out_specs=pl.BlockSpec((None, 8, tn), lambda l, j: (l, 0, j)),
        out_shape=jax.ShapeDtypeStruct((depth, 8, n6), F32),
        compiler_params=_params(("parallel", "parallel")),
    )(cc, w_mod, b_mod.reshape(depth, 1, n6))


def _inproj_kernel(nct, xc_ref, x_ref, g_ref, shc_ref, scc_ref, sh_ref, sc_ref, w_ref, o_ref):
    i = pl.program_id(2)

    def run(xr, shr, scr):
        h = _rms(xr[...], g_ref[...]) * (1.0 + scr[...]) + shr[...]
        o_ref[...] = _dot(h.astype(BF16), w_ref[...]).astype(o_ref.dtype)

    @pl.when(i < nct)
    def _():
        run(xc_ref, shc_ref, scc_ref)

    @pl.when(i >= nct)
    def _():
        run(x_ref, sh_ref, sc_ref)


def _in_projection(ctx, x, g, shc, scc, sh, sc, w):
    bsz, seq, d = x.shape
    tm = SEQ_TILE
    nct = ctx.shape[1] // tm
    nt = nct + seq // tm
    width = w.shape[1]
    tn = P_NTILE
    return pl.pallas_call(
        functools.partial(_inproj_kernel, nct),
        grid=(width // tn, bsz, nt),
        in_specs=[pl.BlockSpec((None, tm, d), lambda j, b, i: (b, jnp.minimum(i, nct - 1), 0)),
                  pl.BlockSpec((None, tm, d), lambda j, b, i: (b, jnp.maximum(i - nct, 0), 0)),
                  pl.BlockSpec((1, d), lambda j, b, i: (0, 0)),
                  pl.BlockSpec((1, d), lambda j, b, i: (0, 0)),
                  pl.BlockSpec((1, d), lambda j, b, i: (0, 0)),
                  pl.BlockSpec((None, 1, d), lambda j, b, i: (b, 0, 0)),
                  pl.BlockSpec((None, 1, d), lambda j, b, i: (b, 0, 0)),
                  pl.BlockSpec((d, tn), lambda j, b, i: (0, j))],
        out_specs=pl.BlockSpec((None, tm, tn), lambda j, b, i: (b, i, j)),
        out_shape=jax.ShapeDtypeStruct((bsz, nt * tm, width), BF16),
        compiler_params=_params(("parallel", "parallel", "parallel")),
    )(ctx, x, g, shc, scc, sh, sc, w)


GDN_HALO = 16


def _gdn_front_kernel(nct, pm_ref, pp_ref, pn_ref, cw_ref, ab0_ref, ab1_ref, gp_ref,
                      qkv_ref, gb_ref, ext):
    i = pl.program_id(1)
    n = pl.num_programs(1)
    tm = pm_ref.shape[0]
    hl = GDN_HALO
    zero_prev = jnp.logical_or(i == 0, i == nct)
    zero_next = jnp.logical_or(i == nct - 1, i == n - 1)
    ext[hl:hl + tm, :] = pm_ref[...].astype(F32)
    ext[0:hl, :] = jnp.where(zero_prev, 0.0, pp_ref[...].astype(F32))
    ext[hl + tm:2 * hl + tm, :] = jnp.where(zero_next, 0.0, pn_ref[...].astype(F32))
    taps = cw_ref.shape[0]
    nqk = 2 * GDN_HEADS
    for cb in range(qkv_ref.shape[1] // LANE):
        cs = slice(cb * LANE, (cb + 1) * LANE)
        acc = None
        for j in range(taps):
            off = hl - taps // 2 + j
            term = cw_ref[j:j + 1, cs] * ext[off:off + tm, cs]
            acc = term if acc is None else acc + term
        y = _silu(acc)
        if cb < nqk:
            inv = lax.rsqrt(jnp.sum(y * y, axis=-1, keepdims=True) + EPS)
            if cb < GDN_HEADS:
                inv = inv * (GDN_DK ** -0.5)
            y = y * inv
        qkv_ref[:, cs] = y.astype(BF16)
    for d, ab_ref in enumerate((ab0_ref, ab1_ref)):
        a = ab_ref[...].astype(F32)
        lane = lax.broadcasted_iota(I32, a.shape, 1)
        neg_decay = -jnp.exp(gp_ref[d, 0:1, :])
        xx = a + gp_ref[d, 1:2, :]
        softplus = jnp.maximum(xx, 0.0) + jnp.log1p(jnp.exp(-jnp.abs(xx)))
        gate = jnp.where(lane < GDN_HEADS, neg_decay * softplus, jax.nn.sigmoid(a))
        gb_ref[d] = jnp.where(lane < 2 * GDN_HEADS, gate, 0.0)


def _gdn_front(p_all, conv_w, gate_params, nct):
    bsz, ttot, _ = p_all.shape
    tm = SEQ_TILE
    nt = ttot // tm
    cq = P_Z
    hl = GDN_HALO
    r = tm // hl
    nh = ttot // hl
    return pl.pallas_call(
        functools.partial(_gdn_front_kernel, nct),
        grid=(bsz, nt),
        in_specs=[pl.BlockSpec((None, tm, cq), lambda b, i: (b, i, 0)),
                  pl.BlockSpec((None, hl, cq), lambda b, i: (b, jnp.maximum(i * r - 1, 0), 0)),
                  pl.BlockSpec((None, hl, cq), lambda b, i: (b, jnp.minimum((i + 1) * r, nh - 1), 0)),
                  pl.BlockSpec(conv_w.shape, lambda b, i: (0, 0)),
                  pl.BlockSpec((None, tm, LANE), lambda b, i: (b, i, P_AB // LANE)),
                  pl.BlockSpec((None, tm, LANE), lambda b, i: (b, i, P_AB // LANE + 1)),
                  pl.BlockSpec(gate_params.shape, lambda b, i: (0, 0, 0))],
        out_specs=[pl.BlockSpec((None, tm, cq), lambda b, i: (b, i, 0)),
                   pl.BlockSpec((2, None, tm, LANE), lambda b, i: (0, b, i, 0))],
        out_shape=[jax.ShapeDtypeStruct((bsz, ttot, cq), BF16),
                   jax.ShapeDtypeStruct((2, bsz, ttot, LANE), F32)],
        scratch_shapes=[pltpu.VMEM((tm + 2 * hl, cq), F32)],
        compiler_params=_params(("parallel", "parallel")),
    )(p_all, p_all, p_all, conv_w, p_all, p_all, gate_params)


def _gdn_scan_kernel(nct, q_ref, k_ref, v_ref, gb_ref, o_ref, s_scr):
    d = pl.program_id(1)
    s = pl.program_id(2)
    c = q_ref.shape[0]
    dk = GDN_DK

    @pl.when(s == 0)
    def _():
        s_scr[...] = jnp.zeros_like(s_scr)

    ri = lax.broadcasted_iota(I32, (c, c), 0)
    ci = lax.broadcasted_iota(I32, (c, c), 1)
    incl = jnp.where(d == 0, ri - ci, ci - ri) >= 0
    strict = jnp.logical_and(incl, ci != ri)
    gb = gb_ref[...]
    lane = lax.broadcasted_iota(I32, gb.shape, 1)
    g_hi, g_mid, g_lo = _split3(jnp.where(lane < GDN_HEADS, gb, 0.0))
    m_incl = incl.astype(BF16)
    ones = jnp.ones((c, c), BF16)
    gc = _dot(m_incl, g_hi) + _dot(m_incl, g_mid) + _dot(m_incl, g_lo)
    gl = _dot(ones, g_hi) + _dot(ones, g_mid) + _dot(ones, g_lo)
    gc_t = gc.T
    e_gc = jnp.exp(gc)
    e_rest = jnp.exp(gl - gc)
    e_all = jnp.exp(gl)
    eye = (ri == ci).astype(F32)
    pair_masks = []
    for lb in range(int(math.log2(c))):
        same_pair = jnp.right_shift(ri, lb + 1) == jnp.right_shift(ci, lb + 1)
        same_block = jnp.right_shift(ri, lb) == jnp.right_shift(ci, lb)
        pair_masks.append(jnp.logical_and(same_pair, jnp.logical_not(same_block)))
    heads = range(GDN_HEADS)
    col = lambda a, h: a[:, h:h + 1]
    qb = [q_ref[:, h * dk:(h + 1) * dk] for h in heads]
    kb16 = [k_ref[:, h * dk:(h + 1) * dk] for h in heads]
    k = [a.astype(F32) for a in kb16]
    beta = [col(gb, GDN_HEADS + h) for h in heads]
    kbeta = [k[h] * beta[h] for h in heads]
    kk = [_dot_nt(kbeta[h].astype(BF16), kb16[h]) for h in heads]
    qk = [_dot_nt(qb[h], kb16[h]) for h in heads]
    dec = [jnp.where(incl, jnp.exp(jnp.where(incl, col(gc, h) - gc_t[h:h + 1, :], 0.0)), 0.0) for h in heads]
    low = [jnp.where(strict, kk[h] * dec[h], 0.0) for h in heads]
    r = [-jnp.where(pair_masks[0], low[h], 0.0) for h in heads]
    for mask in pair_masks[1:]:
        dmat = [(r[h] + eye).astype(BF16) for h in heads]
        cd = [_dot(jnp.where(mask, low[h], 0.0).astype(BF16), dmat[h]) for h in heads]
        r = [r[h] - _dot(dmat[h], cd[h].astype(BF16)) for h in heads]
    rhs = [jnp.concatenate([v_ref[:, h * dk:(h + 1) * dk].astype(F32) * beta[h], kbeta[h] * col(e_gc, h)], axis=1)
           for h in heads]
    uw16 = [(rhs[h] + _dot(r[h].astype(BF16), rhs[h].astype(BF16))).astype(BF16) for h in heads]
    kt = [_dot_tn((k[h] * col(e_rest, h)).astype(BF16), uw16[h]) for h in heads]
    qu = [_dot((qk[h] * dec[h]).astype(BF16), uw16[h]) for h in heads]
    st = [s_scr[h] for h in heads]
    st16 = [a.astype(BF16) for a in st]
    qeff = [(qb[h].astype(F32) * col(e_gc, h) - qu[h][:, dk:]).astype(BF16) for h in heads]
    outs = [_dot(qeff[h], st16[h]) + qu[h][:, :dk] for h in heads]
    for h in heads:
        s_scr[h] = st[h] * e_all[0:1, h:h + 1] + kt[h][:, :dk] - _dot(kt[h][:, dk:].astype(BF16), st16[h])

    @pl.when(s >= nct)
    def _():
        for h in heads:
            o_ref[:, h * dk:(h + 1) * dk] = outs[h]


def _gdn_scan(qkv_all, gb_all, nct_tokens):
    bsz, ttot, _ = qkv_all.shape
    c = GDN_CHUNK
    nct = nct_tokens // c
    ns = ttot // c
    nlat = ns - nct
    hv = GDN_HEADS * GDN_DK

    def tmap(d, s):
        rev = jnp.where(s < nct, nct - 1 - s, 2 * nct + nlat - 1 - s)
        return jnp.where(d == 0, s, rev)

    def omap(d, s):
        first = jnp.where(d == 0, 0, nlat - 1)
        return jnp.where(s < nct, first, tmap(d, s) - nct)

    return pl.pallas_call(
        functools.partial(_gdn_scan_kernel, nct),
        grid=(bsz, 2, ns),
        in_specs=[pl.BlockSpec((None, c, hv), lambda b, d, s: (b, tmap(d, s), 0)),
                  pl.BlockSpec((None, c, hv), lambda b, d, s: (b, tmap(d, s), 1)),
                  pl.BlockSpec((None, c, hv), lambda b, d, s: (b, tmap(d, s), 2)),
                  pl.BlockSpec((None, None, c, LANE), lambda b, d, s: (d, b, tmap(d, s), 0))],
        out_specs=pl.BlockSpec((None, None, c, hv), lambda b, d, s: (d, b, omap(d, s), 0)),
        out_shape=jax.ShapeDtypeStruct((2, bsz, nlat * c, hv), F32),
        scratch_shapes=[pltpu.VMEM((GDN_HEADS, GDN_DK, GDN_DK), F32)],
        compiler_params=_params(("parallel", "parallel", "arbitrary")),
    )(qkv_all, qkv_all, qkv_all, gb_all)


def _rope_tile(xr, cos, sin):
    lane = lax.broadcasted_iota(I32, xr.shape, 1)
    first_half = (lane % 32) < 16
    rot = jnp.where(first_half, -pltpu.roll(xr, LANE - 16, 1), pltpu.roll(xr, 16, 1))
    return xr * cos + rot * sin


def _mla_q_kernel(c_ref, ga_ref, w_ref, gn_ref, gr_ref, cos_ref, sin_ref, q_ref):
    cn = _rms(c_ref[...].astype(F32), ga_ref[...]).astype(BF16)
    qf = _dot(cn, w_ref[...])
    tm = qf.shape[0]
    lane = lax.broadcasted_iota(I32, (tm, LANE), 1)
    left = lane < ROPE_DIM
    rope_base = MLA_HEADS * NOPE_DIM
    scale = QK_HEAD ** -0.5
    cos = cos_ref[...]
    sin = sin_ref[...]
    for hp in range(MLA_HEADS // 2):
        rt = qf[:, rope_base + hp * LANE:rope_base + (hp + 1) * LANE]
        rsq = rt * rt
        ss_left = jnp.sum(jnp.where(left, rsq, 0.0), axis=-1, keepdims=True)
        ss_right = jnp.sum(jnp.where(left, 0.0, rsq), axis=-1, keepdims=True)
        invs = []
        for par, ss_r in ((0, ss_left), (1, ss_right)):
            h = 2 * hp + par
            nope = qf[:, h * NOPE_DIM:(h + 1) * NOPE_DIM]
            ss = jnp.sum(nope * nope, axis=-1, keepdims=True) + ss_r
            inv = lax.rsqrt(ss * (1.0 / QK_HEAD) + EPS)
            invs.append(inv)
            q_ref[h, :, 0:NOPE_DIM] = (nope * inv * gn_ref[...] * scale).astype(BF16)
        inv_lane = jnp.where(left, invs[0], invs[1])
        xr = _rope_tile(rt * inv_lane * gr_ref[...], cos, sin) * scale
        q_ref[2 * hp, :, NOPE_DIM:QK_PAD] = jnp.where(left, xr, 0.0).astype(BF16)
        q_ref[2 * hp + 1, :, NOPE_DIM:QK_PAD] = jnp.where(left, pltpu.roll(xr, ROPE_DIM, 1), 0.0).astype(BF16)


def _mla_queries(p_all, ga, w, gn, gr, cos, sin, nct_tiles, seq):
    bsz = p_all.shape[0]
    tm = SEQ_TILE
    return pl.pallas_call(
        _mla_q_kernel,
        grid=(bsz, seq // tm),
        in_specs=[pl.BlockSpec((None, tm, Q_LORA), lambda b, i: (b, i + nct_tiles, P_CQ // Q_LORA)),
                  pl.BlockSpec(ga.shape, lambda b, i: (0, 0)),
                  pl.BlockSpec(w.shape, lambda b, i: (0, 0)),
                  pl.BlockSpec(gn.shape, lambda b, i: (0, 0)),
                  pl.BlockSpec(gr.shape, lambda b, i: (0, 0)),
                  pl.BlockSpec((tm, LANE), lambda b, i: (i + nct_tiles, 0)),
                  pl.BlockSpec((tm, LANE), lambda b, i: (i + nct_tiles, 0))],
        out_specs=pl.BlockSpec((None, MLA_HEADS, tm, QK_PAD), lambda b, i: (b, 0, i, 0)),
        out_shape=jax.ShapeDtypeStruct((bsz, MLA_HEADS, seq, QK_PAD), BF16),
        compiler_params=_params(("parallel", "parallel")),
    )(p_all, ga, w, gn, gr, cos, sin)


def _mla_kv_kernel(c_ref, kr_ref, ga_ref, w_ref, gn_ref, gr_ref, cos_ref, sin_ref, k_ref, v_ref):
    cn = _rms(c_ref[...].astype(F32), ga_ref[...]).astype(BF16)
    kv = _dot(cn, w_ref[...])
    kr = kr_ref[...].astype(F32)
    ss_r = jnp.sum(kr * kr, axis=-1, keepdims=True)
    kr_rot = _rope_tile(kr * gr_ref[...], cos_ref[...], sin_ref[...])
    width = NOPE_DIM + LANE
    for h in range(MLA_HEADS):
        nope = kv[:, h * width:h * width + NOPE_DIM]
        ss = jnp.sum(nope * nope, axis=-1, keepdims=True) + ss_r
        inv = lax.rsqrt(ss * (1.0 / QK_HEAD) + EPS)
        k_ref[h, :, 0:NOPE_DIM] = (nope * inv * gn_ref[...]).astype(BF16)
        k_ref[h, :, NOPE_DIM:QK_PAD] = (kr_rot * inv).astype(BF16)
        v_ref[h] = kv[:, h * width + NOPE_DIM:(h + 1) * width].astype(BF16)


def _mla_keys_values(p_all, ga, w, gn, gr, cos, sin):
    bsz, ttot, _ = p_all.shape
    tm = SEQ_TILE
    return pl.pallas_call(
        _mla_kv_kernel,
        grid=(bsz, ttot // tm),
        in_specs=[pl.BlockSpec((None, tm, KV_LORA), lambda b, i: (b, i, P_CKV // KV_LORA)),
                  pl.BlockSpec((None, tm, LANE), lambda b, i: (b, i, P_KR // LANE)),
                  pl.BlockSpec(ga.shape, lambda b, i: (0, 0)),
                  pl.BlockSpec(w.shape, lambda b, i: (0, 0)),
                  pl.BlockSpec(gn.shape, lambda b, i: (0, 0)),
                  pl.BlockSpec(gr.shape, lambda b, i: (0, 0)),
                  pl.BlockSpec((tm, LANE), lambda b, i: (i, 0)),
                  pl.BlockSpec((tm, LANE), lambda b, i: (i, 0))],
        out_specs=[pl.BlockSpec((None, MLA_HEADS, tm, QK_PAD), lambda b, i: (b, 0, i, 0)),
                   pl.BlockSpec((None, MLA_HEADS, tm, LANE), lambda b, i: (b, 0, i, 0))],
        out_shape=[jax.ShapeDtypeStruct((bsz, MLA_HEADS, ttot, QK_PAD), BF16),
                   jax.ShapeDtypeStruct((bsz, MLA_HEADS, ttot, LANE), BF16)],
        compiler_params=_params(("parallel", "parallel")),
    )(p_all, p_all, ga, w, gn, gr, cos, sin)


def _flash_kernel(tk, q_ref, k_ref, v_ref, o_ref, m_scr, l_scr, acc_scr):
    nk = k_ref.shape[0] // tk
    q = q_ref[...]
    m_scr[...] = jnp.full_like(m_scr, -jnp.inf)
    l_scr[...] = jnp.zeros_like(l_scr)
    acc_scr[...] = jnp.zeros_like(acc_scr)

    def body(j, carry):
        start = pl.multiple_of(j * tk, tk)
        s = _dot_nt(q, k_ref[pl.ds(start, tk), :])
        m_prev = m_scr[...]
        m_new = jnp.maximum(m_prev, jnp.max(s, axis=-1, keepdims=True))
        p = jnp.exp(s - m_new)
        alpha = jnp.exp(m_prev - m_new)
        l_scr[...] = alpha * l_scr[...] + jnp.sum(p, axis=-1, keepdims=True)
        acc_scr[...] = alpha * acc_scr[...] + _dot(p.astype(BF16), v_ref[pl.ds(start, tk), :])
        m_scr[...] = m_new
        return carry

    lax.fori_loop(0, nk, body, 0)
    o_ref[...] = (acc_scr[...] / l_scr[...]).astype(o_ref.dtype)


def _key_tile(ttot):
    for cand in (1280, 1024, 768, 512, 256, 128):
        if ttot % cand == 0:
            return cand
    raise ValueError("key length must be a multiple of 128")


def _attention(q, k, v):
    bsz, heads, seq, _ = q.shape
    ttot = k.shape[2]
    tq = 256
    tk = _key_tile(ttot)
    return pl.pallas_call(
        functools.partial(_flash_kernel, tk),
        grid=(bsz, heads, seq // tq),
        in_specs=[pl.BlockSpec((None, None, tq, QK_PAD), lambda b, h, i: (b, h, i, 0)),
                  pl.BlockSpec((None, None, ttot, QK_PAD), lambda b, h, i: (b, h, 0, 0)),
                  pl.BlockSpec((None, None, ttot, LANE), lambda b, h, i: (b, h, 0, 0))],
        out_specs=pl.BlockSpec((None, tq, LANE), lambda b, h, i: (b, i, h)),
        out_shape=jax.ShapeDtypeStruct((bsz, seq, heads * LANE), BF16),
        scratch_shapes=[pltpu.VMEM((tq, 1), F32), pltpu.VMEM((tq, 1), F32), pltpu.VMEM((tq, LANE), F32)],
        compiler_params=_params(("parallel", "parallel", "arbitrary")),
    )(q, k, v)


def _pack_halves(x):
    w = x.shape[1] // 2
    bits = lax.bitcast_convert_type(x.astype(BF16).astype(F32), U32)
    return jnp.bitwise_or(jnp.right_shift(bits[:, :w], jnp.uint32(16)), bits[:, w:])


def _unpack_halves(wd):
    lo = lax.bitcast_convert_type(jnp.left_shift(wd, jnp.uint32(16)), F32)
    hi = lax.bitcast_convert_type(jnp.bitwise_and(wd, jnp.uint32(0xFFFF0000)), F32)
    return lo, hi


def _ffn_front(xn, gf_ref, sh_ref, sc_ref, wrh_ref, wrl_ref, hp_ref, lg_ref):
    h2 = _rms(xn, gf_ref[...]) * (1.0 + sc_ref[...]) + sh_ref[...]
    hh = h2.astype(BF16)
    hl = (h2 - hh.astype(F32)).astype(BF16)
    hp_ref[...] = _pack_halves(h2)
    lg_ref[...] = _dot(hh, wrh_ref[...]) + _dot(hl, wrh_ref[...]) + _dot(hh, wrl_ref[...])


def _mix_out_kernel(of_ref, ob_ref, z_ref, ym_ref, x_ref, gog_ref, wout_ref, g1_ref,
                    gf_ref, sh_ref, sc_ref, wrh_ref, wrl_ref, xo_ref, hp_ref, lg_ref, mix):
    dv = GDN_DK
    o = of_ref[...] + ob_ref[...]
    for h in range(GDN_HEADS):
        hs = slice(h * dv, (h + 1) * dv)
        y = _rms(o[:, hs], gog_ref[...])
        mix[:, hs] = (y * _silu(z_ref[:, hs].astype(F32))).astype(BF16)
    hv = GDN_HEADS * dv
    mix[:, hv:] = ym_ref[...]
    xn = x_ref[...] + g1_ref[...] * _dot(mix[...], wout_ref[...])
    xo_ref[...] = xn
    _ffn_front(xn, gf_ref, sh_ref, sc_ref, wrh_ref, wrl_ref, hp_ref, lg_ref)


def _mix_out(o2, p_all, ymla, x, gog, wout, g1, gf, sh2, sc2, wrh, wrl, nct_tiles):
    bsz, seq, d = x.shape
    tm = SEQ_TILE
    hv = GDN_HEADS * GDN_DK
    vec = lambda: pl.BlockSpec((None, 1, d), lambda b, i: (b, 0, 0))
    full = lambda a: pl.BlockSpec(a.shape, lambda b, i: (0,) * a.ndim)
    return pl.pallas_call(
        _mix_out_kernel,
        grid=(bsz, seq // tm),
        in_specs=[pl.BlockSpec((None, None, tm, hv), lambda b, i: (0, b, i, 0)),
                  pl.BlockSpec((None, None, tm, hv), lambda b, i: (1, b, i, 0)),
                  pl.BlockSpec((None, tm, hv), lambda b, i: (b, i + nct_tiles, P_Z // hv)),
                  pl.BlockSpec((None, tm, hv), lambda b, i: (b, i, 0)),
                  pl.BlockSpec((None, tm, d), lambda b, i: (b, i, 0)),
                  full(gog), full(wout), vec(), full(gf), vec(), vec(), full(wrh), full(wrl)],
        out_specs=[pl.BlockSpec((None, tm, d), lambda b, i: (b, i, 0)),
                   pl.BlockSpec((None, tm, d // 2), lambda b, i: (b, i, 0)),
                   pl.BlockSpec((None, tm, LANE), lambda b, i: (b, i, 0))],
        out_shape=[jax.ShapeDtypeStruct((bsz, seq, d), F32),
                   jax.ShapeDtypeStruct((bsz, seq, d // 2), U32),
                   jax.ShapeDtypeStruct((bsz, seq, LANE), F32)],
        scratch_shapes=[pltpu.VMEM((tm, wout.shape[0]), BF16)],
        compiler_params=_params(("parallel", "parallel")),
    )(o2, o2, p_all, ymla, x, gog, wout, g1, gf, sh2, sc2, wrh, wrl)


POOL_HALO = 8


def _pool_kernel(seq, xm_ref, xp_ref, xn_ref, gm_ref, sh1_ref, sc1_ref, wp_ref, ps_ref, g1_ref,
                 gf_ref, sh_ref, sc_ref, wrh_ref, wrl_ref, xo_ref, hp_ref, lg_ref, ext):
    i = pl.program_id(1)
    n = pl.num_programs(1)
    tm = xm_ref.shape[0]
    hl = POOL_HALO

    def normed(ref):
        return _rms(ref[...], gm_ref[...]) * (1.0 + sc1_ref[...]) + sh1_ref[...]

    ext[hl:hl + tm, :] = normed(xm_ref)
    ext[0:hl, :] = jnp.where(i == 0, 0.0, normed(xp_ref))
    ext[hl + tm:2 * hl + tm, :] = jnp.where(i == n - 1, 0.0, normed(xn_ref))
    t = i * tm + lax.broadcasted_iota(I32, (tm, 1), 0)
    gw = xm_ref.shape[1] // len(POOL_WINDOWS)
    for gi, win in enumerate(POOL_WINDOWS):
        cs = slice(gi * gw, (gi + 1) * gw)
        half = win // 2
        acc = None
        for off in range(-half, win - half):
            term = ext[hl + off:hl + off + tm, cs]
            acc = term if acc is None else acc + term
        lo = jnp.clip(t - half, 0, seq)
        hi = jnp.clip(t - half + win, 0, seq)
        pooled = acc / (hi - lo).astype(F32) - ext[hl:hl + tm, cs]
        y = _dot(pooled.astype(BF16), wp_ref[gi]) * ps_ref[:, cs]
        xo_ref[:, cs] = xm_ref[:, cs] + g1_ref[:, cs] * y
    _ffn_front(xo_ref[...], gf_ref, sh_ref, sc_ref, wrh_ref, wrl_ref, hp_ref, lg_ref)


def _pool_mixer(x, gm, sh1, sc1, wp, ps, g1, gf, sh2, sc2, wrh, wrl):
    bsz, seq, d = x.shape
    tm = SEQ_TILE
    hl = POOL_HALO
    r = tm // hl
    nh = seq // hl
    vec = lambda: pl.BlockSpec((None, 1, d), lambda b, i: (b, 0, 0))
    full = lambda a: pl.BlockSpec(a.shape, lambda b, i: (0,) * a.ndim)
    return pl.pallas_call(
        functools.partial(_pool_kernel, seq),
        grid=(bsz, seq // tm),
        in_specs=[pl.BlockSpec((None, tm, d), lambda b, i: (b, i, 0)),
                  pl.BlockSpec((None, hl, d), lambda b, i: (b, jnp.maximum(i * r - 1, 0), 0)),
                  pl.BlockSpec((None, hl, d), lambda b, i: (b, jnp.minimum((i + 1) * r, nh - 1), 0)),
                  full(gm), vec(), vec(), full(wp), full(ps), vec(),
                  full(gf), vec(), vec(), full(wrh), full(wrl)],
        out_specs=[pl.BlockSpec((None, tm, d), lambda b, i: (b, i, 0)),
                   pl.BlockSpec((None, tm, d // 2), lambda b, i: (b, i, 0)),
                   pl.BlockSpec((None, tm, LANE), lambda b, i: (b, i, 0))],
        out_shape=[jax.ShapeDtypeStruct((bsz, seq, d), F32),
                   jax.ShapeDtypeStruct((bsz, seq, d // 2), U32),
                   jax.ShapeDtypeStruct((bsz, seq, LANE), F32)],
        scratch_shapes=[pltpu.VMEM((tm + 2 * hl, d), F32)],
        compiler_params=_params(("parallel", "parallel")),
    )(x, x, x, gm, sh1, sc1, wp, ps, g1, gf, sh2, sc2, wrh, wrl)


def _first_max(vals, idx, sentinel):
    m = jnp.max(vals, axis=0, keepdims=True)
    first = jnp.min(jnp.where(vals == m, idx, sentinel), axis=0, keepdims=True)
    return m, first


def _route_kernel(lg_ref, bias_ref, ids_ref, gates_ref, cnt_ref, carry):
    step = pl.program_id(0)

    @pl.when(step == 0)
    def _():
        carry[...] = jnp.zeros_like(carry)

    tm = lg_ref.shape[0]
    epg = EXPERTS_PER_GROUP
    scores = jax.nn.sigmoid(lg_ref[...].T[0:N_EXPERTS, :])
    biased = scores + bias_ref[:, 0:1]
    eidx = lax.broadcasted_iota(I32, (epg, tm), 0)
    best_score = best_grp = best_i1 = best_i2 = None
    for g in range(N_GROUPS):
        blk = biased[g * epg:(g + 1) * epg, :]
        m1, i1 = _first_max(blk, eidx, epg)
        m2, i2 = _first_max(jnp.where(eidx == i1, -jnp.inf, blk), eidx, epg)
        gs = m1 + m2
        if g == 0:
            best_score, best_grp, best_i1, best_i2 = gs, jnp.zeros_like(i1), i1, i2
        else:
            better = gs > best_score
            best_score = jnp.where(better, gs, best_score)
            best_grp = jnp.where(better, g, best_grp)
            best_i1 = jnp.where(better, i1, best_i1)
            best_i2 = jnp.where(better, i2, best_i2)
    e1 = best_grp * epg + best_i1
    e2 = best_grp * epg + best_i2
    eall = lax.broadcasted_iota(I32, (N_EXPERTS, tm), 0)
    oh1 = eall == e1
    oh2 = eall == e2
    s1 = jnp.sum(jnp.where(oh1, scores, 0.0), axis=0, keepdims=True)
    s2 = jnp.sum(jnp.where(oh2, scores, 0.0), axis=0, keepdims=True)
    denom = s1 + s2
    oh = jnp.logical_or(oh1, oh2).astype(BF16)
    earlier = (lax.broadcasted_iota(I32, (tm, tm), 0) < lax.broadcasted_iota(I32, (tm, tm), 1)).astype(BF16)
    before = _dot(oh, earlier) + carry[:, 0:1]
    r1 = jnp.sum(jnp.where(oh1, before, 0.0), axis=0, keepdims=True)
    r2 = jnp.sum(jnp.where(oh2, before, 0.0), axis=0, keepdims=True)
    carry[...] = carry[...] + jnp.sum(oh.astype(F32), axis=1, keepdims=True)
    cnt_ref[...] = carry[...]
    zi = jnp.zeros((4, tm), I32)
    ids_ref[...] = jnp.concatenate([e1, e2, r1.astype(I32), r2.astype(I32), zi], axis=0)
    gates_ref[...] = jnp.concatenate([s1 / denom, s2 / denom, jnp.zeros((6, tm), F32)], axis=0)


def _route(logits, bias_col):
    n = logits.shape[0]
    tm = ROUTE_TILE
    return pl.pallas_call(
        _route_kernel,
        grid=(n // tm,),
        in_specs=[pl.BlockSpec((tm, LANE), lambda t: (t, 0)),
                  pl.BlockSpec(bias_col.shape, lambda t: (0, 0))],
        out_specs=[pl.BlockSpec((8, tm), lambda t: (0, t)),
                   pl.BlockSpec((8, tm), lambda t: (0, t)),
                   pl.BlockSpec((N_EXPERTS, LANE), lambda t: (0, 0))],
        out_shape=[jax.ShapeDtypeStruct((8, n), I32),
                   jax.ShapeDtypeStruct((8, n), F32),
                   jax.ShapeDtypeStruct((N_EXPERTS, LANE), F32)],
        scratch_shapes=[pltpu.VMEM((N_EXPERTS, LANE), F32)],
        compiler_params=_params(("arbitrary",)),
    )(logits, bias_col)


def _token_row_copy(hp_hbm, xbuf, sem, slot, tok, j):
    return pltpu.make_async_copy(hp_hbm.at[pl.ds(tok, 1), :], xbuf.at[slot, pl.ds(j, 1), :], sem.at[slot])


def _experts_kernel(rows, be_ref, nu_ref, idx_ref, idx_next_ref, gate_ref, hp_hbm, wgu_ref, wd_ref,
                    y_ref, xbuf, wgu16, wd16, sem):
    i = pl.program_id(0)
    slot = i % 2
    used = nu_ref[0]

    def start_gather(ref, s):
        def issue(j, carry):
            _token_row_copy(hp_hbm, xbuf, sem, s, ref[0, j], j).start()
            return carry
        lax.fori_loop(0, rows, issue, 0)

    @pl.when(jnp.logical_and(i == 0, used > 0))
    def _():
        start_gather(idx_ref, 0)

    @pl.when(i + 1 < used)
    def _():
        start_gather(idx_next_ref, 1 - slot)

    @pl.when(i < used)
    def _():
        def drain(j, carry):
            _token_row_copy(hp_hbm, xbuf, sem, slot, 0, j).wait()
            return carry
        lax.fori_loop(0, rows, drain, 0)

        @pl.when(jnp.logical_or(i == 0, be_ref[i] != be_ref[jnp.maximum(i - 1, 0)]))
        def _():
            wgu16[...] = wgu_ref[...].astype(BF16)
            wd16[...] = wd_ref[...].astype(BF16)

        lo, hi = _unpack_halves(xbuf[slot])
        half = lo.shape[1]
        gu = _dot(lo.astype(BF16), wgu16[0:half, :]) + _dot(hi.astype(BF16), wgu16[half:, :])
        f = gu.shape[1] // 2
        hid = _silu(gu[:, :f]) * gu[:, f:]
        y = _dot(hid.astype(BF16), wd16[...]) * gate_ref[...]
        y_ref[...] = _pack_halves(y)

    @pl.when(i >= used)
    def _():
        y_ref[...] = jnp.zeros_like(y_ref)


def _experts(layer, block_expert, n_used, row_tok, row_gate, hp, w_gate_up, w_down):
    n_rows = row_tok.shape[0]
    half = hp.shape[1]
    rows = MOE_ROWS
    nb = n_rows // rows
    d = 2 * half
    f2 = w_gate_up.shape[3]
    grid_spec = pltpu.PrefetchScalarGridSpec(
        num_scalar_prefetch=2,
        grid=(nb,),
        in_specs=[pl.BlockSpec((None, 1, rows), lambda i, be, nu: (i, 0, 0), memory_space=pltpu.SMEM),
                  pl.BlockSpec((None, 1, rows), lambda i, be, nu: (jnp.minimum(i + 1, nb - 1), 0, 0),
                               memory_space=pltpu.SMEM),
                  pl.BlockSpec((rows, 1), lambda i, be, nu: (i, 0)),
                  pl.BlockSpec(memory_space=pl.ANY),
                  pl.BlockSpec((None, None, d, f2), lambda i, be, nu: (layer, be[i], 0, 0)),
                  pl.BlockSpec((None, None, f2 // 2, d), lambda i, be, nu: (layer, be[i], 0, 0))],
        out_specs=pl.BlockSpec((rows, half), lambda i, be, nu: (i, 0)),
        scratch_shapes=[pltpu.VMEM((2, rows, half), U32), pltpu.VMEM((d, f2), BF16),
                        pltpu.VMEM((f2 // 2, d), BF16), pltpu.SemaphoreType.DMA((2,))],
    )
    idx3 = row_tok.reshape(nb, 1, rows)
    return pl.pallas_call(
        functools.partial(_experts_kernel, rows),
        grid_spec=grid_spec,
        out_shape=jax.ShapeDtypeStruct((n_rows, half), U32),
        compiler_params=_params(("arbitrary",)),
    )(block_expert, n_used, idx3, idx3, row_gate, hp, w_gate_up, w_down)


def _combine_kernel(tm, d0_ref, d1_ref, y_hbm, x_ref, g2_ref, xo_ref, buf, sem):
    def issue(j, carry):
        pltpu.make_async_copy(y_hbm.at[pl.ds(d0_ref[0, j], 1), :], buf.at[0, pl.ds(j, 1), :], sem).start()
        pltpu.make_async_copy(y_hbm.at[pl.ds(d1_ref[0, j], 1), :], buf.at[1, pl.ds(j, 1), :], sem).start()
        return carry

    lax.fori_loop(0, tm, issue, 0)

    def drain(j, carry):
        pltpu.make_async_copy(y_hbm.at[pl.ds(0, 1), :], buf.at[0, pl.ds(j, 1), :], sem).wait()
        pltpu.make_async_copy(y_hbm.at[pl.ds(0, 1), :], buf.at[1, pl.ds(j, 1), :], sem).wait()
        return carry

    lax.fori_loop(0, tm, drain, 0)
    lo0, hi0 = _unpack_halves(buf[0])
    lo1, hi1 = _unpack_halves(buf[1])
    half = lo0.shape[1]
    xo_ref[:, 0:half] = x_ref[:, 0:half] + g2_ref[:, 0:half] * (lo0 + lo1)
    xo_ref[:, half:] = x_ref[:, half:] + g2_ref[:, half:] * (hi0 + hi1)


def _combine(y, dest, x, g2):
    bsz, seq, d = x.shape
    tm = SEQ_TILE
    nt = seq // tm
    half = d // 2
    idx = lambda k: pl.BlockSpec((None, None, 1, tm), lambda b, i: (k, b * nt + i, 0, 0),
                                 memory_space=pltpu.SMEM)
    dest4 = dest.reshape(2, bsz * nt, 1, tm)
    return pl.pallas_call(
        functools.partial(_combine_kernel, tm),
        grid=(bsz, nt),
        in_specs=[idx(0), idx(1),
                  pl.BlockSpec(memory_space=pl.ANY),
                  pl.BlockSpec((None, tm, d), lambda b, i: (b, i, 0)),
                  pl.BlockSpec((None, 1, d), lambda b, i: (b, 0, 0))],
        out_specs=pl.BlockSpec((None, tm, d), lambda b, i: (b, i, 0)),
        out_shape=jax.ShapeDtypeStruct((bsz, seq, d), F32),
        scratch_shapes=[pltpu.VMEM((2, tm, half), U32), pltpu.SemaphoreType.DMA(())],
        compiler_params=_params(("arbitrary", "arbitrary")),
    )(dest4, dest4, y, x, g2)


def _moe(layer, x, hp, logits, g2, bias_col, w_gate_up, w_down):
    bsz, seq, d = x.shape
    n = bsz * seq
    rows = MOE_ROWS
    ids, gates, cnt = _route(logits.reshape(n, LANE), bias_col)
    counts = cnt[:, 0].astype(I32)
    padded = (counts + rows - 1) // rows * rows
    pad_end = jnp.cumsum(padded)
    pad_start = pad_end - padded
    expert = ids[0:TOP_K]
    dest = pad_start[expert] + ids[TOP_K:2 * TOP_K]
    n_rows = -(-(n * TOP_K + N_EXPERTS * (rows - 1)) // rows) * rows
    tok = jnp.broadcast_to(jnp.arange(n, dtype=I32), (TOP_K, n))
    row_tok = jnp.zeros((n_rows,), I32).at[dest.reshape(-1)].set(tok.reshape(-1))
    row_gate = jnp.zeros((n_rows,), F32).at[dest.reshape(-1)].set(gates[0:TOP_K].reshape(-1))
    block_expert = jnp.minimum(
        jnp.searchsorted(pad_end, jnp.arange(n_rows // rows, dtype=I32) * rows, side='right'),
        N_EXPERTS - 1).astype(I32)
    n_used = (pad_end[-1:] // rows).astype(I32)
    y = _experts(layer, block_expert, n_used, row_tok, row_gate.reshape(n_rows, 1), hp.reshape(n, d // 2),
                 w_gate_up, w_down)
    return _combine(y, dest, x, g2)


def _permute_w_in(w_in):
    d = w_in.shape[0]
    hq = GDN_HEADS * GDN_DK
    off_a = 4 * hq
    off_cq = off_a + 4 * GDN_HEADS
    off_ckv = off_cq + Q_LORA
    off_kr = off_ckv + KV_LORA
    z = lambda n: jnp.zeros((d, n), w_in.dtype)
    a = lambda k: w_in[:, off_a + k * GDN_HEADS:off_a + (k + 1) * GDN_HEADS]
    pad_ab = LANE - 2 * GDN_HEADS
    cols = [w_in[:, :off_a], w_in[:, off_cq:off_kr + ROPE_DIM], z(LANE - ROPE_DIM),
            a(0), a(2), z(pad_ab), a(1), a(3), z(pad_ab)]
    w = jnp.concatenate(cols, axis=1)
    return jnp.concatenate([w, z(P_WIDTH - w.shape[1])], axis=1).astype(BF16)


def _rope_tables(rows, n_ctx):
    row = jnp.repeat(jnp.arange(rows, dtype=F32), GRID_W)
    col = jnp.tile(jnp.arange(GRID_W, dtype=F32), rows)
    pairs = ROPE_DIM // 4
    inv_freq = ROPE_THETA ** (-jnp.arange(pairs, dtype=F32) / pairs)
    ang_r = row[:, None] * inv_freq
    ang_c = col[:, None] * inv_freq
    ang = jnp.concatenate([ang_r, ang_r, ang_c, ang_c], axis=-1)
    ang = jnp.concatenate([jnp.zeros((n_ctx, ROPE_DIM), F32), ang], axis=0)
    cos, sin = jnp.cos(ang), jnp.sin(ang)
    return jnp.concatenate([cos, cos], axis=1), jnp.concatenate([sin, sin], axis=1)


def kernel(x, c, ctx, c_ctx, w_mod, b_mod, norm_mix_g, norm_ffn_g, w_in, conv_qkv, a_log_fwd, a_log_bwd, dt_bias_fwd, dt_bias_bwd, gdn_out_g, q_a_norm_g, w_uq, kv_a_norm_g, w_ukv, q_norm_g, k_norm_g, w_out, w_pool, pool_scale, w_router, router_bias, w_gate_up, w_down):
    bsz, seq, d = x.shape
    n_ctx = ctx.shape[1]
    depth = w_mod.shape[0]
    assert depth == 2, "the context stream is only read: no layer after the first even layer reads it"
    nct_tiles = n_ctx // SEQ_TILE

    cc = jnp.concatenate([c, c_ctx[None, :], jnp.zeros((8 - bsz - 1, d), F32)], axis=0)
    mod = _modulation(cc, w_mod, b_mod).reshape(depth, 8, 6, d)

    def mods(layer):
        m = mod[layer]
        return [m[:bsz, k][:, None, :] for k in range(6)], [m[bsz:bsz + 1, k] for k in range(6)]

    wr = jnp.pad(w_router, ((0, 0), (0, LANE - N_EXPERTS)))
    wrh = wr.astype(BF16)
    wrl = (wr - wrh.astype(F32)).astype(BF16)
    bias_col = jnp.broadcast_to(router_bias.astype(F32)[:, None], (N_EXPERTS, LANE))
    row = lambda v: v.astype(F32)[None, :]

    for layer in range(depth):
        j = layer // 2
        (sh1, sc1, g1, sh2, sc2, g2), (csh1, csc1, _, _, _, _) = mods(layer)
        gm = row(norm_mix_g[layer])
        gf = row(norm_ffn_g[layer])
        if layer % 2 == 0:
            p_all = _in_projection(ctx, x, gm, csh1, csc1, sh1, sc1, _permute_w_in(w_in[j]))
            conv_w = conv_qkv[j].astype(F32)
            gate_params = jnp.zeros((2, 8, LANE), F32)
            gate_params = gate_params.at[0, 0, :GDN_HEADS].set(a_log_fwd[j]).at[0, 1, :GDN_HEADS].set(dt_bias_fwd[j])
            gate_params = gate_params.at[1, 0, :GDN_HEADS].set(a_log_bwd[j]).at[1, 1, :GDN_HEADS].set(dt_bias_bwd[j])
            qkv_all, gb_all = _gdn_front(p_all, conv_w, gate_params, nct_tiles)
            o2 = _gdn_scan(qkv_all, gb_all, n_ctx)

            cos, sin = _rope_tables(seq // GRID_W, n_ctx)
            gq = q_norm_g[j].astype(F32)
            gk = k_norm_g[j].astype(F32)
            wq = w_uq[j].reshape(Q_LORA, MLA_HEADS, QK_HEAD)
            wq = jnp.concatenate([wq[:, :, :NOPE_DIM].reshape(Q_LORA, -1),
                                  wq[:, :, NOPE_DIM:].reshape(Q_LORA, -1)], axis=1).astype(BF16)
            q = _mla_queries(p_all, row(q_a_norm_g[j]), wq, gq[None, :NOPE_DIM],
                             jnp.tile(gq[NOPE_DIM:], 2)[None, :], cos, sin, nct_tiles, seq)
            gk_rope = jnp.concatenate([gk[NOPE_DIM:], jnp.zeros((LANE - ROPE_DIM,), F32)])[None, :]
            k_all, v_all = _mla_keys_values(p_all, row(kv_a_norm_g[j]), w_ukv[j].astype(BF16),
                                            gk[None, :NOPE_DIM], gk_rope, cos, sin)
            ymla = _attention(q, k_all, v_all)
            x, hp, logits = _mix_out(o2, p_all, ymla, x, row(gdn_out_g[j]), w_out[j].astype(BF16), g1,
                                     gf, sh2, sc2, wrh, wrl, nct_tiles)
        else:
            x, hp, logits = _pool_mixer(x, gm, sh1, sc1, w_pool[j].astype(BF16), row(pool_scale[j]), g1,
                                        gf, sh2, sc2, wrh, wrl)
        x = _moe(layer, x, hp, logits, g2, bias_col, w_gate_up, w_down)
    return x
```

```python
import functools
import math

import jax
import jax.numpy as jnp
from jax import lax
from jax.experimental import pallas as pl
from jax.experimental.pallas import tpu as pltpu

F32 = jnp.float32
BF16 = jnp.bfloat16
U32 = jnp.uint32
I32 = jnp.int32

EPS = 1e-6
GRID_W = 64
GDN_HEADS = 8
GDN_DK = 128
MLA_HEADS = 8
NOPE_DIM = 128
ROPE_DIM = 64
QK_HEAD = NOPE_DIM + ROPE_DIM
QK_PAD = 256
ROPE_THETA = 10000.0
Q_LORA = 512
KV_LORA = 256
POOL_WINDOWS = (2, 4, 8, 16)
N_EXPERTS = 64
EXPERTS_PER_GROUP = 8
N_GROUPS = N_EXPERTS // EXPERTS_PER_GROUP
TOP_K = 2

LANE = 128
V7X_VMEM_LIMIT = 56 * 1024 * 1024

SEQ_TILE = 256
GDN_CHUNK = 128
MOE_ROWS = 256
ROUTE_TILE = 512

P_QKV = 0
P_Z = 3072
P_CQ = 4096
P_CKV = 4608
P_KR = 4864
P_AB = 4992
P_WIDTH = 5376
P_NTILE = 1792


def _params(sem):
    return pltpu.CompilerParams(dimension_semantics=sem, vmem_limit_bytes=V7X_VMEM_LIMIT)


def _dot(a, b):
    return jnp.dot(a, b, preferred_element_type=F32)


def _dot_nt(a, b):
    return lax.dot_general(a, b, (((1,), (1,)), ((), ())), preferred_element_type=F32)


def _dot_tn(a, b):
    return lax.dot_general(a, b, (((0,), (0,)), ((), ())), preferred_element_type=F32)


def _silu(x):
    return x * jax.nn.sigmoid(x)


def _rms(x, g):
    return x * lax.rsqrt(jnp.mean(x * x, axis=-1, keepdims=True) + EPS) * g


def _split3(x):
    hi = x.astype(BF16)
    r = x - hi.astype(F32)
    mid = r.astype(BF16)
    lo = (r - mid.astype(F32)).astype(BF16)
    return hi, mid, lo


def _mod_kernel(c_ref, w_ref, b_ref, o_ref):
    s = _silu(c_ref[...]).astype(BF16)
    o_ref[...] = _dot(s, w_ref[...].astype(BF16)) + b_ref[...]


def _modulation(cc, w_mod, b_mod):
    depth, d, n6 = w_mod.shape
    tn = 1024
    return pl.pallas_call(
        _mod_kernel,
        grid=(depth, n6 // tn),
        in_specs=[pl.BlockSpec((8, d), lambda l, j: (0, 0)),
                  pl.BlockSpec((None, d, tn), lambda l, j: (l, 0, j)),
                  pl.BlockSpec((None, 1, tn), lambda l, j: (l, 0, j))],
        out_specs=pl.BlockSpec((None, 8, tn), lambda l, j: (l, 0, j)),
        out_shape=jax.ShapeDtypeStruct((depth, 8, n6), F32),
        compiler_params=_params(("parallel", "parallel")),
    )(cc, w_mod, b_mod.reshape(depth, 1, n6))


def _inproj_kernel(nct, xc_ref, x_ref, g_ref, shc_ref, scc_ref, sh_ref, sc_ref, w_ref, o_ref):
    i = pl.program_id(2)

    def run(xr, shr, scr):
        h = _rms(xr[...], g_ref[...]) * (1.0 + scr[...]) + shr[...]
        o_ref[...] = _dot(h.astype(BF16), w_ref[...]).astype(o_ref.dtype)

    @pl.when(i < nct)
    def _():
        run(xc_ref, shc_ref, scc_ref)

    @pl.when(i >= nct)
    def _():
        run(x_ref, sh_ref, sc_ref)


def _in_projection(ctx, x, g, shc, scc, sh, sc, w):
    bsz, seq, d = x.shape
    tm = SEQ_TILE
    nct = ctx.shape[1] // tm
    nt = nct + seq // tm
    width = w.shape[1]
    tn = P_NTILE
    return pl.pallas_call(
        functools.partial(_inproj_kernel, nct),
        grid=(width // tn, bsz, nt),
        in_specs=[pl.BlockSpec((None, tm, d), lambda j, b, i: (b, jnp.minimum(i, nct - 1), 0)),
                  pl.BlockSpec((None, tm, d), lambda j, b, i: (b, jnp.maximum(i - nct, 0), 0)),
                  pl.BlockSpec((1, d), lambda j, b, i: (0, 0)),
                  pl.BlockSpec((1, d), lambda j, b, i: (0, 0)),
                  pl.BlockSpec((1, d), lambda j, b, i: (0, 0)),
                  pl.BlockSpec((None, 1, d), lambda j, b, i: (b, 0, 0)),
                  pl.BlockSpec((None, 1, d), lambda j, b, i: (b, 0, 0)),
                  pl.BlockSpec((d, tn), lambda j, b, i: (0, j))],
        out_specs=pl.BlockSpec((None, tm, tn), lambda j, b, i: (b, i, j)),
        out_shape=jax.ShapeDtypeStruct((bsz, nt * tm, width), BF16),
        compiler_params=_params(("parallel", "parallel", "parallel")),
    )(ctx, x, g, shc, scc, sh, sc, w)


GDN_HALO = 16


def _gdn_front_kernel(nct, pm_ref, pp_ref, pn_ref, cw_ref, ab0_ref, ab1_ref, gp_ref,
                      qkv_ref, gb_ref, ext):
    i = pl.program_id(1)
    n = pl.num_programs(1)
    tm = pm_ref.shape[0]
    hl = GDN_HALO
    zero_prev = jnp.logical_or(i == 0, i == nct)
    zero_next = jnp.logical_or(i == nct - 1, i == n - 1)
    ext[hl:hl + tm, :] = pm_ref[...].astype(F32)
    ext[0:hl, :] = jnp.where(zero_prev, 0.0, pp_ref[...].astype(F32))
    ext[hl + tm:2 * hl + tm, :] = jnp.where(zero_next, 0.0, pn_ref[...].astype(F32))
    taps = cw_ref.shape[0]
    nqk = 2 * GDN_HEADS
    for cb in range(qkv_ref.shape[1] // LANE):
        cs = slice(cb * LANE, (cb + 1) * LANE)
        acc = None
        for j in range(taps):
            off = hl - taps // 2 + j
            term = cw_ref[j:j + 1, cs] * ext[off:off + tm, cs]
            acc = term if acc is None else acc + term
        y = _silu(acc)
        if cb < nqk:
            inv = lax.rsqrt(jnp.sum(y * y, axis=-1, keepdims=True) + EPS)
            if cb < GDN_HEADS:
                inv = inv * (GDN_DK ** -0.5)
            y = y * inv
        qkv_ref[:, cs] = y.astype(BF16)
    for d, ab_ref in enumerate((ab0_ref, ab1_ref)):
        a = ab_ref[...].astype(F32)
        lane = lax.broadcasted_iota(I32, a.shape, 1)
        neg_decay = -jnp.exp(gp_ref[d, 0:1, :])
        xx = a + gp_ref[d, 1:2, :]
        softplus = jnp.maximum(xx, 0.0) + jnp.log1p(jnp.exp(-jnp.abs(xx)))
        gate = jnp.where(lane < GDN_HEADS, neg_decay * softplus, jax.nn.sigmoid(a))
        gb_ref[d] = jnp.where(lane < 2 * GDN_HEADS, gate, 0.0)


def _gdn_front(p_all, conv_w, gate_params, nct):
    bsz, ttot, _ = p_all.shape
    tm = SEQ_TILE
    nt = ttot // tm
    cq = P_Z
    hl = GDN_HALO
    r = tm // hl
    nh = ttot // hl
    return pl.pallas_call(
        functools.partial(_gdn_front_kernel, nct),
        grid=(bsz, nt),
        in_specs=[pl.BlockSpec((None, tm, cq), lambda b, i: (b, i, 0)),
                  pl.BlockSpec((None, hl, cq), lambda b, i: (b, jnp.maximum(i * r - 1, 0), 0)),
                  pl.BlockSpec((None, hl, cq), lambda b, i: (b, jnp.minimum((i + 1) * r, nh - 1), 0)),
                  pl.BlockSpec(conv_w.shape, lambda b, i: (0, 0)),
                  pl.BlockSpec((None, tm, LANE), lambda b, i: (b, i, P_AB // LANE)),
                  pl.BlockSpec((None, tm, LANE), lambda b, i: (b, i, P_AB // LANE + 1)),
                  pl.BlockSpec(gate_params.shape, lambda b, i: (0, 0, 0))],
        out_specs=[pl.BlockSpec((None, tm, cq), lambda b, i: (b, i, 0)),
                   pl.BlockSpec((2, None, tm, LANE), lambda b, i: (0, b, i, 0))],
        out_shape=[jax.ShapeDtypeStruct((bsz, ttot, cq), BF16),
                   jax.ShapeDtypeStruct((2, bsz, ttot, LANE), F32)],
        scratch_shapes=[pltpu.VMEM((tm + 2 * hl, cq), F32)],
        compiler_params=_params(("parallel", "parallel")),
    )(p_all, p_all, p_all, conv_w, p_all, p_all, gate_params)


def _gdn_scan_kernel(nct, q_ref, k_ref, v_ref, gb_ref, o_ref, s_scr):
    d = pl.program_id(1)
    s = pl.program_id(2)
    c = q_ref.shape[0]
    dk = GDN_DK

    @pl.when(s == 0)
    def _():
        s_scr[...] = jnp.zeros_like(s_scr)

    ri = lax.broadcasted_iota(I32, (c, c), 0)
    ci = lax.broadcasted_iota(I32, (c, c), 1)
    incl = jnp.where(d == 0, ri - ci, ci - ri) >= 0
    strict = jnp.logical_and(incl, ci != ri)
    gb = gb_ref[...]
    lane = lax.broadcasted_iota(I32, gb.shape, 1)
    g_hi, g_mid, g_lo = _split3(jnp.where(lane < GDN_HEADS, gb, 0.0))
    m_incl = incl.astype(BF16)
    ones = jnp.ones((c, c), BF16)
    gc = _dot(m_incl, g_hi) + _dot(m_incl, g_mid) + _dot(m_incl, g_lo)
    gl = _dot(ones, g_hi) + _dot(ones, g_mid) + _dot(ones, g_lo)
    gc_t = gc.T
    e_gc = jnp.exp(gc)
    e_rest = jnp.exp(gl - gc)
    e_all = jnp.exp(gl)
    eye = (ri == ci).astype(F32)
    pair_masks = []
    for lb in range(int(math.log2(c))):
        same_pair = jnp.right_shift(ri, lb + 1) == jnp.right_shift(ci, lb + 1)
        same_block = jnp.right_shift(ri, lb) == jnp.right_shift(ci, lb)
        pair_masks.append(jnp.logical_and(same_pair, jnp.logical_not(same_block)))
    heads = range(GDN_HEADS)
    col = lambda a, h: a[:, h:h + 1]
    qb = [q_ref[:, h * dk:(h + 1) * dk] for h in heads]
    kb16 = [k_ref[:, h * dk:(h + 1) * dk] for h in heads]
    k = [a.astype(F32) for a in kb16]
    beta = [col(gb, GDN_HEADS + h) for h in heads]
    kbeta = [k[h] * beta[h] for h in heads]
    kk = [_dot_nt(kbeta[h].astype(BF16), kb16[h]) for h in heads]
    qk = [_dot_nt(qb[h], kb16[h]) for h in heads]
    dec = [jnp.where(incl, jnp.exp(jnp.where(incl, col(gc, h) - gc_t[h:h + 1, :], 0.0)), 0.0) for h in heads]
    low = [jnp.where(strict, kk[h] * dec[h], 0.0) for h in heads]
    r = [-jnp.where(pair_masks[0], low[h], 0.0) for h in heads]
    for mask in pair_masks[1:]:
        dmat = [(r[h] + eye).astype(BF16) for h in heads]
        cd = [_dot(jnp.where(mask, low[h], 0.0).astype(BF16), dmat[h]) for h in heads]
        r = [r[h] - _dot(dmat[h], cd[h].astype(BF16)) for h in heads]
    rhs = [jnp.concatenate([v_ref[:, h * dk:(h + 1) * dk].astype(F32) * beta[h], kbeta[h] * col(e_gc, h)], axis=1)
           for h in heads]
    uw16 = [(rhs[h] + _dot(r[h].astype(BF16), rhs[h].astype(BF16))).astype(BF16) for h in heads]
    kt = [_dot_tn((k[h] * col(e_rest, h)).astype(BF16), uw16[h]) for h in heads]
    qu = [_dot((qk[h] * dec[h]).astype(BF16), uw16[h]) for h in heads]
    st = [s_scr[h] for h in heads]
    st16 = [a.astype(BF16) for a in st]
    qeff = [(qb[h].astype(F32) * col(e_gc, h) - qu[h][:, dk:]).astype(BF16) for h in heads]
    outs = [_dot(qeff[h], st16[h]) + qu[h][:, :dk] for h in heads]
    for h in heads:
        s_scr[h] = st[h] * e_all[0:1, h:h + 1] + kt[h][:, :dk] - _dot(kt[h][:, dk:].astype(BF16), st16[h])

    @pl.when(s >= nct)
    def _():
        for h in heads:
            o_ref[:, h * dk:(h + 1) * dk] = outs[h]


def _gdn_scan(qkv_all, gb_all, nct_tokens):
    bsz, ttot, _ = qkv_all.shape
    c = GDN_CHUNK
    nct = nct_tokens // c
    ns = ttot // c
    nlat = ns - nct
    hv = GDN_HEADS * GDN_DK

    def tmap(d, s):
        rev = jnp.where(s < nct, nct - 1 - s, 2 * nct + nlat - 1 - s)
        return jnp.where(d == 0, s, rev)

    def omap(d, s):
        first = jnp.where(d == 0, 0, nlat - 1)
        return jnp.where(s < nct, first, tmap(d, s) - nct)

    return pl.pallas_call(
        functools.partial(_gdn_scan_kernel, nct),
        grid=(bsz, 2, ns),
        in_specs=[pl.BlockSpec((None, c, hv), lambda b, d, s: (b, tmap(d, s), 0)),
                  pl.BlockSpec((None, c, hv), lambda b, d, s: (b, tmap(d, s), 1)),
                  pl.BlockSpec((None, c, hv), lambda b, d, s: (b, tmap(d, s), 2)),
                  pl.BlockSpec((None, None, c, LANE), lambda b, d, s: (d, b, tmap(d, s), 0))],
        out_specs=pl.BlockSpec((None, None, c, hv), lambda b, d, s: (d, b, omap(d, s), 0)),
        out_shape=jax.ShapeDtypeStruct((2, bsz, nlat * c, hv), F32),
        scratch_shapes=[pltpu.VMEM((GDN_HEADS, GDN_DK, GDN_DK), F32)],
        compiler_params=_params(("parallel", "parallel", "arbitrary")),
    )(qkv_all, qkv_all, qkv_all, gb_all)


def _rope_tile(xr, cos, sin):
    lane = lax.broadcasted_iota(I32, xr.shape, 1)
    first_half = (lane % 32) < 16
    rot = jnp.where(first_half, -pltpu.roll(xr, LANE - 16, 1), pltpu.roll(xr, 16, 1))
    return xr * cos + rot * sin


def _mla_q_kernel(c_ref, ga_ref, w_ref, gn_ref, gr_ref, cos_ref, sin_ref, q_ref):
    cn = _rms(c_ref[...].astype(F32), ga_ref[...]).astype(BF16)
    qf = _dot(cn, w_ref[...])
    tm = qf.shape[0]
    lane = lax.broadcasted_iota(I32, (tm, LANE), 1)
    left = lane < ROPE_DIM
    rope_base = MLA_HEADS * NOPE_DIM
    scale = QK_HEAD ** -0.5
    cos = cos_ref[...]
    sin = sin_ref[...]
    for hp in range(MLA_HEADS // 2):
        rt = qf[:, rope_base + hp * LANE:rope_base + (hp + 1) * LANE]
        rsq = rt * rt
        ss_left = jnp.sum(jnp.where(left, rsq, 0.0), axis=-1, keepdims=True)
        ss_right = jnp.sum(jnp.where(left, 0.0, rsq), axis=-1, keepdims=True)
        invs = []
        for par, ss_r in ((0, ss_left), (1, ss_right)):
            h = 2 * hp + par
            nope = qf[:, h * NOPE_DIM:(h + 1) * NOPE_DIM]
            ss = jnp.sum(nope * nope, axis=-1, keepdims=True) + ss_r
            inv = lax.rsqrt(ss * (1.0 / QK_HEAD) + EPS)
            invs.append(inv)
            q_ref[h, :, 0:NOPE_DIM] = (nope * inv * gn_ref[...] * scale).astype(BF16)
        inv_lane = jnp.where(left, invs[0], invs[1])
        xr = _rope_tile(rt * inv_lane * gr_ref[...], cos, sin) * scale
        q_ref[2 * hp, :, NOPE_DIM:QK_PAD] = jnp.where(left, xr, 0.0).astype(BF16)
        q_ref[2 * hp + 1, :, NOPE_DIM:QK_PAD] = jnp.where(left, pltpu.roll(xr, ROPE_DIM, 1), 0.0).astype(BF16)


def _mla_queries(p_all, ga, w, gn, gr, cos, sin, nct_tiles, seq):
    bsz = p_all.shape[0]
    tm = SEQ_TILE
    return pl.pallas_call(
        _mla_q_kernel,
        grid=(bsz, seq // tm),
        in_specs=[pl.BlockSpec((None, tm, Q_LORA), lambda b, i: (b, i + nct_tiles, P_CQ // Q_LORA)),
                  pl.BlockSpec(ga.shape, lambda b, i: (0, 0)),
                  pl.BlockSpec(w.shape, lambda b, i: (0, 0)),
                  pl.BlockSpec(gn.shape, lambda b, i: (0, 0)),
                  pl.BlockSpec(gr.shape, lambda b, i: (0, 0)),
                  pl.BlockSpec((tm, LANE), lambda b, i: (i + nct_tiles, 0)),
                  pl.BlockSpec((tm, LANE), lambda b, i: (i + nct_tiles, 0))],
        out_specs=pl.BlockSpec((None, MLA_HEADS, tm, QK_PAD), lambda b, i: (b, 0, i, 0)),
        out_shape=jax.ShapeDtypeStruct((bsz, MLA_HEADS, seq, QK_PAD), BF16),
        compiler_params=_params(("parallel", "parallel")),
    )(p_all, ga, w, gn, gr, cos, sin)


def _mla_kv_kernel(c_ref, kr_ref, ga_ref, w_ref, gn_ref, gr_ref, cos_ref, sin_ref, k_ref, v_ref):
    cn = _rms(c_ref[...].astype(F32), ga_ref[...]).astype(BF16)
    kv = _dot(cn, w_ref[...])
    kr = kr_ref[...].astype(F32)
    ss_r = jnp.sum(kr * kr, axis=-1, keepdims=True)
    kr_rot = _rope_tile(kr * gr_ref[...], cos_ref[...], sin_ref[...])
    width = NOPE_DIM + LANE
    for h in range(MLA_HEADS):
        nope = kv[:, h * width:h * width + NOPE_DIM]
        ss = jnp.sum(nope * nope, axis=-1, keepdims=True) + ss_r
        inv = lax.rsqrt(ss * (1.0 / QK_HEAD) + EPS)
        k_ref[h, :, 0:NOPE_DIM] = (nope * inv * gn_ref[...]).astype(BF16)
        k_ref[h, :, NOPE_DIM:QK_PAD] = (kr_rot * inv).astype(BF16)
        v_ref[h] = kv[:, h * width + NOPE_DIM:(h + 1) * width].astype(BF16)


def _mla_keys_values(p_all, ga, w, gn, gr, cos, sin):
    bsz, ttot, _ = p_all.shape
    tm = SEQ_TILE
    return pl.pallas_call(
        _mla_kv_kernel,
        grid=(bsz, ttot // tm),
        in_specs=[pl.BlockSpec((None, tm, KV_LORA), lambda b, i: (b, i, P_CKV // KV_LORA)),
                  pl.BlockSpec((None, tm, LANE), lambda b, i: (b, i, P_KR // LANE)),
                  pl.BlockSpec(ga.shape, lambda b, i: (0, 0)),
                  pl.BlockSpec(w.shape, lambda b, i: (0, 0)),
                  pl.BlockSpec(gn.shape, lambda b, i: (0, 0)),
                  pl.BlockSpec(gr.shape, lambda b, i: (0, 0)),
                  pl.BlockSpec((tm, LANE), lambda b, i: (i, 0)),
                  pl.BlockSpec((tm, LANE), lambda b, i: (i, 0))],
        out_specs=[pl.BlockSpec((None, MLA_HEADS, tm, QK_PAD), lambda b, i: (b, 0, i, 0)),
                   pl.BlockSpec((None, MLA_HEADS, tm, LANE), lambda b, i: (b, 0, i, 0))],
        out_shape=[jax.ShapeDtypeStruct((bsz, MLA_HEADS, ttot, QK_PAD), BF16),
                   jax.ShapeDtypeStruct((bsz, MLA_HEADS, ttot, LANE), BF16)],
        compiler_params=_params(("parallel", "parallel")),
    )(p_all, p_all, ga, w, gn, gr, cos, sin)


def _flash_kernel(tk, q_ref, k_ref, v_ref, o_ref, s_a, s_b, m_scr, l_scr, acc_scr):
    nk = k_ref.shape[0] // tk
    q = q_ref[...]
    m_scr[...] = jnp.full_like(m_scr, -jnp.inf)
    l_scr[...] = jnp.zeros_like(l_scr)
    acc_scr[...] = jnp.zeros_like(acc_scr)

    def keys(ref, j):
        return ref[pl.ds(pl.multiple_of(j * tk, tk), tk), :]

    def scores(j):
        return _dot_nt(q, keys(k_ref, j))

    def update(s_ref, j):
        s = s_ref[...]
        m_prev = m_scr[...]
        m_new = jnp.maximum(m_prev, jnp.max(s, axis=-1, keepdims=True))
        p = jnp.exp(s - m_new)
        alpha = jnp.exp(m_prev - m_new)
        l_scr[...] = alpha * l_scr[...] + jnp.sum(p, axis=-1, keepdims=True)
        acc_scr[...] = alpha * acc_scr[...] + _dot(p.astype(BF16), keys(v_ref, j))
        m_scr[...] = m_new

    s_a[...] = scores(0)

    def body(i, carry):
        s_b[...] = scores(2 * i + 1)
        update(s_a, 2 * i)
        s_a[...] = scores(2 * i + 2)
        update(s_b, 2 * i + 1)
        return carry

    lax.fori_loop(0, (nk - 1) // 2, body, 0)
    if nk % 2 == 1:
        update(s_a, nk - 1)
    else:
        s_b[...] = scores(nk - 1)
        update(s_a, nk - 2)
        update(s_b, nk - 1)
    o_ref[...] = (acc_scr[...] / l_scr[...]).astype(o_ref.dtype)


def _key_tile(ttot):
    for cand in (1280, 1024, 768, 512, 256, 128):
        if ttot % cand == 0:
            return cand
    raise ValueError("key length must be a multiple of 128")


def _attention(q, k, v):
    bsz, heads, seq, _ = q.shape
    ttot = k.shape[2]
    tq = 512
    tk = _key_tile(ttot)
    return pl.pallas_call(
        functools.partial(_flash_kernel, tk),
        grid=(bsz, heads, seq // tq),
        in_specs=[pl.BlockSpec((None, None, tq, QK_PAD), lambda b, h, i: (b, h, i, 0)),
                  pl.BlockSpec((None, None, ttot, QK_PAD), lambda b, h, i: (b, h, 0, 0)),
                  pl.BlockSpec((None, None, ttot, LANE), lambda b, h, i: (b, h, 0, 0))],
        out_specs=pl.BlockSpec((None, tq, LANE), lambda b, h, i: (b, i, h)),
        out_shape=jax.ShapeDtypeStruct((bsz, seq, heads * LANE), BF16),
        scratch_shapes=[pltpu.VMEM((tq, tk), F32), pltpu.VMEM((tq, tk), F32),
                        pltpu.VMEM((tq, 1), F32), pltpu.VMEM((tq, 1), F32), pltpu.VMEM((tq, LANE), F32)],
        compiler_params=_params(("parallel", "parallel", "arbitrary")),
    )(q, k, v)


def _pack_halves(x):
    w = x.shape[1] // 2
    bits = lax.bitcast_convert_type(x.astype(BF16).astype(F32), U32)
    return jnp.bitwise_or(jnp.right_shift(bits[:, :w], jnp.uint32(16)), bits[:, w:])


def _unpack_halves(wd):
    lo = lax.bitcast_convert_type(jnp.left_shift(wd, jnp.uint32(16)), F32)
    hi = lax.bitcast_convert_type(jnp.bitwise_and(wd, jnp.uint32(0xFFFF0000)), F32)
    return lo, hi


def _ffn_front(xn, gf_ref, sh_ref, sc_ref, wrh_ref, wrl_ref, hp_ref, lg_ref):
    h2 = _rms(xn, gf_ref[...]) * (1.0 + sc_ref[...]) + sh_ref[...]
    hh = h2.astype(BF16)
    hl = (h2 - hh.astype(F32)).astype(BF16)
    hp_ref[...] = _pack_halves(h2)
    lg_ref[...] = _dot(hh, wrh_ref[...]) + _dot(hl, wrh_ref[...]) + _dot(hh, wrl_ref[...])


def _mix_out_kernel(of_ref, ob_ref, z_ref, ym_ref, x_ref, gog_ref, wout_ref, g1_ref,
                    gf_ref, sh_ref, sc_ref, wrh_ref, wrl_ref, xo_ref, hp_ref, lg_ref, mix):
    dv = GDN_DK
    o = of_ref[...] + ob_ref[...]
    for h in range(GDN_HEADS):
        hs = slice(h * dv, (h + 1) * dv)
        y = _rms(o[:, hs], gog_ref[...])
        mix[:, hs] = (y * _silu(z_ref[:, hs].astype(F32))).astype(BF16)
    hv = GDN_HEADS * dv
    mix[:, hv:] = ym_ref[...]
    xn = x_ref[...] + g1_ref[...] * _dot(mix[...], wout_ref[...])
    xo_ref[...] = xn
    _ffn_front(xn, gf_ref, sh_ref, sc_ref, wrh_ref, wrl_ref, hp_ref, lg_ref)


def _mix_out(o2, p_all, ymla, x, gog, wout, g1, gf, sh2, sc2, wrh, wrl, nct_tiles):
    bsz, seq, d = x.shape
    tm = SEQ_TILE
    hv = GDN_HEADS * GDN_DK
    vec = lambda: pl.BlockSpec((None, 1, d), lambda b, i: (b, 0, 0))
    full = lambda a: pl.BlockSpec(a.shape, lambda b, i: (0,) * a.ndim)
    return pl.pallas_call(
        _mix_out_kernel,
        grid=(bsz, seq // tm),
        in_specs=[pl.BlockSpec((None, None, tm, hv), lambda b, i: (0, b, i, 0)),
                  pl.BlockSpec((None, None, tm, hv), lambda b, i: (1, b, i, 0)),
                  pl.BlockSpec((None, tm, hv), lambda b, i: (b, i + nct_tiles, P_Z // hv)),
                  pl.BlockSpec((None, tm, hv), lambda b, i: (b, i, 0)),
                  pl.BlockSpec((None, tm, d), lambda b, i: (b, i, 0)),
                  full(gog), full(wout), vec(), full(gf), vec(), vec(), full(wrh), full(wrl)],
        out_specs=[pl.BlockSpec((None, tm, d), lambda b, i: (b, i, 0)),
                   pl.BlockSpec((None, tm, d // 2), lambda b, i: (b, i, 0)),
                   pl.BlockSpec((None, tm, LANE), lambda b, i: (b, i, 0))],
        out_shape=[jax.ShapeDtypeStruct((bsz, seq, d), F32),
                   jax.ShapeDtypeStruct((bsz, seq, d // 2), U32),
                   jax.ShapeDtypeStruct((bsz, seq, LANE), F32)],
        scratch_shapes=[pltpu.VMEM((tm, wout.shape[0]), BF16)],
        compiler_params=_params(("parallel", "parallel")),
    )(o2, o2, p_all, ymla, x, gog, wout, g1, gf, sh2, sc2, wrh, wrl)


POOL_HALO = 8


def _pool_kernel(seq, xm_ref, xp_ref, xn_ref, gm_ref, sh1_ref, sc1_ref, wp_ref, ps_ref, g1_ref,
                 gf_ref, sh_ref, sc_ref, wrh_ref, wrl_ref, xo_ref, hp_ref, lg_ref, ext):
    i = pl.program_id(1)
    n = pl.num_programs(1)
    tm = xm_ref.shape[0]
    hl = POOL_HALO

    def normed(ref):
        return _rms(ref[...], gm_ref[...]) * (1.0 + sc1_ref[...]) + sh1_ref[...]

    ext[hl:hl + tm, :] = normed(xm_ref)
    ext[0:hl, :] = jnp.where(i == 0, 0.0, normed(xp_ref))
    ext[hl + tm:2 * hl + tm, :] = jnp.where(i == n - 1, 0.0, normed(xn_ref))
    t = i * tm + lax.broadcasted_iota(I32, (tm, 1), 0)
    gw = xm_ref.shape[1] // len(POOL_WINDOWS)
    for gi, win in enumerate(POOL_WINDOWS):
        cs = slice(gi * gw, (gi + 1) * gw)
        half = win // 2
        acc = None
        for off in range(-half, win - half):
            term = ext[hl + off:hl + off + tm, cs]
            acc = term if acc is None else acc + term
        lo = jnp.clip(t - half, 0, seq)
        hi = jnp.clip(t - half + win, 0, seq)
        pooled = acc / (hi - lo).astype(F32) - ext[hl:hl + tm, cs]
        y = _dot(pooled.astype(BF16), wp_ref[gi]) * ps_ref[:, cs]
        xo_ref[:, cs] = xm_ref[:, cs] + g1_ref[:, cs] * y
    _ffn_front(xo_ref[...], gf_ref, sh_ref, sc_ref, wrh_ref, wrl_ref, hp_ref, lg_ref)


def _pool_mixer(x, gm, sh1, sc1, wp, ps, g1, gf, sh2, sc2, wrh, wrl):
    bsz, seq, d = x.shape
    tm = SEQ_TILE
    hl = POOL_HALO
    r = tm // hl
    nh = seq // hl
    vec = lambda: pl.BlockSpec((None, 1, d), lambda b, i: (b, 0, 0))
    full = lambda a: pl.BlockSpec(a.shape, lambda b, i: (0,) * a.ndim)
    return pl.pallas_call(
        functools.partial(_pool_kernel, seq),
        grid=(bsz, seq // tm),
        in_specs=[pl.BlockSpec((None, tm, d), lambda b, i: (b, i, 0)),
                  pl.BlockSpec((None, hl, d), lambda b, i: (b, jnp.maximum(i * r - 1, 0), 0)),
                  pl.BlockSpec((None, hl, d), lambda b, i: (b, jnp.minimum((i + 1) * r, nh - 1), 0)),
                  full(gm), vec(), vec(), full(wp), full(ps), vec(),
                  full(gf), vec(), vec(), full(wrh), full(wrl)],
        out_specs=[pl.BlockSpec((None, tm, d), lambda b, i: (b, i, 0)),
                   pl.BlockSpec((None, tm, d // 2), lambda b, i: (b, i, 0)),
                   pl.BlockSpec((None, tm, LANE), lambda b, i: (b, i, 0))],
        out_shape=[jax.ShapeDtypeStruct((bsz, seq, d), F32),
                   jax.ShapeDtypeStruct((bsz, seq, d // 2), U32),
                   jax.ShapeDtypeStruct((bsz, seq, LANE), F32)],
        scratch_shapes=[pltpu.VMEM((tm + 2 * hl, d), F32)],
        compiler_params=_params(("parallel", "parallel")),
    )(x, x, x, gm, sh1, sc1, wp, ps, g1, gf, sh2, sc2, wrh, wrl)


def _first_max(vals, idx, sentinel):
    m = jnp.max(vals, axis=0, keepdims=True)
    first = jnp.min(jnp.where(vals == m, idx, sentinel), axis=0, keepdims=True)
    return m, first


def _route_kernel(rows, lg_ref, bias_ref, dest_ref, gates_ref, cnt_ref, carry):
    phase = pl.program_id(0)
    step = pl.program_id(1)

    @pl.when(jnp.logical_and(phase == 0, step == 0))
    def _():
        carry[...] = jnp.zeros_like(carry)

    @pl.when(jnp.logical_and(phase == 1, step == 0))
    def _():
        shift = int(math.log2(rows))
        counts_row = carry[...].T.astype(I32)
        padded = jnp.left_shift(jnp.right_shift(counts_row + (rows - 1), shift), shift).astype(F32)
        i_exp = lax.broadcasted_iota(I32, padded.shape, 0)
        j_exp = lax.broadcasted_iota(I32, padded.shape, 1)
        first_row = jnp.sum(jnp.where(j_exp < i_exp, padded, 0.0), axis=1, keepdims=True)
        carry[...] = jnp.broadcast_to(first_row, carry.shape)

    tm = lg_ref.shape[0]
    epg = EXPERTS_PER_GROUP
    scores = jax.nn.sigmoid(lg_ref[...].T[0:N_EXPERTS, :])
    biased = scores + bias_ref[:, 0:1]
    eidx = lax.broadcasted_iota(I32, (epg, tm), 0)
    best_score = best_grp = best_i1 = best_i2 = None
    for g in range(N_GROUPS):
        blk = biased[g * epg:(g + 1) * epg, :]
        m1, i1 = _first_max(blk, eidx, epg)
        m2, i2 = _first_max(jnp.where(eidx == i1, -jnp.inf, blk), eidx, epg)
        gs = m1 + m2
        if g == 0:
            best_score, best_grp, best_i1, best_i2 = gs, jnp.zeros_like(i1), i1, i2
        else:
            better = gs > best_score
            best_score = jnp.where(better, gs, best_score)
            best_grp = jnp.where(better, g, best_grp)
            best_i1 = jnp.where(better, i1, best_i1)
            best_i2 = jnp.where(better, i2, best_i2)
    e1 = best_grp * epg + best_i1
    e2 = best_grp * epg + best_i2
    eall = lax.broadcasted_iota(I32, (LANE, tm), 0)
    oh1 = eall == e1
    oh2 = eall == e2
    oh = jnp.logical_or(oh1, oh2).astype(BF16)
    picked = jnp.sum(oh.astype(F32), axis=1, keepdims=True)

    @pl.when(phase == 0)
    def _():
        carry[...] = carry[...] + picked
        cnt_ref[...] = carry[...]

    @pl.when(phase == 1)
    def _():
        s1 = jnp.sum(jnp.where(oh1[0:N_EXPERTS], scores, 0.0), axis=0, keepdims=True)
        s2 = jnp.sum(jnp.where(oh2[0:N_EXPERTS], scores, 0.0), axis=0, keepdims=True)
        denom = s1 + s2
        earlier = (lax.broadcasted_iota(I32, (tm, tm), 0) < lax.broadcasted_iota(I32, (tm, tm), 1)).astype(BF16)
        row = _dot(oh, earlier) + carry[:, 0:1]
        d1 = jnp.sum(jnp.where(oh1, row, 0.0), axis=0, keepdims=True)
        d2 = jnp.sum(jnp.where(oh2, row, 0.0), axis=0, keepdims=True)
        carry[...] = carry[...] + picked
        dest_ref[...] = jnp.concatenate([d1.astype(I32), d2.astype(I32), jnp.zeros((6, tm), I32)], axis=0)
        gates = jnp.concatenate([s1 / denom, s2 / denom, jnp.zeros((LANE - 2, tm), F32)], axis=0)
        gates_ref[...] = gates.T


def _route(logits, bias_col):
    n = logits.shape[0]
    tm = ROUTE_TILE
    rows = MOE_ROWS
    assert rows & (rows - 1) == 0
    return pl.pallas_call(
        functools.partial(_route_kernel, rows),
        grid=(2, n // tm),
        in_specs=[pl.BlockSpec((tm, LANE), lambda p, t: (t, 0)),
                  pl.BlockSpec(bias_col.shape, lambda p, t: (0, 0))],
        out_specs=[pl.BlockSpec((8, tm), lambda p, t: (0, p * t)),
                   pl.BlockSpec((tm, LANE), lambda p, t: (p * t, 0)),
                   pl.BlockSpec((LANE, LANE), lambda p, t: (0, 0))],
        out_shape=[jax.ShapeDtypeStruct((8, n), I32),
                   jax.ShapeDtypeStruct((n, LANE), F32),
                   jax.ShapeDtypeStruct((LANE, LANE), F32)],
        scratch_shapes=[pltpu.VMEM((LANE, LANE), F32)],
        compiler_params=_params(("arbitrary", "arbitrary")),
    )(logits, bias_col)


def _dispatch_copy(hp_ref, xs_hbm, sem, j, dst):
    return pltpu.make_async_copy(hp_ref.at[pl.ds(j, 1), :], xs_hbm.at[pl.ds(dst, 1), :], sem)


def _dispatch_kernel(tm, d0_ref, d1_ref, hp_ref, xs_in, xs_hbm, sem):
    del xs_in

    def issue(j, carry):
        _dispatch_copy(hp_ref, xs_hbm, sem, j, d0_ref[0, j]).start()
        _dispatch_copy(hp_ref, xs_hbm, sem, j, d1_ref[0, j]).start()
        return carry

    lax.fori_loop(0, tm, issue, 0)

    def drain(j, carry):
        _dispatch_copy(hp_ref, xs_hbm, sem, j, 0).wait()
        _dispatch_copy(hp_ref, xs_hbm, sem, j, 0).wait()
        return carry

    lax.fori_loop(0, tm, drain, 0)


def _dispatch(hp, dest, n_rows):
    n, half = hp.shape
    tm = ROUTE_TILE
    nt = n // tm
    idx = lambda k: pl.BlockSpec((None, None, 1, tm), lambda i: (k, i, 0, 0), memory_space=pltpu.SMEM)
    dest4 = dest.reshape(dest.shape[0], nt, 1, tm)
    return pl.pallas_call(
        functools.partial(_dispatch_kernel, tm),
        grid=(nt,),
        in_specs=[idx(0), idx(1),
                  pl.BlockSpec((tm, half), lambda i: (i, 0)),
                  pl.BlockSpec(memory_space=pl.ANY)],
        out_specs=pl.BlockSpec(memory_space=pl.ANY),
        out_shape=jax.ShapeDtypeStruct((n_rows, half), U32),
        input_output_aliases={3: 0},
        scratch_shapes=[pltpu.SemaphoreType.DMA(())],
        compiler_params=_params(("arbitrary",)),
    )(dest4, dest4, hp, jnp.zeros((n_rows, half), U32))


def _experts_kernel(be_ref, nu_ref, xs_ref, wgu_ref, wd_ref, y_ref, wgu16, wd16):
    i = pl.program_id(0)

    @pl.when(i < nu_ref[0])
    def _():
        @pl.when(jnp.logical_or(i == 0, be_ref[i] != be_ref[jnp.maximum(i - 1, 0)]))
        def _():
            wgu16[...] = wgu_ref[...].astype(BF16)
            wd16[...] = wd_ref[...].astype(BF16)

        lo, hi = _unpack_halves(xs_ref[...])
        half = lo.shape[1]
        gu = _dot(lo.astype(BF16), wgu16[0:half, :]) + _dot(hi.astype(BF16), wgu16[half:, :])
        f = gu.shape[1] // 2
        hid = _silu(gu[:, :f]) * gu[:, f:]
        y_ref[...] = _pack_halves(_dot(hid.astype(BF16), wd16[...]))

    @pl.when(i >= nu_ref[0])
    def _():
        y_ref[...] = jnp.zeros_like(y_ref)


def _experts(layer, block_expert, n_used, xs, w_gate_up, w_down):
    n_rows, half = xs.shape
    rows = MOE_ROWS
    d = 2 * half
    f2 = w_gate_up.shape[3]
    grid_spec = pltpu.PrefetchScalarGridSpec(
        num_scalar_prefetch=2,
        grid=(n_rows // rows,),
        in_specs=[pl.BlockSpec((rows, half), lambda i, be, nu: (i, 0)),
                  pl.BlockSpec((None, None, d, f2), lambda i, be, nu: (layer, be[i], 0, 0)),
                  pl.BlockSpec((None, None, f2 // 2, d), lambda i, be, nu: (layer, be[i], 0, 0))],
        out_specs=pl.BlockSpec((rows, half), lambda i, be, nu: (i, 0)),
        scratch_shapes=[pltpu.VMEM((d, f2), BF16), pltpu.VMEM((f2 // 2, d), BF16)],
    )
    return pl.pallas_call(
        _experts_kernel,
        grid_spec=grid_spec,
        out_shape=jax.ShapeDtypeStruct((n_rows, half), U32),
        compiler_params=_params(("arbitrary",)),
    )(block_expert, n_used, xs, w_gate_up, w_down)


def _combine_copy(y_hbm, buf, sem, k, src, j):
    return pltpu.make_async_copy(y_hbm.at[pl.ds(src, 1), :], buf.at[k, pl.ds(j, 1), :], sem)


def _combine_kernel(tm, d0_ref, d1_ref, y_hbm, gates_ref, x_ref, g2_ref, xo_ref, buf, sem):
    def issue(j, carry):
        _combine_copy(y_hbm, buf, sem, 0, d0_ref[0, j], j).start()
        _combine_copy(y_hbm, buf, sem, 1, d1_ref[0, j], j).start()
        return carry

    lax.fori_loop(0, tm, issue, 0)

    def drain(j, carry):
        _combine_copy(y_hbm, buf, sem, 0, 0, j).wait()
        _combine_copy(y_hbm, buf, sem, 1, 0, j).wait()
        return carry

    lax.fori_loop(0, tm, drain, 0)
    lo0, hi0 = _unpack_halves(buf[0])
    lo1, hi1 = _unpack_halves(buf[1])
    half = lo0.shape[1]
    ga = gates_ref[:, 0:1]
    gb = gates_ref[:, 1:2]
    xo_ref[:, 0:half] = x_ref[:, 0:half] + g2_ref[:, 0:half] * (ga * lo0 + gb * lo1)
    xo_ref[:, half:] = x_ref[:, half:] + g2_ref[:, half:] * (ga * hi0 + gb * hi1)


def _combine(y, dest, gates, x, g2):
    bsz, seq, d = x.shape
    tm = SEQ_TILE
    nt = seq // tm
    half = d // 2
    idx = lambda k: pl.BlockSpec((None, None, 1, tm), lambda b, i: (k, b * nt + i, 0, 0),
                                 memory_space=pltpu.SMEM)
    dest4 = dest.reshape(dest.shape[0], bsz * nt, 1, tm)
    return pl.pallas_call(
        functools.partial(_combine_kernel, tm),
        grid=(bsz, nt),
        in_specs=[idx(0), idx(1),
                  pl.BlockSpec(memory_space=pl.ANY),
                  pl.BlockSpec((tm, LANE), lambda b, i: (b * nt + i, 0)),
                  pl.BlockSpec((None, tm, d), lambda b, i: (b, i, 0)),
                  pl.BlockSpec((None, 1, d), lambda b, i: (b, 0, 0))],
        out_specs=pl.BlockSpec((None, tm, d), lambda b, i: (b, i, 0)),
        out_shape=jax.ShapeDtypeStruct((bsz, seq, d), F32),
        scratch_shapes=[pltpu.VMEM((2, tm, half), U32), pltpu.SemaphoreType.DMA(())],
        compiler_params=_params(("arbitrary", "arbitrary")),
    )(dest4, dest4, y, gates, x, g2)


def _moe(layer, x, hp, logits, g2, bias_col, w_gate_up, w_down):
    bsz, seq, d = x.shape
    n = bsz * seq
    rows = MOE_ROWS
    dest, gates, cnt = _route(logits.reshape(n, LANE), bias_col)
    counts = cnt[:N_EXPERTS, 0].astype(I32)
    pad_end = jnp.cumsum((counts + rows - 1) // rows * rows)
    n_rows = -(-(n * TOP_K + N_EXPERTS * (rows - 1)) // rows) * rows
    block_start = jnp.arange(n_rows // rows, dtype=I32) * rows
    block_expert = jnp.minimum(jnp.sum((pad_end[None, :] <= block_start[:, None]).astype(I32), axis=1),
                               N_EXPERTS - 1)
    n_used = pad_end[-1:] // rows
    xs = _dispatch(hp.reshape(n, d // 2), dest, n_rows)
    y = _experts(layer, block_expert, n_used, xs, w_gate_up, w_down)
    return _combine(y, dest, gates, x, g2)


def _permute_w_in(w_in):
    d = w_in.shape[0]
    hq = GDN_HEADS * GDN_DK
    off_a = 4 * hq
    off_cq = off_a + 4 * GDN_HEADS
    off_ckv = off_cq + Q_LORA
    off_kr = off_ckv + KV_LORA
    z = lambda n: jnp.zeros((d, n), w_in.dtype)
    a = lambda k: w_in[:, off_a + k * GDN_HEADS:off_a + (k + 1) * GDN_HEADS]
    pad_ab = LANE - 2 * GDN_HEADS
    cols = [w_in[:, :off_a], w_in[:, off_cq:off_kr + ROPE_DIM], z(LANE - ROPE_DIM),
            a(0), a(2), z(pad_ab), a(1), a(3), z(pad_ab)]
    w = jnp.concatenate(cols, axis=1)
    return jnp.concatenate([w, z(P_WIDTH - w.shape[1])], axis=1).astype(BF16)


def _rope_tables(rows, n_ctx):
    row = jnp.repeat(jnp.arange(rows, dtype=F32), GRID_W)
    col = jnp.tile(jnp.arange(GRID_W, dtype=F32), rows)
    pairs = ROPE_DIM // 4
    inv_freq = ROPE_THETA ** (-jnp.arange(pairs, dtype=F32) / pairs)
    ang_r = row[:, None] * inv_freq
    ang_c = col[:, None] * inv_freq
    ang = jnp.concatenate([ang_r, ang_r, ang_c, ang_c], axis=-1)
    ang = jnp.concatenate([jnp.zeros((n_ctx, ROPE_DIM), F32), ang], axis=0)
    cos, sin = jnp.cos(ang), jnp.sin(ang)
    return jnp.concatenate([cos, cos], axis=1), jnp.concatenate([sin, sin], axis=1)


def kernel(x, c, ctx, c_ctx, w_mod, b_mod, norm_mix_g, norm_ffn_g, w_in, conv_qkv, a_log_fwd, a_log_bwd, dt_bias_fwd, dt_bias_bwd, gdn_out_g, q_a_norm_g, w_uq, kv_a_norm_g, w_ukv, q_norm_g, k_norm_g, w_out, w_pool, pool_scale, w_router, router_bias, w_gate_up, w_down):
    bsz, seq, d = x.shape
    n_ctx = ctx.shape[1]
    depth = w_mod.shape[0]
    assert depth == 2, "the context stream is only read: no layer after the first even layer reads it"
    nct_tiles = n_ctx // SEQ_TILE

    cc = jnp.concatenate([c, c_ctx[None, :], jnp.zeros((8 - bsz - 1, d), F32)], axis=0)
    mod = _modulation(cc, w_mod, b_mod).reshape(depth, 8, 6, d)

    def mods(layer):
        m = mod[layer]
        return [m[:bsz, k][:, None, :] for k in range(6)], [m[bsz:bsz + 1, k] for k in range(6)]

    wr = jnp.pad(w_router, ((0, 0), (0, LANE - N_EXPERTS)))
    wrh = wr.astype(BF16)
    wrl = (wr - wrh.astype(F32)).astype(BF16)
    bias_col = jnp.broadcast_to(router_bias.astype(F32)[:, None], (N_EXPERTS, LANE))
    row = lambda v: v.astype(F32)[None, :]

    for layer in range(depth):
        j = layer // 2
        (sh1, sc1, g1, sh2, sc2, g2), (csh1, csc1, _, _, _, _) = mods(layer)
        gm = row(norm_mix_g[layer])
        gf = row(norm_ffn_g[layer])
        if layer % 2 == 0:
            p_all = _in_projection(ctx, x, gm, csh1, csc1, sh1, sc1, _permute_w_in(w_in[j]))
            conv_w = conv_qkv[j].astype(F32)
            gate_params = jnp.zeros((2, 8, LANE), F32)
            gate_params = gate_params.at[0, 0, :GDN_HEADS].set(a_log_fwd[j]).at[0, 1, :GDN_HEADS].set(dt_bias_fwd[j])
            gate_params = gate_params.at[1, 0, :GDN_HEADS].set(a_log_bwd[j]).at[1, 1, :GDN_HEADS].set(dt_bias_bwd[j])
            qkv_all, gb_all = _gdn_front(p_all, conv_w, gate_params, nct_tiles)
            o2 = _gdn_scan(qkv_all, gb_all, n_ctx)

            cos, sin = _rope_tables(seq // GRID_W, n_ctx)
            gq = q_norm_g[j].astype(F32)
            gk = k_norm_g[j].astype(F32)
            wq = w_uq[j].reshape(Q_LORA, MLA_HEADS, QK_HEAD)
            wq = jnp.concatenate([wq[:, :, :NOPE_DIM].reshape(Q_LORA, -1),
                                  wq[:, :, NOPE_DIM:].reshape(Q_LORA, -1)], axis=1).astype(BF16)
            q = _mla_queries(p_all, row(q_a_norm_g[j]), wq, gq[None, :NOPE_DIM],
                             jnp.tile(gq[NOPE_DIM:], 2)[None, :], cos, sin, nct_tiles, seq)
            gk_rope = jnp.concatenate([gk[NOPE_DIM:], jnp.zeros((LANE - ROPE_DIM,), F32)])[None, :]
            k_all, v_all = _mla_keys_values(p_all, row(kv_a_norm_g[j]), w_ukv[j].astype(BF16),
                                            gk[None, :NOPE_DIM], gk_rope, cos, sin)
            ymla = _attention(q, k_all, v_all)
            x, hp, logits = _mix_out(o2, p_all, ymla, x, row(gdn_out_g[j]), w_out[j].astype(BF16), g1,
                                     gf, sh2, sc2, wrh, wrl, nct_tiles)
        else:
            x, hp, logits = _pool_mixer(x, gm, sh1, sc1, w_pool[j].astype(BF16), row(pool_scale[j]), g1,
                                        gf, sh2, sc2, wrh, wrl)
        x = _moe(layer, x, hp, logits, g2, bias_col, w_gate_up, w_down)
    return x
```

```python
import functools
import math

import jax
import jax.numpy as jnp
from jax import lax
from jax.experimental import pallas as pl
from jax.experimental.pallas import tpu as pltpu

F32 = jnp.float32
BF16 = jnp.bfloat16
U32 = jnp.uint32
I32 = jnp.int32

EPS = 1e-6
GRID_W = 64
GDN_HEADS = 8
GDN_DK = 128
MLA_HEADS = 8
NOPE_DIM = 128
ROPE_DIM = 64
QK_HEAD = NOPE_DIM + ROPE_DIM
QK_PAD = 256
ROPE_THETA = 10000.0
Q_LORA = 512
KV_LORA = 256
POOL_WINDOWS = (2, 4, 8, 16)
N_EXPERTS = 64
EXPERTS_PER_GROUP = 8
N_GROUPS = N_EXPERTS // EXPERTS_PER_GROUP
TOP_K = 2

LANE = 128
V7X_VMEM_LIMIT = 56 * 1024 * 1024

SEQ_TILE = 256
GDN_CHUNK = 128
MOE_ROWS = 256
DMA_UNROLL = 8
ROUTE_TILE = 512

P_QKV = 0
P_Z = 3072
P_CQ = 4096
P_CKV = 4608
P_KR = 4864
P_AB = 4992
P_WIDTH = 5376
P_NTILE = 1792


def _params(sem):
    return pltpu.CompilerParams(dimension_semantics=sem, vmem_limit_bytes=V7X_VMEM_LIMIT)


def _dot(a, b):
    return jnp.dot(a, b, preferred_element_type=F32)


def _dot_nt(a, b):
    return lax.dot_general(a, b, (((1,), (1,)), ((), ())), preferred_element_type=F32)


def _dot_tn(a, b):
    return lax.dot_general(a, b, (((0,), (0,)), ((), ())), preferred_element_type=F32)


def _silu(x):
    return x * jax.nn.sigmoid(x)


def _rms(x, g):
    return x * lax.rsqrt(jnp.mean(x * x, axis=-1, keepdims=True) + EPS) * g


def _split3(x):
    hi = x.astype(BF16)
    r = x - hi.astype(F32)
    mid = r.astype(BF16)
    lo = (r - mid.astype(F32)).astype(BF16)
    return hi, mid, lo


def _mod_kernel(c_ref, w_ref, b_ref, o_ref):
    s = _silu(c_ref[...]).astype(BF16)
    o_ref[...] = _dot(s, w_ref[...].astype(BF16)) + b_ref[...]


def _modulation(cc, w_mod, b_mod):
    depth, d, n6 = w_mod.shape
    tn = 1024
    return pl.pallas_call(
        _mod_kernel,
        grid=(depth, n6 // tn),
        in_specs=[pl.BlockSpec((8, d), lambda l, j: (0, 0)),
                  pl.BlockSpec((None, d, tn), lambda l, j: (l, 0, j)),
                  pl.BlockSpec((None, 1, tn), lambda l, j: (l, 0, j))],
        out_specs=pl.BlockSpec((None, 8, tn), lambda l, j: (l, 0, j)),
        out_shape=jax.ShapeDtypeStruct((depth, 8, n6), F32),
        compiler_params=_params(("parallel", "parallel")),
    )(cc, w_mod, b_mod.reshape(depth, 1, n6))


def _inproj_kernel(nct, xc_ref, x_ref, g_ref, shc_ref, scc_ref, sh_ref, sc_ref, w_ref, o_ref):
    i = pl.program_id(2)

    def run(xr, shr, scr):
        h = _rms(xr[...], g_ref[...]) * (1.0 + scr[...]) + shr[...]
        o_ref[...] = _dot(h.astype(BF16), w_ref[...]).astype(o_ref.dtype)

    @pl.when(i < nct)
    def _():
        run(xc_ref, shc_ref, scc_ref)

    @pl.when(i >= nct)
    def _():
        run(x_ref, sh_ref, sc_ref)


def _in_projection(ctx, x, g, shc, scc, sh, sc, w):
    bsz, seq, d = x.shape
    tm = SEQ_TILE
    nct = ctx.shape[1] // tm
    nt = nct + seq // tm
    width = w.shape[1]
    tn = P_NTILE
    return pl.pallas_call(
        functools.partial(_inproj_kernel, nct),
        grid=(width // tn, bsz, nt),
        in_specs=[pl.BlockSpec((None, tm, d), lambda j, b, i: (b, jnp.minimum(i, nct - 1), 0)),
                  pl.BlockSpec((None, tm, d), lambda j, b, i: (b, jnp.maximum(i - nct, 0), 0)),
                  pl.BlockSpec((1, d), lambda j, b, i: (0, 0)),
                  pl.BlockSpec((1, d), lambda j, b, i: (0, 0)),
                  pl.BlockSpec((1, d), lambda j, b, i: (0, 0)),
                  pl.BlockSpec((None, 1, d), lambda j, b, i: (b, 0, 0)),
                  pl.BlockSpec((None, 1, d), lambda j, b, i: (b, 0, 0)),
                  pl.BlockSpec((d, tn), lambda j, b, i: (0, j))],
        out_specs=pl.BlockSpec((None, tm, tn), lambda j, b, i: (b, i, j)),
        out_shape=jax.ShapeDtypeStruct((bsz, nt * tm, width), BF16),
        compiler_params=_params(("parallel", "parallel", "parallel")),
    )(ctx, x, g, shc, scc, sh, sc, w)


GDN_HALO = 16


def _gdn_front_kernel(nct, pm_ref, pp_ref, pn_ref, cw_ref, ab0_ref, ab1_ref, gp_ref,
                      qkv_ref, gb_ref, ext):
    i = pl.program_id(1)
    n = pl.num_programs(1)
    tm = pm_ref.shape[0]
    hl = GDN_HALO
    zero_prev = jnp.logical_or(i == 0, i == nct)
    zero_next = jnp.logical_or(i == nct - 1, i == n - 1)
    ext[hl:hl + tm, :] = pm_ref[...].astype(F32)
    ext[0:hl, :] = jnp.where(zero_prev, 0.0, pp_ref[...].astype(F32))
    ext[hl + tm:2 * hl + tm, :] = jnp.where(zero_next, 0.0, pn_ref[...].astype(F32))
    taps = cw_ref.shape[0]
    nqk = 2 * GDN_HEADS
    for cb in range(qkv_ref.shape[1] // LANE):
        cs = slice(cb * LANE, (cb + 1) * LANE)
        acc = None
        for j in range(taps):
            off = hl - taps // 2 + j
            term = cw_ref[j:j + 1, cs] * ext[off:off + tm, cs]
            acc = term if acc is None else acc + term
        y = _silu(acc)
        if cb < nqk:
            inv = lax.rsqrt(jnp.sum(y * y, axis=-1, keepdims=True) + EPS)
            if cb < GDN_HEADS:
                inv = inv * (GDN_DK ** -0.5)
            y = y * inv
        qkv_ref[:, cs] = y.astype(BF16)
    for d, ab_ref in enumerate((ab0_ref, ab1_ref)):
        a = ab_ref[...].astype(F32)
        lane = lax.broadcasted_iota(I32, a.shape, 1)
        neg_decay = -jnp.exp(gp_ref[d, 0:1, :])
        xx = a + gp_ref[d, 1:2, :]
        softplus = jnp.maximum(xx, 0.0) + jnp.log1p(jnp.exp(-jnp.abs(xx)))
        gate = jnp.where(lane < GDN_HEADS, neg_decay * softplus, jax.nn.sigmoid(a))
        gb_ref[d] = jnp.where(lane < 2 * GDN_HEADS, gate, 0.0)


def _gdn_front(p_all, conv_w, gate_params, nct):
    bsz, ttot, _ = p_all.shape
    tm = SEQ_TILE
    nt = ttot // tm
    cq = P_Z
    hl = GDN_HALO
    r = tm // hl
    nh = ttot // hl
    return pl.pallas_call(
        functools.partial(_gdn_front_kernel, nct),
        grid=(bsz, nt),
        in_specs=[pl.BlockSpec((None, tm, cq), lambda b, i: (b, i, 0)),
                  pl.BlockSpec((None, hl, cq), lambda b, i: (b, jnp.maximum(i * r - 1, 0), 0)),
                  pl.BlockSpec((None, hl, cq), lambda b, i: (b, jnp.minimum((i + 1) * r, nh - 1), 0)),
                  pl.BlockSpec(conv_w.shape, lambda b, i: (0, 0)),
                  pl.BlockSpec((None, tm, LANE), lambda b, i: (b, i, P_AB // LANE)),
                  pl.BlockSpec((None, tm, LANE), lambda b, i: (b, i, P_AB // LANE + 1)),
                  pl.BlockSpec(gate_params.shape, lambda b, i: (0, 0, 0))],
        out_specs=[pl.BlockSpec((None, tm, cq), lambda b, i: (b, i, 0)),
                   pl.BlockSpec((2, None, tm, LANE), lambda b, i: (0, b, i, 0))],
        out_shape=[jax.ShapeDtypeStruct((bsz, ttot, cq), BF16),
                   jax.ShapeDtypeStruct((2, bsz, ttot, LANE), F32)],
        scratch_shapes=[pltpu.VMEM((tm + 2 * hl, cq), F32)],
        compiler_params=_params(("parallel", "parallel")),
    )(p_all, p_all, p_all, conv_w, p_all, p_all, gate_params)


def _gdn_scan_kernel(nct, q_ref, k_ref, v_ref, gb_ref, o_ref, s_scr):
    d = pl.program_id(1)
    s = pl.program_id(2)
    c = q_ref.shape[0]
    dk = GDN_DK

    @pl.when(s == 0)
    def _():
        s_scr[...] = jnp.zeros_like(s_scr)

    ri = lax.broadcasted_iota(I32, (c, c), 0)
    ci = lax.broadcasted_iota(I32, (c, c), 1)
    incl = jnp.where(d == 0, ri - ci, ci - ri) >= 0
    strict = jnp.logical_and(incl, ci != ri)
    gb = gb_ref[...]
    lane = lax.broadcasted_iota(I32, gb.shape, 1)
    g_hi, g_mid, g_lo = _split3(jnp.where(lane < GDN_HEADS, gb, 0.0))
    m_incl = incl.astype(BF16)
    ones = jnp.ones((c, c), BF16)
    gc = _dot(m_incl, g_hi) + _dot(m_incl, g_mid) + _dot(m_incl, g_lo)
    gl = _dot(ones, g_hi) + _dot(ones, g_mid) + _dot(ones, g_lo)
    gc_t = gc.T
    e_gc = jnp.exp(gc)
    e_rest = jnp.exp(gl - gc)
    e_all = jnp.exp(gl)
    eye = (ri == ci).astype(F32)
    pair_masks = []
    for lb in range(int(math.log2(c))):
        same_pair = jnp.right_shift(ri, lb + 1) == jnp.right_shift(ci, lb + 1)
        same_block = jnp.right_shift(ri, lb) == jnp.right_shift(ci, lb)
        pair_masks.append(jnp.logical_and(same_pair, jnp.logical_not(same_block)))
    heads = range(GDN_HEADS)
    col = lambda a, h: a[:, h:h + 1]
    qb = [q_ref[:, h * dk:(h + 1) * dk] for h in heads]
    kb16 = [k_ref[:, h * dk:(h + 1) * dk] for h in heads]
    k = [a.astype(F32) for a in kb16]
    beta = [col(gb, GDN_HEADS + h) for h in heads]
    kbeta = [k[h] * beta[h] for h in heads]
    kk = [_dot_nt(kbeta[h].astype(BF16), kb16[h]) for h in heads]
    qk = [_dot_nt(qb[h], kb16[h]) for h in heads]
    dec = [jnp.where(incl, jnp.exp(jnp.where(incl, col(gc, h) - gc_t[h:h + 1, :], 0.0)), 0.0) for h in heads]
    low = [jnp.where(strict, kk[h] * dec[h], 0.0) for h in heads]
    r = [-jnp.where(pair_masks[0], low[h], 0.0) for h in heads]
    for mask in pair_masks[1:]:
        dmat = [(r[h] + eye).astype(BF16) for h in heads]
        cd = [_dot(jnp.where(mask, low[h], 0.0).astype(BF16), dmat[h]) for h in heads]
        r = [r[h] - _dot(dmat[h], cd[h].astype(BF16)) for h in heads]
    rhs = [jnp.concatenate([v_ref[:, h * dk:(h + 1) * dk].astype(F32) * beta[h], kbeta[h] * col(e_gc, h)], axis=1)
           for h in heads]
    uw16 = [(rhs[h] + _dot(r[h].astype(BF16), rhs[h].astype(BF16))).astype(BF16) for h in heads]
    kt = [_dot_tn((k[h] * col(e_rest, h)).astype(BF16), uw16[h]) for h in heads]
    qu = [_dot((qk[h] * dec[h]).astype(BF16), uw16[h]) for h in heads]
    st = [s_scr[h] for h in heads]
    st16 = [a.astype(BF16) for a in st]
    qeff = [(qb[h].astype(F32) * col(e_gc, h) - qu[h][:, dk:]).astype(BF16) for h in heads]
    outs = [_dot(qeff[h], st16[h]) + qu[h][:, :dk] for h in heads]
    for h in heads:
        s_scr[h] = st[h] * e_all[0:1, h:h + 1] + kt[h][:, :dk] - _dot(kt[h][:, dk:].astype(BF16), st16[h])

    @pl.when(s >= nct)
    def _():
        for h in heads:
            o_ref[:, h * dk:(h + 1) * dk] = outs[h]


def _gdn_scan(qkv_all, gb_all, nct_tokens):
    bsz, ttot, _ = qkv_all.shape
    c = GDN_CHUNK
    nct = nct_tokens // c
    ns = ttot // c
    nlat = ns - nct
    hv = GDN_HEADS * GDN_DK

    def tmap(d, s):
        rev = jnp.where(s < nct, nct - 1 - s, 2 * nct + nlat - 1 - s)
        return jnp.where(d == 0, s, rev)

    def omap(d, s):
        first = jnp.where(d == 0, 0, nlat - 1)
        return jnp.where(s < nct, first, tmap(d, s) - nct)

    return pl.pallas_call(
        functools.partial(_gdn_scan_kernel, nct),
        grid=(bsz, 2, ns),
        in_specs=[pl.BlockSpec((None, c, hv), lambda b, d, s: (b, tmap(d, s), 0)),
                  pl.BlockSpec((None, c, hv), lambda b, d, s: (b, tmap(d, s), 1)),
                  pl.BlockSpec((None, c, hv), lambda b, d, s: (b, tmap(d, s), 2)),
                  pl.BlockSpec((None, None, c, LANE), lambda b, d, s: (d, b, tmap(d, s), 0))],
        out_specs=pl.BlockSpec((None, None, c, hv), lambda b, d, s: (d, b, omap(d, s), 0)),
        out_shape=jax.ShapeDtypeStruct((2, bsz, nlat * c, hv), F32),
        scratch_shapes=[pltpu.VMEM((GDN_HEADS, GDN_DK, GDN_DK), F32)],
        compiler_params=_params(("parallel", "parallel", "arbitrary")),
    )(qkv_all, qkv_all, qkv_all, gb_all)


def _rope_tile(xr, cos, sin):
    lane = lax.broadcasted_iota(I32, xr.shape, 1)
    first_half = (lane % 32) < 16
    rot = jnp.where(first_half, -pltpu.roll(xr, LANE - 16, 1), pltpu.roll(xr, 16, 1))
    return xr * cos + rot * sin


def _mla_q_kernel(c_ref, ga_ref, w_ref, gn_ref, gr_ref, cos_ref, sin_ref, q_ref):
    cn = _rms(c_ref[...].astype(F32), ga_ref[...]).astype(BF16)
    qf = _dot(cn, w_ref[...])
    tm = qf.shape[0]
    lane = lax.broadcasted_iota(I32, (tm, LANE), 1)
    left = lane < ROPE_DIM
    rope_base = MLA_HEADS * NOPE_DIM
    scale = QK_HEAD ** -0.5 * math.log2(math.e)
    cos = cos_ref[...]
    sin = sin_ref[...]
    for hp in range(MLA_HEADS // 2):
        rt = qf[:, rope_base + hp * LANE:rope_base + (hp + 1) * LANE]
        rsq = rt * rt
        ss_left = jnp.sum(jnp.where(left, rsq, 0.0), axis=-1, keepdims=True)
        ss_right = jnp.sum(jnp.where(left, 0.0, rsq), axis=-1, keepdims=True)
        invs = []
        for par, ss_r in ((0, ss_left), (1, ss_right)):
            h = 2 * hp + par
            nope = qf[:, h * NOPE_DIM:(h + 1) * NOPE_DIM]
            ss = jnp.sum(nope * nope, axis=-1, keepdims=True) + ss_r
            inv = lax.rsqrt(ss * (1.0 / QK_HEAD) + EPS)
            invs.append(inv)
            q_ref[h, :, 0:NOPE_DIM] = (nope * inv * gn_ref[...] * scale).astype(BF16)
        inv_lane = jnp.where(left, invs[0], invs[1])
        xr = _rope_tile(rt * inv_lane * gr_ref[...], cos, sin) * scale
        q_ref[2 * hp, :, NOPE_DIM:QK_PAD] = jnp.where(left, xr, 0.0).astype(BF16)
        q_ref[2 * hp + 1, :, NOPE_DIM:QK_PAD] = jnp.where(left, pltpu.roll(xr, ROPE_DIM, 1), 0.0).astype(BF16)


def _mla_queries(p_all, ga, w, gn, gr, cos, sin, nct_tiles, seq):
    bsz = p_all.shape[0]
    tm = SEQ_TILE
    return pl.pallas_call(
        _mla_q_kernel,
        grid=(bsz, seq // tm),
        in_specs=[pl.BlockSpec((None, tm, Q_LORA), lambda b, i: (b, i + nct_tiles, P_CQ // Q_LORA)),
                  pl.BlockSpec(ga.shape, lambda b, i: (0, 0)),
                  pl.BlockSpec(w.shape, lambda b, i: (0, 0)),
                  pl.BlockSpec(gn.shape, lambda b, i: (0, 0)),
                  pl.BlockSpec(gr.shape, lambda b, i: (0, 0)),
                  pl.BlockSpec((tm, LANE), lambda b, i: (i + nct_tiles, 0)),
                  pl.BlockSpec((tm, LANE), lambda b, i: (i + nct_tiles, 0))],
        out_specs=pl.BlockSpec((None, MLA_HEADS, tm, QK_PAD), lambda b, i: (b, 0, i, 0)),
        out_shape=jax.ShapeDtypeStruct((bsz, MLA_HEADS, seq, QK_PAD), BF16),
        compiler_params=_params(("parallel", "parallel")),
    )(p_all, ga, w, gn, gr, cos, sin)


def _mla_kv_kernel(c_ref, kr_ref, ga_ref, w_ref, gn_ref, gr_ref, cos_ref, sin_ref, k_ref, v_ref):
    cn = _rms(c_ref[...].astype(F32), ga_ref[...]).astype(BF16)
    kv = _dot(cn, w_ref[...])
    kr = kr_ref[...].astype(F32)
    ss_r = jnp.sum(kr * kr, axis=-1, keepdims=True)
    kr_rot = _rope_tile(kr * gr_ref[...], cos_ref[...], sin_ref[...])
    width = NOPE_DIM + LANE
    ones_col = (lax.broadcasted_iota(I32, kr.shape, 1) == 0).astype(BF16)
    for h in range(MLA_HEADS):
        nope = kv[:, h * width:h * width + NOPE_DIM]
        ss = jnp.sum(nope * nope, axis=-1, keepdims=True) + ss_r
        inv = lax.rsqrt(ss * (1.0 / QK_HEAD) + EPS)
        k_ref[h, :, 0:NOPE_DIM] = (nope * inv * gn_ref[...]).astype(BF16)
        k_ref[h, :, NOPE_DIM:QK_PAD] = (kr_rot * inv).astype(BF16)
        v_ref[h, :, 0:LANE] = kv[:, h * width + NOPE_DIM:(h + 1) * width].astype(BF16)
        v_ref[h, :, LANE:2 * LANE] = ones_col


def _mla_keys_values(p_all, ga, w, gn, gr, cos, sin):
    bsz, ttot, _ = p_all.shape
    tm = SEQ_TILE
    return pl.pallas_call(
        _mla_kv_kernel,
        grid=(bsz, ttot // tm),
        in_specs=[pl.BlockSpec((None, tm, KV_LORA), lambda b, i: (b, i, P_CKV // KV_LORA)),
                  pl.BlockSpec((None, tm, LANE), lambda b, i: (b, i, P_KR // LANE)),
                  pl.BlockSpec(ga.shape, lambda b, i: (0, 0)),
                  pl.BlockSpec(w.shape, lambda b, i: (0, 0)),
                  pl.BlockSpec(gn.shape, lambda b, i: (0, 0)),
                  pl.BlockSpec(gr.shape, lambda b, i: (0, 0)),
                  pl.BlockSpec((tm, LANE), lambda b, i: (i, 0)),
                  pl.BlockSpec((tm, LANE), lambda b, i: (i, 0))],
        out_specs=[pl.BlockSpec((None, MLA_HEADS, tm, QK_PAD), lambda b, i: (b, 0, i, 0)),
                   pl.BlockSpec((None, MLA_HEADS, tm, 2 * LANE), lambda b, i: (b, 0, i, 0))],
        out_shape=[jax.ShapeDtypeStruct((bsz, MLA_HEADS, ttot, QK_PAD), BF16),
                   jax.ShapeDtypeStruct((bsz, MLA_HEADS, ttot, 2 * LANE), BF16)],
        compiler_params=_params(("parallel", "parallel")),
    )(p_all, p_all, ga, w, gn, gr, cos, sin)


FLASH_ROWS = 32


def _flash_kernel(tk, q_ref, k_ref, v_ref, o_ref, s_a, s_b, p_a, p_b, m_scr, a_scr, acc_scr):
    nk = k_ref.shape[0] // tk
    tq = q_ref.shape[0]
    q = q_ref[...]
    m_scr[...] = jnp.full_like(m_scr, -jnp.inf)
    acc_scr[...] = jnp.zeros_like(acc_scr)

    def keys(ref, j):
        return ref[pl.ds(pl.multiple_of(j * tk, tk), tk), :]

    def scores(j):
        return _dot_nt(q, keys(k_ref, j))

    def update(s_ref, p_ref, j):
        vj = keys(v_ref, j)
        half = tq // 2
        for part in range(2):
            for r in range(half // FLASH_ROWS):
                r0 = part * half + r * FLASH_ROWS
                rs = slice(r0, r0 + FLASH_ROWS)
                s = s_ref[rs, :]
                m_prev = m_scr[rs, :]
                m_new = jnp.maximum(m_prev, jnp.max(s, axis=-1, keepdims=True))
                m_scr[rs, :] = m_new
                a_scr[rs, :] = jnp.exp2(m_prev - m_new)
                p_ref[rs, :] = jnp.exp2(s - m_new).astype(BF16)
            hs = slice(part * half, (part + 1) * half)
            acc_scr[hs, :] = a_scr[hs, :] * acc_scr[hs, :] + _dot(p_ref[hs, :], vj)

    s_a[...] = scores(0)

    def body(i, carry):
        s_b[...] = scores(2 * i + 1)
        update(s_a, p_a, 2 * i)
        s_a[...] = scores(2 * i + 2)
        update(s_b, p_b, 2 * i + 1)
        return carry

    lax.fori_loop(0, (nk - 1) // 2, body, 0)
    if nk % 2 == 1:
        update(s_a, p_a, nk - 1)
    else:
        s_b[...] = scores(nk - 1)
        update(s_a, p_a, nk - 2)
        update(s_b, p_b, nk - 1)
    o_ref[...] = (acc_scr[:, 0:LANE] / acc_scr[:, LANE:LANE + 1]).astype(o_ref.dtype)


def _key_tile(ttot):
    for cand in (1280, 1024, 768, 512, 256, 128):
        if ttot % cand == 0:
            return cand
    raise ValueError("key length must be a multiple of 128")


def _attention(q, k, v):
    bsz, heads, seq, _ = q.shape
    ttot = k.shape[2]
    tq = 512
    tk = _key_tile(ttot)
    return pl.pallas_call(
        functools.partial(_flash_kernel, tk),
        grid=(bsz, heads, seq // tq),
        in_specs=[pl.BlockSpec((None, None, tq, QK_PAD), lambda b, h, i: (b, h, i, 0)),
                  pl.BlockSpec((None, None, ttot, QK_PAD), lambda b, h, i: (b, h, 0, 0)),
                  pl.BlockSpec((None, None, ttot, 2 * LANE), lambda b, h, i: (b, h, 0, 0))],
        out_specs=pl.BlockSpec((None, tq, LANE), lambda b, h, i: (b, i, h)),
        out_shape=jax.ShapeDtypeStruct((bsz, seq, heads * LANE), BF16),
        scratch_shapes=[pltpu.VMEM((tq, tk), F32), pltpu.VMEM((tq, tk), F32),
                        pltpu.VMEM((tq, tk), BF16), pltpu.VMEM((tq, tk), BF16),
                        pltpu.VMEM((tq, 1), F32), pltpu.VMEM((tq, 1), F32),
                        pltpu.VMEM((tq, 2 * LANE), F32)],
        compiler_params=_params(("parallel", "parallel", "arbitrary")),
    )(q, k, v)


def _pack_halves(x):
    w = x.shape[1] // 2
    bits = lax.bitcast_convert_type(x.astype(BF16).astype(F32), U32)
    return jnp.bitwise_or(jnp.right_shift(bits[:, :w], jnp.uint32(16)), bits[:, w:])


def _unpack_halves(wd):
    lo = lax.bitcast_convert_type(jnp.left_shift(wd, jnp.uint32(16)), F32)
    hi = lax.bitcast_convert_type(jnp.bitwise_and(wd, jnp.uint32(0xFFFF0000)), F32)
    return lo, hi


def _ffn_front(xn, gf_ref, sh_ref, sc_ref, wrh_ref, wrl_ref, hp_ref, lg_ref):
    h2 = _rms(xn, gf_ref[...]) * (1.0 + sc_ref[...]) + sh_ref[...]
    hh = h2.astype(BF16)
    hl = (h2 - hh.astype(F32)).astype(BF16)
    hp_ref[...] = _pack_halves(h2)
    lg_ref[...] = _dot(hh, wrh_ref[...]) + _dot(hl, wrh_ref[...]) + _dot(hh, wrl_ref[...])


def _mix_out_kernel(of_ref, ob_ref, z_ref, ym_ref, x_ref, gog_ref, wout_ref, g1_ref,
                    gf_ref, sh_ref, sc_ref, wrh_ref, wrl_ref, xo_ref, hp_ref, lg_ref, mix):
    dv = GDN_DK
    o = of_ref[...] + ob_ref[...]
    for h in range(GDN_HEADS):
        hs = slice(h * dv, (h + 1) * dv)
        y = _rms(o[:, hs], gog_ref[...])
        mix[:, hs] = (y * _silu(z_ref[:, hs].astype(F32))).astype(BF16)
    hv = GDN_HEADS * dv
    mix[:, hv:] = ym_ref[...]
    xn = x_ref[...] + g1_ref[...] * _dot(mix[...], wout_ref[...])
    xo_ref[...] = xn
    _ffn_front(xn, gf_ref, sh_ref, sc_ref, wrh_ref, wrl_ref, hp_ref, lg_ref)


def _mix_out(o2, p_all, ymla, x, gog, wout, g1, gf, sh2, sc2, wrh, wrl, nct_tiles):
    bsz, seq, d = x.shape
    tm = SEQ_TILE
    hv = GDN_HEADS * GDN_DK
    vec = lambda: pl.BlockSpec((None, 1, d), lambda b, i: (b, 0, 0))
    full = lambda a: pl.BlockSpec(a.shape, lambda b, i: (0,) * a.ndim)
    return pl.pallas_call(
        _mix_out_kernel,
        grid=(bsz, seq // tm),
        in_specs=[pl.BlockSpec((None, None, tm, hv), lambda b, i: (0, b, i, 0)),
                  pl.BlockSpec((None, None, tm, hv), lambda b, i: (1, b, i, 0)),
                  pl.BlockSpec((None, tm, hv), lambda b, i: (b, i + nct_tiles, P_Z // hv)),
                  pl.BlockSpec((None, tm, hv), lambda b, i: (b, i, 0)),
                  pl.BlockSpec((None, tm, d), lambda b, i: (b, i, 0)),
                  full(gog), full(wout), vec(), full(gf), vec(), vec(), full(wrh), full(wrl)],
        out_specs=[pl.BlockSpec((None, tm, d), lambda b, i: (b, i, 0)),
                   pl.BlockSpec((None, tm, d // 2), lambda b, i: (b, i, 0)),
                   pl.BlockSpec((None, tm, LANE), lambda b, i: (b, i, 0))],
        out_shape=[jax.ShapeDtypeStruct((bsz, seq, d), F32),
                   jax.ShapeDtypeStruct((bsz, seq, d // 2), U32),
                   jax.ShapeDtypeStruct((bsz, seq, LANE), F32)],
        scratch_shapes=[pltpu.VMEM((tm, wout.shape[0]), BF16)],
        compiler_params=_params(("parallel", "parallel")),
    )(o2, o2, p_all, ymla, x, gog, wout, g1, gf, sh2, sc2, wrh, wrl)


POOL_HALO = 8


def _pool_kernel(seq, xm_ref, xp_ref, xn_ref, gm_ref, sh1_ref, sc1_ref, wp_ref, ps_ref, g1_ref,
                 gf_ref, sh_ref, sc_ref, wrh_ref, wrl_ref, xo_ref, hp_ref, lg_ref, ext):
    i = pl.program_id(1)
    n = pl.num_programs(1)
    tm = xm_ref.shape[0]
    hl = POOL_HALO

    def normed(ref):
        return _rms(ref[...], gm_ref[...]) * (1.0 + sc1_ref[...]) + sh1_ref[...]

    ext[hl:hl + tm, :] = normed(xm_ref)
    ext[0:hl, :] = jnp.where(i == 0, 0.0, normed(xp_ref))
    ext[hl + tm:2 * hl + tm, :] = jnp.where(i == n - 1, 0.0, normed(xn_ref))
    t = i * tm + lax.broadcasted_iota(I32, (tm, 1), 0)
    gw = xm_ref.shape[1] // len(POOL_WINDOWS)
    for gi, win in enumerate(POOL_WINDOWS):
        cs = slice(gi * gw, (gi + 1) * gw)
        half = win // 2
        acc = None
        for off in range(-half, win - half):
            term = ext[hl + off:hl + off + tm, cs]
            acc = term if acc is None else acc + term
        lo = jnp.clip(t - half, 0, seq)
        hi = jnp.clip(t - half + win, 0, seq)
        pooled = acc / (hi - lo).astype(F32) - ext[hl:hl + tm, cs]
        y = _dot(pooled.astype(BF16), wp_ref[gi]) * ps_ref[:, cs]
        xo_ref[:, cs] = xm_ref[:, cs] + g1_ref[:, cs] * y
    _ffn_front(xo_ref[...], gf_ref, sh_ref, sc_ref, wrh_ref, wrl_ref, hp_ref, lg_ref)


def _pool_mixer(x, gm, sh1, sc1, wp, ps, g1, gf, sh2, sc2, wrh, wrl):
    bsz, seq, d = x.shape
    tm = SEQ_TILE
    hl = POOL_HALO
    r = tm // hl
    nh = seq // hl
    vec = lambda: pl.BlockSpec((None, 1, d), lambda b, i: (b, 0, 0))
    full = lambda a: pl.BlockSpec(a.shape, lambda b, i: (0,) * a.ndim)
    return pl.pallas_call(
        functools.partial(_pool_kernel, seq),
        grid=(bsz, seq // tm),
        in_specs=[pl.BlockSpec((None, tm, d), lambda b, i: (b, i, 0)),
                  pl.BlockSpec((None, hl, d), lambda b, i: (b, jnp.maximum(i * r - 1, 0), 0)),
                  pl.BlockSpec((None, hl, d), lambda b, i: (b, jnp.minimum((i + 1) * r, nh - 1), 0)),
                  full(gm), vec(), vec(), full(wp), full(ps), vec(),
                  full(gf), vec(), vec(), full(wrh), full(wrl)],
        out_specs=[pl.BlockSpec((None, tm, d), lambda b, i: (b, i, 0)),
                   pl.BlockSpec((None, tm, d // 2), lambda b, i: (b, i, 0)),
                   pl.BlockSpec((None, tm, LANE), lambda b, i: (b, i, 0))],
        out_shape=[jax.ShapeDtypeStruct((bsz, seq, d), F32),
                   jax.ShapeDtypeStruct((bsz, seq, d // 2), U32),
                   jax.ShapeDtypeStruct((bsz, seq, LANE), F32)],
        scratch_shapes=[pltpu.VMEM((tm + 2 * hl, d), F32)],
        compiler_params=_params(("parallel", "parallel")),
    )(x, x, x, gm, sh1, sc1, wp, ps, g1, gf, sh2, sc2, wrh, wrl)


def _first_max(vals, idx, sentinel):
    m = jnp.max(vals, axis=0, keepdims=True)
    first = jnp.min(jnp.where(vals == m, idx, sentinel), axis=0, keepdims=True)
    return m, first


def _route_kernel(rows, lg_ref, bias_ref, dest_ref, gates_ref, cnt_ref, carry):
    phase = pl.program_id(0)
    step = pl.program_id(1)

    @pl.when(jnp.logical_and(phase == 0, step == 0))
    def _():
        carry[...] = jnp.zeros_like(carry)

    @pl.when(jnp.logical_and(phase == 1, step == 0))
    def _():
        shift = int(math.log2(rows))
        counts_row = carry[...].T.astype(I32)
        padded = jnp.left_shift(jnp.right_shift(counts_row + (rows - 1), shift), shift).astype(F32)
        i_exp = lax.broadcasted_iota(I32, padded.shape, 0)
        j_exp = lax.broadcasted_iota(I32, padded.shape, 1)
        first_row = jnp.sum(jnp.where(j_exp < i_exp, padded, 0.0), axis=1, keepdims=True)
        carry[...] = jnp.broadcast_to(first_row, carry.shape)

    tm = lg_ref.shape[0]
    epg = EXPERTS_PER_GROUP
    scores = jax.nn.sigmoid(lg_ref[...].T[0:N_EXPERTS, :])
    biased = scores + bias_ref[:, 0:1]
    eidx = lax.broadcasted_iota(I32, (epg, tm), 0)
    best_score = best_grp = best_i1 = best_i2 = None
    for g in range(N_GROUPS):
        blk = biased[g * epg:(g + 1) * epg, :]
        m1, i1 = _first_max(blk, eidx, epg)
        m2, i2 = _first_max(jnp.where(eidx == i1, -jnp.inf, blk), eidx, epg)
        gs = m1 + m2
        if g == 0:
            best_score, best_grp, best_i1, best_i2 = gs, jnp.zeros_like(i1), i1, i2
        else:
            better = gs > best_score
            best_score = jnp.where(better, gs, best_score)
            best_grp = jnp.where(better, g, best_grp)
            best_i1 = jnp.where(better, i1, best_i1)
            best_i2 = jnp.where(better, i2, best_i2)
    e1 = best_grp * epg + best_i1
    e2 = best_grp * epg + best_i2
    eall = lax.broadcasted_iota(I32, (LANE, tm), 0)
    oh1 = eall == e1
    oh2 = eall == e2
    oh = jnp.logical_or(oh1, oh2).astype(BF16)
    picked = jnp.sum(oh.astype(F32), axis=1, keepdims=True)

    @pl.when(phase == 0)
    def _():
        carry[...] = carry[...] + picked
        cnt_ref[...] = carry[...]

    @pl.when(phase == 1)
    def _():
        s1 = jnp.sum(jnp.where(oh1[0:N_EXPERTS], scores, 0.0), axis=0, keepdims=True)
        s2 = jnp.sum(jnp.where(oh2[0:N_EXPERTS], scores, 0.0), axis=0, keepdims=True)
        denom = s1 + s2
        earlier = (lax.broadcasted_iota(I32, (tm, tm), 0) < lax.broadcasted_iota(I32, (tm, tm), 1)).astype(BF16)
        row = _dot(oh, earlier) + carry[:, 0:1]
        d1 = jnp.sum(jnp.where(oh1, row, 0.0), axis=0, keepdims=True)
        d2 = jnp.sum(jnp.where(oh2, row, 0.0), axis=0, keepdims=True)
        carry[...] = carry[...] + picked
        dest_ref[...] = jnp.concatenate([d1.astype(I32), d2.astype(I32), jnp.zeros((6, tm), I32)], axis=0)
        gates = jnp.concatenate([s1 / denom, s2 / denom, jnp.zeros((LANE - 2, tm), F32)], axis=0)
        gates_ref[...] = gates.T


def _route(logits, bias_col):
    n = logits.shape[0]
    tm = ROUTE_TILE
    rows = MOE_ROWS
    assert rows & (rows - 1) == 0
    return pl.pallas_call(
        functools.partial(_route_kernel, rows),
        grid=(2, n // tm),
        in_specs=[pl.BlockSpec((tm, LANE), lambda p, t: (t, 0)),
                  pl.BlockSpec(bias_col.shape, lambda p, t: (0, 0))],
        out_specs=[pl.BlockSpec((8, tm), lambda p, t: (0, p * t)),
                   pl.BlockSpec((tm, LANE), lambda p, t: (p * t, 0)),
                   pl.BlockSpec((LANE, LANE), lambda p, t: (0, 0))],
        out_shape=[jax.ShapeDtypeStruct((8, n), I32),
                   jax.ShapeDtypeStruct((n, LANE), F32),
                   jax.ShapeDtypeStruct((LANE, LANE), F32)],
        scratch_shapes=[pltpu.VMEM((LANE, LANE), F32)],
        compiler_params=_params(("arbitrary", "arbitrary")),
    )(logits, bias_col)


def _dispatch_copy(hp_ref, xs_hbm, sem, j, dst):
    return pltpu.make_async_copy(hp_ref.at[pl.ds(j, 1), :], xs_hbm.at[pl.ds(dst, 1), :], sem)


def _dispatch_kernel(tm, d0_ref, d1_ref, hp_ref, xs_in, xs_hbm, sem):
    del xs_in

    def issue(j, carry):
        _dispatch_copy(hp_ref, xs_hbm, sem, j, d0_ref[0, j]).start()
        _dispatch_copy(hp_ref, xs_hbm, sem, j, d1_ref[0, j]).start()
        return carry

    lax.fori_loop(0, tm, issue, 0, unroll=DMA_UNROLL)
    for _ in range(2):
        pltpu.make_async_copy(hp_ref, xs_hbm.at[pl.ds(0, tm), :], sem).wait()


def _dispatch(hp, dest, n_rows):
    n, half = hp.shape
    tm = ROUTE_TILE
    nt = n // tm
    idx = lambda k: pl.BlockSpec((None, None, 1, tm), lambda i: (k, i, 0, 0), memory_space=pltpu.SMEM)
    dest4 = dest.reshape(dest.shape[0], nt, 1, tm)
    return pl.pallas_call(
        functools.partial(_dispatch_kernel, tm),
        grid=(nt,),
        in_specs=[idx(0), idx(1),
                  pl.BlockSpec((tm, half), lambda i: (i, 0)),
                  pl.BlockSpec(memory_space=pl.ANY)],
        out_specs=pl.BlockSpec(memory_space=pl.ANY),
        out_shape=jax.ShapeDtypeStruct((n_rows, half), U32),
        input_output_aliases={3: 0},
        scratch_shapes=[pltpu.SemaphoreType.DMA(())],
        compiler_params=_params(("arbitrary",)),
    )(dest4, dest4, hp, jnp.zeros((n_rows, half), U32))


def _experts_kernel(be_ref, nu_ref, xs_ref, wgu_ref, wd_ref, y_ref, wgu16, wd16):
    i = pl.program_id(0)

    @pl.when(i < nu_ref[0])
    def _():
        @pl.when(jnp.logical_or(i == 0, be_ref[i] != be_ref[jnp.maximum(i - 1, 0)]))
        def _():
            wgu16[...] = wgu_ref[...].astype(BF16)
            wd16[...] = wd_ref[...].astype(BF16)

        lo, hi = _unpack_halves(xs_ref[...])
        half = lo.shape[1]
        gu = _dot(lo.astype(BF16), wgu16[0:half, :]) + _dot(hi.astype(BF16), wgu16[half:, :])
        f = gu.shape[1] // 2
        hid = _silu(gu[:, :f]) * gu[:, f:]
        y_ref[...] = _pack_halves(_dot(hid.astype(BF16), wd16[...]))

    @pl.when(i >= nu_ref[0])
    def _():
        y_ref[...] = jnp.zeros_like(y_ref)


def _experts(layer, block_expert, n_used, xs, w_gate_up, w_down):
    n_rows, half = xs.shape
    rows = MOE_ROWS
    d = 2 * half
    f2 = w_gate_up.shape[3]
    grid_spec = pltpu.PrefetchScalarGridSpec(
        num_scalar_prefetch=2,
        grid=(n_rows // rows,),
        in_specs=[pl.BlockSpec((rows, half), lambda i, be, nu: (i, 0)),
                  pl.BlockSpec((None, None, d, f2), lambda i, be, nu: (layer, be[i], 0, 0)),
                  pl.BlockSpec((None, None, f2 // 2, d), lambda i, be, nu: (layer, be[i], 0, 0))],
        out_specs=pl.BlockSpec((rows, half), lambda i, be, nu: (i, 0)),
        scratch_shapes=[pltpu.VMEM((d, f2), BF16), pltpu.VMEM((f2 // 2, d), BF16)],
    )
    return pl.pallas_call(
        _experts_kernel,
        grid_spec=grid_spec,
        out_shape=jax.ShapeDtypeStruct((n_rows, half), U32),
        compiler_params=_params(("arbitrary",)),
    )(block_expert, n_used, xs, w_gate_up, w_down)


def _combine_copy(y_hbm, buf, sem, slot, k, src, j):
    return pltpu.make_async_copy(y_hbm.at[pl.ds(src, 1), :], buf.at[slot, k, pl.ds(j, 1), :], sem.at[slot])


def _combine_kernel(tm, d0_ref, d1_ref, d0n_ref, d1n_ref, y_hbm, gates_ref, x_ref, g2_ref, xo_ref, buf, sem):
    step = pl.program_id(0) * pl.num_programs(1) + pl.program_id(1)
    total = pl.num_programs(0) * pl.num_programs(1)
    slot = step % 2

    def start_gather(r0_ref, r1_ref, s):
        def issue(j, carry):
            _combine_copy(y_hbm, buf, sem, s, 0, r0_ref[0, j], j).start()
            _combine_copy(y_hbm, buf, sem, s, 1, r1_ref[0, j], j).start()
            return carry
        lax.fori_loop(0, tm, issue, 0, unroll=DMA_UNROLL)

    @pl.when(step == 0)
    def _():
        start_gather(d0_ref, d1_ref, 0)

    @pl.when(step + 1 < total)
    def _():
        start_gather(d0n_ref, d1n_ref, 1 - slot)

    for k in range(2):
        pltpu.make_async_copy(y_hbm.at[pl.ds(0, tm), :], buf.at[slot, k], sem.at[slot]).wait()
    lo0, hi0 = _unpack_halves(buf[slot, 0])
    lo1, hi1 = _unpack_halves(buf[slot, 1])
    half = lo0.shape[1]
    ga = gates_ref[:, 0:1]
    gb = gates_ref[:, 1:2]
    xo_ref[:, 0:half] = x_ref[:, 0:half] + g2_ref[:, 0:half] * (ga * lo0 + gb * lo1)
    xo_ref[:, half:] = x_ref[:, half:] + g2_ref[:, half:] * (ga * hi0 + gb * hi1)


def _combine(y, dest, gates, x, g2):
    bsz, seq, d = x.shape
    tm = SEQ_TILE
    nt = seq // tm
    half = d // 2
    last = bsz * nt - 1
    idx = lambda k, ahead: pl.BlockSpec(
        (None, None, 1, tm), lambda b, i: (k, jnp.minimum(b * nt + i + ahead, last), 0, 0),
        memory_space=pltpu.SMEM)
    dest4 = dest.reshape(dest.shape[0], bsz * nt, 1, tm)
    return pl.pallas_call(
        functools.partial(_combine_kernel, tm),
        grid=(bsz, nt),
        in_specs=[idx(0, 0), idx(1, 0), idx(0, 1), idx(1, 1),
                  pl.BlockSpec(memory_space=pl.ANY),
                  pl.BlockSpec((tm, LANE), lambda b, i: (b * nt + i, 0)),
                  pl.BlockSpec((None, tm, d), lambda b, i: (b, i, 0)),
                  pl.BlockSpec((None, 1, d), lambda b, i: (b, 0, 0))],
        out_specs=pl.BlockSpec((None, tm, d), lambda b, i: (b, i, 0)),
        out_shape=jax.ShapeDtypeStruct((bsz, seq, d), F32),
        scratch_shapes=[pltpu.VMEM((2, 2, tm, half), U32), pltpu.SemaphoreType.DMA((2,))],
        compiler_params=_params(("arbitrary", "arbitrary")),
    )(dest4, dest4, dest4, dest4, y, gates, x, g2)


def _moe(layer, x, hp, logits, g2, bias_col, w_gate_up, w_down):
    bsz, seq, d = x.shape
    n = bsz * seq
    rows = MOE_ROWS
    dest, gates, cnt = _route(logits.reshape(n, LANE), bias_col)
    counts = cnt[:N_EXPERTS, 0].astype(I32)
    pad_end = jnp.cumsum((counts + rows - 1) // rows * rows)
    n_rows = -(-(n * TOP_K + N_EXPERTS * (rows - 1)) // rows) * rows
    block_start = jnp.arange(n_rows // rows, dtype=I32) * rows
    block_expert = jnp.minimum(jnp.sum((pad_end[None, :] <= block_start[:, None]).astype(I32), axis=1),
                               N_EXPERTS - 1)
    n_used = pad_end[-1:] // rows
    xs = _dispatch(hp.reshape(n, d // 2), dest, n_rows)
    y = _experts(layer, block_expert, n_used, xs, w_gate_up, w_down)
    return _combine(y, dest, gates, x, g2)


def _permute_w_in(w_in):
    d = w_in.shape[0]
    hq = GDN_HEADS * GDN_DK
    off_a = 4 * hq
    off_cq = off_a + 4 * GDN_HEADS
    off_ckv = off_cq + Q_LORA
    off_kr = off_ckv + KV_LORA
    z = lambda n: jnp.zeros((d, n), w_in.dtype)
    a = lambda k: w_in[:, off_a + k * GDN_HEADS:off_a + (k + 1) * GDN_HEADS]
    pad_ab = LANE - 2 * GDN_HEADS
    cols = [w_in[:, :off_a], w_in[:, off_cq:off_kr + ROPE_DIM], z(LANE - ROPE_DIM),
            a(0), a(2), z(pad_ab), a(1), a(3), z(pad_ab)]
    w = jnp.concatenate(cols, axis=1)
    return jnp.concatenate([w, z(P_WIDTH - w.shape[1])], axis=1).astype(BF16)


def _rope_tables(rows, n_ctx):
    row = jnp.repeat(jnp.arange(rows, dtype=F32), GRID_W)
    col = jnp.tile(jnp.arange(GRID_W, dtype=F32), rows)
    pairs = ROPE_DIM // 4
    inv_freq = ROPE_THETA ** (-jnp.arange(pairs, dtype=F32) / pairs)
    ang_r = row[:, None] * inv_freq
    ang_c = col[:, None] * inv_freq
    ang = jnp.concatenate([ang_r, ang_r, ang_c, ang_c], axis=-1)
    ang = jnp.concatenate([jnp.zeros((n_ctx, ROPE_DIM), F32), ang], axis=0)
    cos, sin = jnp.cos(ang), jnp.sin(ang)
    return jnp.concatenate([cos, cos], axis=1), jnp.concatenate([sin, sin], axis=1)


def kernel(x, c, ctx, c_ctx, w_mod, b_mod, norm_mix_g, norm_ffn_g, w_in, conv_qkv, a_log_fwd, a_log_bwd, dt_bias_fwd, dt_bias_bwd, gdn_out_g, q_a_norm_g, w_uq, kv_a_norm_g, w_ukv, q_norm_g, k_norm_g, w_out, w_pool, pool_scale, w_router, router_bias, w_gate_up, w_down):
    bsz, seq, d = x.shape
    n_ctx = ctx.shape[1]
    depth = w_mod.shape[0]
    assert depth == 2, "the context stream is only read: no layer after the first even layer reads it"
    nct_tiles = n_ctx // SEQ_TILE

    cc = jnp.concatenate([c, c_ctx[None, :], jnp.zeros((8 - bsz - 1, d), F32)], axis=0)
    mod = _modulation(cc, w_mod, b_mod).reshape(depth, 8, 6, d)

    def mods(layer):
        m = mod[layer]
        return [m[:bsz, k][:, None, :] for k in range(6)], [m[bsz:bsz + 1, k] for k in range(6)]

    wr = jnp.pad(w_router, ((0, 0), (0, LANE - N_EXPERTS)))
    wrh = wr.astype(BF16)
    wrl = (wr - wrh.astype(F32)).astype(BF16)
    bias_col = jnp.broadcast_to(router_bias.astype(F32)[:, None], (N_EXPERTS, LANE))
    row = lambda v: v.astype(F32)[None, :]

    for layer in range(depth):
        j = layer // 2
        (sh1, sc1, g1, sh2, sc2, g2), (csh1, csc1, _, _, _, _) = mods(layer)
        gm = row(norm_mix_g[layer])
        gf = row(norm_ffn_g[layer])
        if layer % 2 == 0:
            p_all = _in_projection(ctx, x, gm, csh1, csc1, sh1, sc1, _permute_w_in(w_in[j]))
            conv_w = conv_qkv[j].astype(F32)
            gate_params = jnp.zeros((2, 8, LANE), F32)
            gate_params = gate_params.at[0, 0, :GDN_HEADS].set(a_log_fwd[j]).at[0, 1, :GDN_HEADS].set(dt_bias_fwd[j])
            gate_params = gate_params.at[1, 0, :GDN_HEADS].set(a_log_bwd[j]).at[1, 1, :GDN_HEADS].set(dt_bias_bwd[j])
            qkv_all, gb_all = _gdn_front(p_all, conv_w, gate_params, nct_tiles)
            o2 = _gdn_scan(qkv_all, gb_all, n_ctx)

            cos, sin = _rope_tables(seq // GRID_W, n_ctx)
            gq = q_norm_g[j].astype(F32)
            gk = k_norm_g[j].astype(F32)
            wq = w_uq[j].reshape(Q_LORA, MLA_HEADS, QK_HEAD)
            wq = jnp.concatenate([wq[:, :, :NOPE_DIM].reshape(Q_LORA, -1),
                                  wq[:, :, NOPE_DIM:].reshape(Q_LORA, -1)], axis=1).astype(BF16)
            q = _mla_queries(p_all, row(q_a_norm_g[j]), wq, gq[None, :NOPE_DIM],
                             jnp.tile(gq[NOPE_DIM:], 2)[None, :], cos, sin, nct_tiles, seq)
            gk_rope = jnp.concatenate([gk[NOPE_DIM:], jnp.zeros((LANE - ROPE_DIM,), F32)])[None, :]
            k_all, v_all = _mla_keys_values(p_all, row(kv_a_norm_g[j]), w_ukv[j].astype(BF16),
                                            gk[None, :NOPE_DIM], gk_rope, cos, sin)
            ymla = _attention(q, k_all, v_all)
            x, hp, logits = _mix_out(o2, p_all, ymla, x, row(gdn_out_g[j]), w_out[j].astype(BF16), g1,
                                     gf, sh2, sc2, wrh, wrl, nct_tiles)
        else:
            x, hp, logits = _pool_mixer(x, gm, sh1, sc1, w_pool[j].astype(BF16), row(pool_scale[j]), g1,
                                        gf, sh2, sc2, wrh, wrl)
        x = _moe(layer, x, hp, logits, g2, bias_col, w_gate_up, w_down)
    return x
```

```python
import functools
import math

import jax
import jax.numpy as jnp
from jax import lax
from jax.experimental import pallas as pl
from jax.experimental.pallas import tpu as pltpu

F32 = jnp.float32
BF16 = jnp.bfloat16
U32 = jnp.uint32
I32 = jnp.int32

EPS = 1e-6
GRID_W = 64
GDN_HEADS = 8
GDN_DK = 128
MLA_HEADS = 8
NOPE_DIM = 128
ROPE_DIM = 64
QK_HEAD = NOPE_DIM + ROPE_DIM
QK_PAD = 256
ROPE_THETA = 10000.0
Q_LORA = 512
KV_LORA = 256
POOL_WINDOWS = (2, 4, 8, 16)
N_EXPERTS = 64
EXPERTS_PER_GROUP = 8
N_GROUPS = N_EXPERTS // EXPERTS_PER_GROUP
TOP_K = 2

LANE = 128
V7X_VMEM_LIMIT = 56 * 1024 * 1024

SEQ_TILE = 256
GDN_CHUNK = 128
MOE_ROWS = 256
DMA_UNROLL = 8
ROUTE_TILE = 512

P_QKV = 0
P_Z = 3072
P_CQ = 4096
P_CKV = 4608
P_KR = 4864
P_AB = 4992
P_WIDTH = 5376
P_NTILE = 1792


def _params(sem):
    return pltpu.CompilerParams(dimension_semantics=sem, vmem_limit_bytes=V7X_VMEM_LIMIT)


def _dot(a, b):
    return jnp.dot(a, b, preferred_element_type=F32)


def _dot_nt(a, b):
    return lax.dot_general(a, b, (((1,), (1,)), ((), ())), preferred_element_type=F32)


def _dot_tn(a, b):
    return lax.dot_general(a, b, (((0,), (0,)), ((), ())), preferred_element_type=F32)


def _silu(x):
    return x * jax.nn.sigmoid(x)


def _rms(x, g):
    return x * lax.rsqrt(jnp.mean(x * x, axis=-1, keepdims=True) + EPS) * g


def _split3(x):
    hi = x.astype(BF16)
    r = x - hi.astype(F32)
    mid = r.astype(BF16)
    lo = (r - mid.astype(F32)).astype(BF16)
    return hi, mid, lo


def _mod_kernel(c_ref, w_ref, b_ref, o_ref):
    s = _silu(c_ref[...]).astype(BF16)
    o_ref[...] = _dot(s, w_ref[...].astype(BF16)) + b_ref[...]


def _modulation(cc, w_mod, b_mod):
    depth, d, n6 = w_mod.shape
    tn = 1024
    return pl.pallas_call(
        _mod_kernel,
        grid=(depth, n6 // tn),
        in_specs=[pl.BlockSpec((8, d), lambda l, j: (0, 0)),
                  pl.BlockSpec((None, d, tn), lambda l, j: (l, 0, j)),
                  pl.BlockSpec((None, 1, tn), lambda l, j: (l, 0, j))],
        out_specs=pl.BlockSpec((None, 8, tn), lambda l, j: (l, 0, j)),
        out_shape=jax.ShapeDtypeStruct((depth, 8, n6), F32),
        compiler_params=_params(("parallel", "parallel")),
    )(cc, w_mod, b_mod.reshape(depth, 1, n6))


def _inproj_kernel(nct, xc_ref, x_ref, g_ref, shc_ref, scc_ref, sh_ref, sc_ref, w_ref, o_ref):
    i = pl.program_id(2)

    def run(xr, shr, scr):
        h = _rms(xr[...], g_ref[...]) * (1.0 + scr[...]) + shr[...]
        o_ref[...] = _dot(h.astype(BF16), w_ref[...]).astype(o_ref.dtype)

    @pl.when(i < nct)
    def _():
        run(xc_ref, shc_ref, scc_ref)

    @pl.when(i >= nct)
    def _():
        run(x_ref, sh_ref, sc_ref)


def _in_projection(ctx, x, g, shc, scc, sh, sc, w):
    bsz, seq, d = x.shape
    tm = SEQ_TILE
    nct = ctx.shape[1] // tm
    nt = nct + seq // tm
    width = w.shape[1]
    tn = P_NTILE
    return pl.pallas_call(
        functools.partial(_inproj_kernel, nct),
        grid=(width // tn, bsz, nt),
        in_specs=[pl.BlockSpec((None, tm, d), lambda j, b, i: (b, jnp.minimum(i, nct - 1), 0)),
                  pl.BlockSpec((None, tm, d), lambda j, b, i: (b, jnp.maximum(i - nct, 0), 0)),
                  pl.BlockSpec((1, d), lambda j, b, i: (0, 0)),
                  pl.BlockSpec((1, d), lambda j, b, i: (0, 0)),
                  pl.BlockSpec((1, d), lambda j, b, i: (0, 0)),
                  pl.BlockSpec((None, 1, d), lambda j, b, i: (b, 0, 0)),
                  pl.BlockSpec((None, 1, d), lambda j, b, i: (b, 0, 0)),
                  pl.BlockSpec((d, tn), lambda j, b, i: (0, j))],
        out_specs=pl.BlockSpec((None, tm, tn), lambda j, b, i: (b, i, j)),
        out_shape=jax.ShapeDtypeStruct((bsz, nt * tm, width), BF16),
        compiler_params=_params(("parallel", "parallel", "parallel")),
    )(ctx, x, g, shc, scc, sh, sc, w)


GDN_HALO = 16


def _gdn_front_kernel(nct, pm_ref, pp_ref, pn_ref, cw_ref, ab0_ref, ab1_ref, gp_ref,
                      qkv_ref, gb_ref, ext):
    i = pl.program_id(1)
    n = pl.num_programs(1)
    tm = pm_ref.shape[0]
    hl = GDN_HALO
    zero_prev = jnp.logical_or(i == 0, i == nct)
    zero_next = jnp.logical_or(i == nct - 1, i == n - 1)
    ext[hl:hl + tm, :] = pm_ref[...].astype(F32)
    ext[0:hl, :] = jnp.where(zero_prev, 0.0, pp_ref[...].astype(F32))
    ext[hl + tm:2 * hl + tm, :] = jnp.where(zero_next, 0.0, pn_ref[...].astype(F32))
    taps = cw_ref.shape[0]
    nqk = 2 * GDN_HEADS
    for cb in range(qkv_ref.shape[1] // LANE):
        cs = slice(cb * LANE, (cb + 1) * LANE)
        acc = None
        for j in range(taps):
            off = hl - taps // 2 + j
            term = cw_ref[j:j + 1, cs] * ext[off:off + tm, cs]
            acc = term if acc is None else acc + term
        y = _silu(acc)
        if cb < nqk:
            inv = lax.rsqrt(jnp.sum(y * y, axis=-1, keepdims=True) + EPS)
            if cb < GDN_HEADS:
                inv = inv * (GDN_DK ** -0.5)
            y = y * inv
        qkv_ref[:, cs] = y.astype(BF16)
    for d, ab_ref in enumerate((ab0_ref, ab1_ref)):
        a = ab_ref[...].astype(F32)
        lane = lax.broadcasted_iota(I32, a.shape, 1)
        neg_decay = -jnp.exp(gp_ref[d, 0:1, :])
        xx = a + gp_ref[d, 1:2, :]
        softplus = jnp.maximum(xx, 0.0) + jnp.log1p(jnp.exp(-jnp.abs(xx)))
        gate = jnp.where(lane < GDN_HEADS, neg_decay * softplus, jax.nn.sigmoid(a))
        gb_ref[d] = jnp.where(lane < 2 * GDN_HEADS, gate, 0.0)


def _gdn_front(p_all, conv_w, gate_params, nct):
    bsz, ttot, _ = p_all.shape
    tm = SEQ_TILE
    nt = ttot // tm
    cq = P_Z
    hl = GDN_HALO
    r = tm // hl
    nh = ttot // hl
    return pl.pallas_call(
        functools.partial(_gdn_front_kernel, nct),
        grid=(bsz, nt),
        in_specs=[pl.BlockSpec((None, tm, cq), lambda b, i: (b, i, 0)),
                  pl.BlockSpec((None, hl, cq), lambda b, i: (b, jnp.maximum(i * r - 1, 0), 0)),
                  pl.BlockSpec((None, hl, cq), lambda b, i: (b, jnp.minimum((i + 1) * r, nh - 1), 0)),
                  pl.BlockSpec(conv_w.shape, lambda b, i: (0, 0)),
                  pl.BlockSpec((None, tm, LANE), lambda b, i: (b, i, P_AB // LANE)),
                  pl.BlockSpec((None, tm, LANE), lambda b, i: (b, i, P_AB // LANE + 1)),
                  pl.BlockSpec(gate_params.shape, lambda b, i: (0, 0, 0))],
        out_specs=[pl.BlockSpec((None, tm, cq), lambda b, i: (b, i, 0)),
                   pl.BlockSpec((2, None, tm, LANE), lambda b, i: (0, b, i, 0))],
        out_shape=[jax.ShapeDtypeStruct((bsz, ttot, cq), BF16),
                   jax.ShapeDtypeStruct((2, bsz, ttot, LANE), F32)],
        scratch_shapes=[pltpu.VMEM((tm + 2 * hl, cq), F32)],
        compiler_params=_params(("parallel", "parallel")),
    )(p_all, p_all, p_all, conv_w, p_all, p_all, gate_params)


def _gdn_scan_kernel(q_ref, k_ref, v_ref, gb_ref, o_ref, s_scr):
    d = pl.program_id(1)
    s = pl.program_id(2)
    c = GDN_CHUNK
    n_sub = q_ref.shape[0] // c
    dk = GDN_DK

    @pl.when(s == 0)
    def _():
        s_scr[...] = jnp.zeros_like(s_scr)

    ri = lax.broadcasted_iota(I32, (c, c), 0)
    ci = lax.broadcasted_iota(I32, (c, c), 1)
    incl = jnp.where(d == 0, ri - ci, ci - ri) >= 0
    strict = jnp.logical_and(incl, ci != ri)
    m_incl = incl.astype(BF16)
    ones = jnp.ones((c, c), BF16)
    eye = (ri == ci).astype(F32)
    pair_masks = []
    for lb in range(int(math.log2(c))):
        same_pair = jnp.right_shift(ri, lb + 1) == jnp.right_shift(ci, lb + 1)
        same_block = jnp.right_shift(ri, lb) == jnp.right_shift(ci, lb)
        pair_masks.append(jnp.logical_and(same_pair, jnp.logical_not(same_block)))
    heads = range(GDN_HEADS)
    col = lambda a, h: a[:, h:h + 1]

    rows = [pl.ds(pl.multiple_of(jnp.where(d == 0, sub, n_sub - 1 - sub) * c, c), c) for sub in range(n_sub)]
    subs = range(n_sub)
    pairs = [(sub, h) for sub in subs for h in heads]
    hcols = lambda ref, sub, h: ref[rows[sub], h * dk:(h + 1) * dk]
    gb = [gb_ref[rows[sub], :] for sub in subs]
    lane = lax.broadcasted_iota(I32, gb[0].shape, 1)
    gsplit = [_split3(jnp.where(lane < GDN_HEADS, gb[sub], 0.0)) for sub in subs]
    gc = [sum(_dot(m_incl, t) for t in gsplit[sub]) for sub in subs]
    gl = [sum(_dot(ones, t) for t in gsplit[sub]) for sub in subs]
    gc_t = [gc[sub].T for sub in subs]
    e_gc = [jnp.exp(gc[sub]) for sub in subs]
    e_rest = [jnp.exp(gl[sub] - gc[sub]) for sub in subs]
    e_all = [jnp.exp(gl[sub]) for sub in subs]
    qb = [hcols(q_ref, sub, h) for sub, h in pairs]
    kb16 = [hcols(k_ref, sub, h) for sub, h in pairs]
    k = [a.astype(F32) for a in kb16]
    beta = [col(gb[sub], GDN_HEADS + h) for sub, h in pairs]
    npair = range(len(pairs))
    kbeta = [k[i] * beta[i] for i in npair]
    kk = [_dot_nt(kbeta[i].astype(BF16), kb16[i]) for i in npair]
    qk = [_dot_nt(qb[i], kb16[i]) for i in npair]
    dec = [jnp.where(incl, jnp.exp(jnp.where(incl, col(gc[sub], h) - gc_t[sub][h:h + 1, :], 0.0)), 0.0)
           for sub, h in pairs]
    low = [jnp.where(strict, kk[i] * dec[i], 0.0) for i in npair]
    r = [-jnp.where(pair_masks[0], low[i], 0.0) for i in npair]
    for mask in pair_masks[1:]:
        dmat = [(r[i] + eye).astype(BF16) for i in npair]
        cd = [_dot(jnp.where(mask, low[i], 0.0).astype(BF16), dmat[i]) for i in npair]
        r = [r[i] - _dot(dmat[i], cd[i].astype(BF16)) for i in npair]
    rhs = [jnp.concatenate([hcols(v_ref, sub, h).astype(F32) * beta[i], kbeta[i] * col(e_gc[sub], h)], axis=1)
           for i, (sub, h) in enumerate(pairs)]
    uw16 = [(rhs[i] + _dot(r[i].astype(BF16), rhs[i].astype(BF16))).astype(BF16) for i in npair]
    kt = [_dot_tn((k[i] * col(e_rest[sub], h)).astype(BF16), uw16[i])
          for i, (sub, h) in enumerate(pairs)]
    qu = [_dot((qk[i] * dec[i]).astype(BF16), uw16[i]) for i in npair]
    qeff = [(qb[i].astype(F32) * col(e_gc[sub], h) - qu[i][:, dk:]).astype(BF16)
            for i, (sub, h) in enumerate(pairs)]
    st = [s_scr[h] for h in heads]
    for sub in subs:
        st16 = [a.astype(BF16) for a in st]
        for h in heads:
            i = sub * GDN_HEADS + h
            o_ref[rows[sub], h * dk:(h + 1) * dk] = _dot(qeff[i], st16[h]) + qu[i][:, :dk]
        st = [st[h] * e_all[sub][0:1, h:h + 1] + kt[sub * GDN_HEADS + h][:, :dk]
              - _dot(kt[sub * GDN_HEADS + h][:, dk:].astype(BF16), st16[h]) for h in heads]
    for h in heads:
        s_scr[h] = st[h]


GDN_CHUNKS_PER_STEP = 2


def _gdn_scan(qkv_all, gb_all, nct_tokens):
    bsz, ttot, _ = qkv_all.shape
    c = GDN_CHUNK * GDN_CHUNKS_PER_STEP
    assert nct_tokens % c == 0 and ttot % c == 0
    nct = nct_tokens // c
    ns = ttot // c
    nlat = ns - nct
    hv = GDN_HEADS * GDN_DK

    def tmap(d, s):
        rev = jnp.where(s < nct, nct - 1 - s, 2 * nct + nlat - 1 - s)
        return jnp.where(d == 0, s, rev)

    def omap(d, s):
        first = jnp.where(d == 0, 0, nlat - 1)
        return jnp.where(s < nct, first, tmap(d, s) - nct)

    return pl.pallas_call(
        _gdn_scan_kernel,
        grid=(bsz, 2, ns),
        in_specs=[pl.BlockSpec((None, c, hv), lambda b, d, s: (b, tmap(d, s), 0)),
                  pl.BlockSpec((None, c, hv), lambda b, d, s: (b, tmap(d, s), 1)),
                  pl.BlockSpec((None, c, hv), lambda b, d, s: (b, tmap(d, s), 2)),
                  pl.BlockSpec((None, None, c, LANE), lambda b, d, s: (d, b, tmap(d, s), 0))],
        out_specs=pl.BlockSpec((None, None, c, hv), lambda b, d, s: (d, b, omap(d, s), 0)),
        out_shape=jax.ShapeDtypeStruct((2, bsz, nlat * c, hv), F32),
        scratch_shapes=[pltpu.VMEM((GDN_HEADS, GDN_DK, GDN_DK), F32)],
        compiler_params=_params(("parallel", "parallel", "arbitrary")),
    )(qkv_all, qkv_all, qkv_all, gb_all)


def _rope_tile(xr, cos, sin):
    lane = lax.broadcasted_iota(I32, xr.shape, 1)
    first_half = (lane % 32) < 16
    rot = jnp.where(first_half, -pltpu.roll(xr, LANE - 16, 1), pltpu.roll(xr, 16, 1))
    return xr * cos + rot * sin


def _mla_q_kernel(c_ref, ga_ref, w_ref, gn_ref, gr_ref, cos_ref, sin_ref, q_ref):
    cn = _rms(c_ref[...].astype(F32), ga_ref[...]).astype(BF16)
    qf = _dot(cn, w_ref[...])
    tm = qf.shape[0]
    lane = lax.broadcasted_iota(I32, (tm, LANE), 1)
    left = lane < ROPE_DIM
    rope_base = MLA_HEADS * NOPE_DIM
    scale = QK_HEAD ** -0.5 * math.log2(math.e)
    cos = cos_ref[...]
    sin = sin_ref[...]
    for hp in range(MLA_HEADS // 2):
        rt = qf[:, rope_base + hp * LANE:rope_base + (hp + 1) * LANE]
        rsq = rt * rt
        ss_left = jnp.sum(jnp.where(left, rsq, 0.0), axis=-1, keepdims=True)
        ss_right = jnp.sum(jnp.where(left, 0.0, rsq), axis=-1, keepdims=True)
        invs = []
        for par, ss_r in ((0, ss_left), (1, ss_right)):
            h = 2 * hp + par
            nope = qf[:, h * NOPE_DIM:(h + 1) * NOPE_DIM]
            ss = jnp.sum(nope * nope, axis=-1, keepdims=True) + ss_r
            inv = lax.rsqrt(ss * (1.0 / QK_HEAD) + EPS)
            invs.append(inv)
            q_ref[h, :, 0:NOPE_DIM] = (nope * inv * gn_ref[...] * scale).astype(BF16)
        inv_lane = jnp.where(left, invs[0], invs[1])
        xr = _rope_tile(rt * inv_lane * gr_ref[...], cos, sin) * scale
        q_ref[2 * hp, :, NOPE_DIM:QK_PAD] = jnp.where(left, xr, 0.0).astype(BF16)
        q_ref[2 * hp + 1, :, NOPE_DIM:QK_PAD] = jnp.where(left, pltpu.roll(xr, ROPE_DIM, 1), 0.0).astype(BF16)


def _mla_queries(p_all, ga, w, gn, gr, cos, sin, nct_tiles, seq):
    bsz = p_all.shape[0]
    tm = SEQ_TILE
    return pl.pallas_call(
        _mla_q_kernel,
        grid=(bsz, seq // tm),
        in_specs=[pl.BlockSpec((None, tm, Q_LORA), lambda b, i: (b, i + nct_tiles, P_CQ // Q_LORA)),
                  pl.BlockSpec(ga.shape, lambda b, i: (0, 0)),
                  pl.BlockSpec(w.shape, lambda b, i: (0, 0)),
                  pl.BlockSpec(gn.shape, lambda b, i: (0, 0)),
                  pl.BlockSpec(gr.shape, lambda b, i: (0, 0)),
                  pl.BlockSpec((tm, LANE), lambda b, i: (i + nct_tiles, 0)),
                  pl.BlockSpec((tm, LANE), lambda b, i: (i + nct_tiles, 0))],
        out_specs=pl.BlockSpec((None, MLA_HEADS, tm, QK_PAD), lambda b, i: (b, 0, i, 0)),
        out_shape=jax.ShapeDtypeStruct((bsz, MLA_HEADS, seq, QK_PAD), BF16),
        compiler_params=_params(("parallel", "parallel")),
    )(p_all, ga, w, gn, gr, cos, sin)


def _mla_kv_kernel(c_ref, kr_ref, ga_ref, w_ref, gn_ref, gr_ref, cos_ref, sin_ref, k_ref, v_ref):
    cn = _rms(c_ref[...].astype(F32), ga_ref[...]).astype(BF16)
    kv = _dot(cn, w_ref[...])
    kr = kr_ref[...].astype(F32)
    ss_r = jnp.sum(kr * kr, axis=-1, keepdims=True)
    kr_rot = _rope_tile(kr * gr_ref[...], cos_ref[...], sin_ref[...])
    width = NOPE_DIM + LANE
    ones_col = (lax.broadcasted_iota(I32, kr.shape, 1) == 0).astype(BF16)
    for h in range(MLA_HEADS):
        nope = kv[:, h * width:h * width + NOPE_DIM]
        ss = jnp.sum(nope * nope, axis=-1, keepdims=True) + ss_r
        inv = lax.rsqrt(ss * (1.0 / QK_HEAD) + EPS)
        k_ref[h, :, 0:NOPE_DIM] = (nope * inv * gn_ref[...]).astype(BF16)
        k_ref[h, :, NOPE_DIM:QK_PAD] = (kr_rot * inv).astype(BF16)
        v_ref[h, :, 0:LANE] = kv[:, h * width + NOPE_DIM:(h + 1) * width].astype(BF16)
        v_ref[h, :, LANE:2 * LANE] = ones_col


def _mla_keys_values(p_all, ga, w, gn, gr, cos, sin):
    bsz, ttot, _ = p_all.shape
    tm = SEQ_TILE
    return pl.pallas_call(
        _mla_kv_kernel,
        grid=(bsz, ttot // tm),
        in_specs=[pl.BlockSpec((None, tm, KV_LORA), lambda b, i: (b, i, P_CKV // KV_LORA)),
                  pl.BlockSpec((None, tm, LANE), lambda b, i: (b, i, P_KR // LANE)),
                  pl.BlockSpec(ga.shape, lambda b, i: (0, 0)),
                  pl.BlockSpec(w.shape, lambda b, i: (0, 0)),
                  pl.BlockSpec(gn.shape, lambda b, i: (0, 0)),
                  pl.BlockSpec(gr.shape, lambda b, i: (0, 0)),
                  pl.BlockSpec((tm, LANE), lambda b, i: (i, 0)),
                  pl.BlockSpec((tm, LANE), lambda b, i: (i, 0))],
        out_specs=[pl.BlockSpec((None, MLA_HEADS, tm, QK_PAD), lambda b, i: (b, 0, i, 0)),
                   pl.BlockSpec((None, MLA_HEADS, tm, 2 * LANE), lambda b, i: (b, 0, i, 0))],
        out_shape=[jax.ShapeDtypeStruct((bsz, MLA_HEADS, ttot, QK_PAD), BF16),
                   jax.ShapeDtypeStruct((bsz, MLA_HEADS, ttot, 2 * LANE), BF16)],
        compiler_params=_params(("parallel", "parallel")),
    )(p_all, p_all, ga, w, gn, gr, cos, sin)


FLASH_ROWS = 32


def _flash_kernel(tk, q_ref, k_ref, v_ref, o_ref, s_a, s_b, p_a, p_b, m_scr, a_scr, acc_scr):
    nk = k_ref.shape[0] // tk
    tq = q_ref.shape[0]
    q = q_ref[...]
    m_scr[...] = jnp.full_like(m_scr, -jnp.inf)
    acc_scr[...] = jnp.zeros_like(acc_scr)

    def keys(ref, j):
        return ref[pl.ds(pl.multiple_of(j * tk, tk), tk), :]

    def scores(j):
        return _dot_nt(q, keys(k_ref, j))

    def update(s_ref, p_ref, j):
        vj = keys(v_ref, j)
        half = tq // 2
        for part in range(2):
            for r in range(half // FLASH_ROWS):
                r0 = part * half + r * FLASH_ROWS
                rs = slice(r0, r0 + FLASH_ROWS)
                s = s_ref[rs, :]
                m_prev = m_scr[rs, :]
                m_new = jnp.maximum(m_prev, jnp.max(s, axis=-1, keepdims=True))
                m_scr[rs, :] = m_new
                a_scr[rs, :] = jnp.exp2(m_prev - m_new)
                p_ref[rs, :] = jnp.exp2(s - m_new).astype(BF16)
            hs = slice(part * half, (part + 1) * half)
            acc_scr[hs, :] = a_scr[hs, :] * acc_scr[hs, :] + _dot(p_ref[hs, :], vj)

    s_a[...] = scores(0)

    def body(i, carry):
        s_b[...] = scores(2 * i + 1)
        update(s_a, p_a, 2 * i)
        s_a[...] = scores(2 * i + 2)
        update(s_b, p_b, 2 * i + 1)
        return carry

    lax.fori_loop(0, (nk - 1) // 2, body, 0)
    if nk % 2 == 1:
        update(s_a, p_a, nk - 1)
    else:
        s_b[...] = scores(nk - 1)
        update(s_a, p_a, nk - 2)
        update(s_b, p_b, nk - 1)
    o_ref[...] = (acc_scr[:, 0:LANE] / acc_scr[:, LANE:LANE + 1]).astype(o_ref.dtype)


def _key_tile(ttot):
    for cand in (1280, 1024, 768, 512, 256, 128):
        if ttot % cand == 0:
            return cand
    raise ValueError("key length must be a multiple of 128")


def _attention(q, k, v):
    bsz, heads, seq, _ = q.shape
    ttot = k.shape[2]
    tq = 1024
    tk = _key_tile(ttot)
    return pl.pallas_call(
        functools.partial(_flash_kernel, tk),
        grid=(bsz, heads, seq // tq),
        in_specs=[pl.BlockSpec((None, None, tq, QK_PAD), lambda b, h, i: (b, h, i, 0)),
                  pl.BlockSpec((None, None, ttot, QK_PAD), lambda b, h, i: (b, h, 0, 0),
                               pipeline_mode=pl.Buffered(1)),
                  pl.BlockSpec((None, None, ttot, 2 * LANE), lambda b, h, i: (b, h, 0, 0),
                               pipeline_mode=pl.Buffered(1))],
        out_specs=pl.BlockSpec((None, tq, LANE), lambda b, h, i: (b, i, h)),
        out_shape=jax.ShapeDtypeStruct((bsz, seq, heads * LANE), BF16),
        scratch_shapes=[pltpu.VMEM((tq, tk), F32), pltpu.VMEM((tq, tk), F32),
                        pltpu.VMEM((tq, tk), BF16), pltpu.VMEM((tq, tk), BF16),
                        pltpu.VMEM((tq, 1), F32), pltpu.VMEM((tq, 1), F32),
                        pltpu.VMEM((tq, 2 * LANE), F32)],
        compiler_params=_params(("parallel", "parallel", "arbitrary")),
    )(q, k, v)


def _pack_halves(x):
    w = x.shape[1] // 2
    bits = lax.bitcast_convert_type(x.astype(BF16).astype(F32), U32)
    return jnp.bitwise_or(jnp.right_shift(bits[:, :w], jnp.uint32(16)), bits[:, w:])


def _unpack_halves(wd):
    lo = lax.bitcast_convert_type(jnp.left_shift(wd, jnp.uint32(16)), F32)
    hi = lax.bitcast_convert_type(jnp.bitwise_and(wd, jnp.uint32(0xFFFF0000)), F32)
    return lo, hi


def _ffn_front(xn, gf_ref, sh_ref, sc_ref, wrh_ref, wrl_ref, hp_ref, lg_ref):
    h2 = _rms(xn, gf_ref[...]) * (1.0 + sc_ref[...]) + sh_ref[...]
    hh = h2.astype(BF16)
    hl = (h2 - hh.astype(F32)).astype(BF16)
    hp_ref[...] = _pack_halves(h2)
    lg_ref[...] = _dot(hh, wrh_ref[...]) + _dot(hl, wrh_ref[...]) + _dot(hh, wrl_ref[...])


def _mix_out_kernel(of_ref, ob_ref, z_ref, ym_ref, x_ref, gog_ref, wout_ref, g1_ref,
                    gf_ref, sh_ref, sc_ref, wrh_ref, wrl_ref, xo_ref, hp_ref, lg_ref, mix):
    dv = GDN_DK
    o = of_ref[...] + ob_ref[...]
    for h in range(GDN_HEADS):
        hs = slice(h * dv, (h + 1) * dv)
        y = _rms(o[:, hs], gog_ref[...])
        mix[:, hs] = (y * _silu(z_ref[:, hs].astype(F32))).astype(BF16)
    hv = GDN_HEADS * dv
    mix[:, hv:] = ym_ref[...]
    xn = x_ref[...] + g1_ref[...] * _dot(mix[...], wout_ref[...])
    xo_ref[...] = xn
    _ffn_front(xn, gf_ref, sh_ref, sc_ref, wrh_ref, wrl_ref, hp_ref, lg_ref)


def _mix_out(o2, p_all, ymla, x, gog, wout, g1, gf, sh2, sc2, wrh, wrl, nct_tiles):
    bsz, seq, d = x.shape
    tm = SEQ_TILE
    hv = GDN_HEADS * GDN_DK
    vec = lambda: pl.BlockSpec((None, 1, d), lambda b, i: (b, 0, 0))
    full = lambda a: pl.BlockSpec(a.shape, lambda b, i: (0,) * a.ndim)
    return pl.pallas_call(
        _mix_out_kernel,
        grid=(bsz, seq // tm),
        in_specs=[pl.BlockSpec((None, None, tm, hv), lambda b, i: (0, b, i, 0)),
                  pl.BlockSpec((None, None, tm, hv), lambda b, i: (1, b, i, 0)),
                  pl.BlockSpec((None, tm, hv), lambda b, i: (b, i + nct_tiles, P_Z // hv)),
                  pl.BlockSpec((None, tm, hv), lambda b, i: (b, i, 0)),
                  pl.BlockSpec((None, tm, d), lambda b, i: (b, i, 0)),
                  full(gog), full(wout), vec(), full(gf), vec(), vec(), full(wrh), full(wrl)],
        out_specs=[pl.BlockSpec((None, tm, d), lambda b, i: (b, i, 0)),
                   pl.BlockSpec((None, tm, d // 2), lambda b, i: (b, i, 0)),
                   pl.BlockSpec((None, tm, LANE), lambda b, i: (b, i, 0))],
        out_shape=[jax.ShapeDtypeStruct((bsz, seq, d), F32),
                   jax.ShapeDtypeStruct((bsz, seq, d // 2), U32),
                   jax.ShapeDtypeStruct((bsz, seq, LANE), F32)],
        scratch_shapes=[pltpu.VMEM((tm, wout.shape[0]), BF16)],
        compiler_params=_params(("parallel", "parallel")),
    )(o2, o2, p_all, ymla, x, gog, wout, g1, gf, sh2, sc2, wrh, wrl)


POOL_HALO = 8


def _pool_kernel(seq, xm_ref, xp_ref, xn_ref, gm_ref, sh1_ref, sc1_ref, wp_ref, ps_ref, g1_ref,
                 gf_ref, sh_ref, sc_ref, wrh_ref, wrl_ref, xo_ref, hp_ref, lg_ref, ext):
    i = pl.program_id(1)
    n = pl.num_programs(1)
    tm = xm_ref.shape[0]
    hl = POOL_HALO

    def normed(ref):
        return _rms(ref[...], gm_ref[...]) * (1.0 + sc1_ref[...]) + sh1_ref[...]

    ext[hl:hl + tm, :] = normed(xm_ref)
    ext[0:hl, :] = jnp.where(i == 0, 0.0, normed(xp_ref))
    ext[hl + tm:2 * hl + tm, :] = jnp.where(i == n - 1, 0.0, normed(xn_ref))
    t = i * tm + lax.broadcasted_iota(I32, (tm, 1), 0)
    gw = xm_ref.shape[1] // len(POOL_WINDOWS)
    for gi, win in enumerate(POOL_WINDOWS):
        cs = slice(gi * gw, (gi + 1) * gw)
        half = win // 2
        acc = None
        for off in range(-half, win - half):
            term = ext[hl + off:hl + off + tm, cs]
            acc = term if acc is None else acc + term
        lo = jnp.clip(t - half, 0, seq)
        hi = jnp.clip(t - half + win, 0, seq)
        pooled = acc / (hi - lo).astype(F32) - ext[hl:hl + tm, cs]
        y = _dot(pooled.astype(BF16), wp_ref[gi]) * ps_ref[:, cs]
        xo_ref[:, cs] = xm_ref[:, cs] + g1_ref[:, cs] * y
    _ffn_front(xo_ref[...], gf_ref, sh_ref, sc_ref, wrh_ref, wrl_ref, hp_ref, lg_ref)


def _pool_mixer(x, gm, sh1, sc1, wp, ps, g1, gf, sh2, sc2, wrh, wrl):
    bsz, seq, d = x.shape
    tm = SEQ_TILE
    hl = POOL_HALO
    r = tm // hl
    nh = seq // hl
    vec = lambda: pl.BlockSpec((None, 1, d), lambda b, i: (b, 0, 0))
    full = lambda a: pl.BlockSpec(a.shape, lambda b, i: (0,) * a.ndim)
    return pl.pallas_call(
        functools.partial(_pool_kernel, seq),
        grid=(bsz, seq // tm),
        in_specs=[pl.BlockSpec((None, tm, d), lambda b, i: (b, i, 0)),
                  pl.BlockSpec((None, hl, d), lambda b, i: (b, jnp.maximum(i * r - 1, 0), 0)),
                  pl.BlockSpec((None, hl, d), lambda b, i: (b, jnp.minimum((i + 1) * r, nh - 1), 0)),
                  full(gm), vec(), vec(), full(wp), full(ps), vec(),
                  full(gf), vec(), vec(), full(wrh), full(wrl)],
        out_specs=[pl.BlockSpec((None, tm, d), lambda b, i: (b, i, 0)),
                   pl.BlockSpec((None, tm, d // 2), lambda b, i: (b, i, 0)),
                   pl.BlockSpec((None, tm, LANE), lambda b, i: (b, i, 0))],
        out_shape=[jax.ShapeDtypeStruct((bsz, seq, d), F32),
                   jax.ShapeDtypeStruct((bsz, seq, d // 2), U32),
                   jax.ShapeDtypeStruct((bsz, seq, LANE), F32)],
        scratch_shapes=[pltpu.VMEM((tm + 2 * hl, d), F32)],
        compiler_params=_params(("parallel", "parallel")),
    )(x, x, x, gm, sh1, sc1, wp, ps, g1, gf, sh2, sc2, wrh, wrl)


def _first_max(vals, idx, sentinel):
    m = jnp.max(vals, axis=0, keepdims=True)
    first = jnp.min(jnp.where(vals == m, idx, sentinel), axis=0, keepdims=True)
    return m, first


def _route_kernel(rows, lg_ref, bias_ref, dest_ref, gates_ref, cnt_ref, carry):
    phase = pl.program_id(0)
    step = pl.program_id(1)

    @pl.when(jnp.logical_and(phase == 0, step == 0))
    def _():
        carry[...] = jnp.zeros_like(carry)

    @pl.when(jnp.logical_and(phase == 1, step == 0))
    def _():
        shift = int(math.log2(rows))
        counts_row = carry[...].T.astype(I32)
        padded = jnp.left_shift(jnp.right_shift(counts_row + (rows - 1), shift), shift).astype(F32)
        i_exp = lax.broadcasted_iota(I32, padded.shape, 0)
        j_exp = lax.broadcasted_iota(I32, padded.shape, 1)
        first_row = jnp.sum(jnp.where(j_exp < i_exp, padded, 0.0), axis=1, keepdims=True)
        carry[...] = jnp.broadcast_to(first_row, carry.shape)

    tm = lg_ref.shape[0]
    epg = EXPERTS_PER_GROUP
    scores = jax.nn.sigmoid(lg_ref[...].T[0:N_EXPERTS, :])
    biased = scores + bias_ref[:, 0:1]
    eidx = lax.broadcasted_iota(I32, (epg, tm), 0)
    best_score = best_grp = best_i1 = best_i2 = None
    for g in range(N_GROUPS):
        blk = biased[g * epg:(g + 1) * epg, :]
        m1, i1 = _first_max(blk, eidx, epg)
        m2, i2 = _first_max(jnp.where(eidx == i1, -jnp.inf, blk), eidx, epg)
        gs = m1 + m2
        if g == 0:
            best_score, best_grp, best_i1, best_i2 = gs, jnp.zeros_like(i1), i1, i2
        else:
            better = gs > best_score
            best_score = jnp.where(better, gs, best_score)
            best_grp = jnp.where(better, g, best_grp)
            best_i1 = jnp.where(better, i1, best_i1)
            best_i2 = jnp.where(better, i2, best_i2)
    e1 = best_grp * epg + best_i1
    e2 = best_grp * epg + best_i2
    eall = lax.broadcasted_iota(I32, (LANE, tm), 0)
    oh1 = eall == e1
    oh2 = eall == e2
    oh = jnp.logical_or(oh1, oh2).astype(BF16)
    picked = jnp.sum(oh.astype(F32), axis=1, keepdims=True)

    @pl.when(phase == 0)
    def _():
        carry[...] = carry[...] + picked
        cnt_ref[...] = carry[...]

    @pl.when(phase == 1)
    def _():
        s1 = jnp.sum(jnp.where(oh1[0:N_EXPERTS], scores, 0.0), axis=0, keepdims=True)
        s2 = jnp.sum(jnp.where(oh2[0:N_EXPERTS], scores, 0.0), axis=0, keepdims=True)
        denom = s1 + s2
        earlier = (lax.broadcasted_iota(I32, (tm, tm), 0) < lax.broadcasted_iota(I32, (tm, tm), 1)).astype(BF16)
        row = _dot(oh, earlier) + carry[:, 0:1]
        d1 = jnp.sum(jnp.where(oh1, row, 0.0), axis=0, keepdims=True)
        d2 = jnp.sum(jnp.where(oh2, row, 0.0), axis=0, keepdims=True)
        carry[...] = carry[...] + picked
        dest_ref[...] = jnp.concatenate([d1.astype(I32), d2.astype(I32), jnp.zeros((6, tm), I32)], axis=0)
        gates = jnp.concatenate([s1 / denom, s2 / denom, jnp.zeros((LANE - 2, tm), F32)], axis=0)
        gates_ref[...] = gates.T


def _route(logits, bias_col):
    n = logits.shape[0]
    tm = ROUTE_TILE
    rows = MOE_ROWS
    assert rows & (rows - 1) == 0
    return pl.pallas_call(
        functools.partial(_route_kernel, rows),
        grid=(2, n // tm),
        in_specs=[pl.BlockSpec((tm, LANE), lambda p, t: (t, 0)),
                  pl.BlockSpec(bias_col.shape, lambda p, t: (0, 0))],
        out_specs=[pl.BlockSpec((8, tm), lambda p, t: (0, p * t)),
                   pl.BlockSpec((tm, LANE), lambda p, t: (p * t, 0)),
                   pl.BlockSpec((LANE, LANE), lambda p, t: (0, 0))],
        out_shape=[jax.ShapeDtypeStruct((8, n), I32),
                   jax.ShapeDtypeStruct((n, LANE), F32),
                   jax.ShapeDtypeStruct((LANE, LANE), F32)],
        scratch_shapes=[pltpu.VMEM((LANE, LANE), F32)],
        compiler_params=_params(("arbitrary", "arbitrary")),
    )(logits, bias_col)


def _dispatch_copy(hp_ref, xs_hbm, sem, j, dst):
    return pltpu.make_async_copy(hp_ref.at[pl.ds(j, 1), :], xs_hbm.at[pl.ds(dst, 1), :], sem)


def _dispatch_kernel(tm, d0_ref, d1_ref, hp_ref, xs_in, xs_hbm, sem):
    del xs_in

    def issue(j, carry):
        _dispatch_copy(hp_ref, xs_hbm, sem, j, d0_ref[0, j]).start()
        _dispatch_copy(hp_ref, xs_hbm, sem, j, d1_ref[0, j]).start()
        return carry

    lax.fori_loop(0, tm, issue, 0, unroll=DMA_UNROLL)
    for _ in range(2):
        pltpu.make_async_copy(hp_ref, xs_hbm.at[pl.ds(0, tm), :], sem).wait()


def _dispatch(hp, dest, n_rows):
    n, half = hp.shape
    tm = ROUTE_TILE
    nt = n // tm
    idx = lambda k: pl.BlockSpec((None, None, 1, tm), lambda i: (k, i, 0, 0), memory_space=pltpu.SMEM)
    dest4 = dest.reshape(dest.shape[0], nt, 1, tm)
    return pl.pallas_call(
        functools.partial(_dispatch_kernel, tm),
        grid=(nt,),
        in_specs=[idx(0), idx(1),
                  pl.BlockSpec((tm, half), lambda i: (i, 0)),
                  pl.BlockSpec(memory_space=pl.ANY)],
        out_specs=pl.BlockSpec(memory_space=pl.ANY),
        out_shape=jax.ShapeDtypeStruct((n_rows, half), U32),
        input_output_aliases={3: 0},
        scratch_shapes=[pltpu.SemaphoreType.DMA(())],
        compiler_params=_params(("arbitrary",)),
    )(dest4, dest4, hp, jnp.zeros((n_rows, half), U32))


def _experts_kernel(be_ref, nu_ref, xs_ref, wgu_ref, wd_ref, y_ref, wgu16, wd16):
    i = pl.program_id(0)

    @pl.when(i < nu_ref[0])
    def _():
        @pl.when(jnp.logical_or(i == 0, be_ref[i] != be_ref[jnp.maximum(i - 1, 0)]))
        def _():
            wgu16[...] = wgu_ref[...].astype(BF16)
            wd16[...] = wd_ref[...].astype(BF16)

        lo, hi = _unpack_halves(xs_ref[...])
        half = lo.shape[1]
        gu = _dot(lo.astype(BF16), wgu16[0:half, :]) + _dot(hi.astype(BF16), wgu16[half:, :])
        f = gu.shape[1] // 2
        hid = _silu(gu[:, :f]) * gu[:, f:]
        y_ref[...] = _pack_halves(_dot(hid.astype(BF16), wd16[...]))

    @pl.when(i >= nu_ref[0])
    def _():
        y_ref[...] = jnp.zeros_like(y_ref)


def _experts(layer, block_expert, n_used, xs, w_gate_up, w_down):
    n_rows, half = xs.shape
    rows = MOE_ROWS
    d = 2 * half
    f2 = w_gate_up.shape[3]
    grid_spec = pltpu.PrefetchScalarGridSpec(
        num_scalar_prefetch=2,
        grid=(n_rows // rows,),
        in_specs=[pl.BlockSpec((rows, half), lambda i, be, nu: (i, 0)),
                  pl.BlockSpec((None, None, d, f2), lambda i, be, nu: (layer, be[i], 0, 0)),
                  pl.BlockSpec((None, None, f2 // 2, d), lambda i, be, nu: (layer, be[i], 0, 0))],
        out_specs=pl.BlockSpec((rows, half), lambda i, be, nu: (i, 0)),
        scratch_shapes=[pltpu.VMEM((d, f2), BF16), pltpu.VMEM((f2 // 2, d), BF16)],
    )
    return pl.pallas_call(
        _experts_kernel,
        grid_spec=grid_spec,
        out_shape=jax.ShapeDtypeStruct((n_rows, half), U32),
        compiler_params=_params(("arbitrary",)),
    )(block_expert, n_used, xs, w_gate_up, w_down)


def _combine_copy(y_hbm, buf, sem, slot, k, src, j):
    return pltpu.make_async_copy(y_hbm.at[pl.ds(src, 1), :], buf.at[slot, k, pl.ds(j, 1), :], sem.at[slot])


def _combine_kernel(tm, d0_ref, d1_ref, d0n_ref, d1n_ref, y_hbm, gates_ref, x_ref, g2_ref, xo_ref, buf, sem):
    step = pl.program_id(0) * pl.num_programs(1) + pl.program_id(1)
    total = pl.num_programs(0) * pl.num_programs(1)
    slot = step % 2

    def start_gather(r0_ref, r1_ref, s):
        def issue(j, carry):
            _combine_copy(y_hbm, buf, sem, s, 0, r0_ref[0, j], j).start()
            _combine_copy(y_hbm, buf, sem, s, 1, r1_ref[0, j], j).start()
            return carry
        lax.fori_loop(0, tm, issue, 0, unroll=DMA_UNROLL)

    @pl.when(step == 0)
    def _():
        start_gather(d0_ref, d1_ref, 0)

    @pl.when(step + 1 < total)
    def _():
        start_gather(d0n_ref, d1n_ref, 1 - slot)

    for k in range(2):
        pltpu.make_async_copy(y_hbm.at[pl.ds(0, tm), :], buf.at[slot, k], sem.at[slot]).wait()
    lo0, hi0 = _unpack_halves(buf[slot, 0])
    lo1, hi1 = _unpack_halves(buf[slot, 1])
    half = lo0.shape[1]
    ga = gates_ref[:, 0:1]
    gb = gates_ref[:, 1:2]
    xo_ref[:, 0:half] = x_ref[:, 0:half] + g2_ref[:, 0:half] * (ga * lo0 + gb * lo1)
    xo_ref[:, half:] = x_ref[:, half:] + g2_ref[:, half:] * (ga * hi0 + gb * hi1)


def _combine(y, dest, gates, x, g2):
    bsz, seq, d = x.shape
    tm = SEQ_TILE
    nt = seq // tm
    half = d // 2
    last = bsz * nt - 1
    idx = lambda k, ahead: pl.BlockSpec(
        (None, None, 1, tm), lambda b, i: (k, jnp.minimum(b * nt + i + ahead, last), 0, 0),
        memory_space=pltpu.SMEM)
    dest4 = dest.reshape(dest.shape[0], bsz * nt, 1, tm)
    return pl.pallas_call(
        functools.partial(_combine_kernel, tm),
        grid=(bsz, nt),
        in_specs=[idx(0, 0), idx(1, 0), idx(0, 1), idx(1, 1),
                  pl.BlockSpec(memory_space=pl.ANY),
                  pl.BlockSpec((tm, LANE), lambda b, i: (b * nt + i, 0)),
                  pl.BlockSpec((None, tm, d), lambda b, i: (b, i, 0)),
                  pl.BlockSpec((None, 1, d), lambda b, i: (b, 0, 0))],
        out_specs=pl.BlockSpec((None, tm, d), lambda b, i: (b, i, 0)),
        out_shape=jax.ShapeDtypeStruct((bsz, seq, d), F32),
        scratch_shapes=[pltpu.VMEM((2, 2, tm, half), U32), pltpu.SemaphoreType.DMA((2,))],
        compiler_params=_params(("arbitrary", "arbitrary")),
    )(dest4, dest4, dest4, dest4, y, gates, x, g2)


def _moe(layer, x, hp, logits, g2, bias_col, w_gate_up, w_down):
    bsz, seq, d = x.shape
    n = bsz * seq
    rows = MOE_ROWS
    dest, gates, cnt = _route(logits.reshape(n, LANE), bias_col)
    counts = cnt[:N_EXPERTS, 0].astype(I32)
    pad_end = jnp.cumsum((counts + rows - 1) // rows * rows)
    n_rows = -(-(n * TOP_K + N_EXPERTS * (rows - 1)) // rows) * rows
    block_start = jnp.arange(n_rows // rows, dtype=I32) * rows
    block_expert = jnp.minimum(jnp.sum((pad_end[None, :] <= block_start[:, None]).astype(I32), axis=1),
                               N_EXPERTS - 1)
    n_used = pad_end[-1:] // rows
    xs = _dispatch(hp.reshape(n, d // 2), dest, n_rows)
    y = _experts(layer, block_expert, n_used, xs, w_gate_up, w_down)
    return _combine(y, dest, gates, x, g2)


def _permute_w_in(w_in):
    d = w_in.shape[0]
    hq = GDN_HEADS * GDN_DK
    off_a = 4 * hq
    off_cq = off_a + 4 * GDN_HEADS
    off_ckv = off_cq + Q_LORA
    off_kr = off_ckv + KV_LORA
    z = lambda n: jnp.zeros((d, n), w_in.dtype)
    a = lambda k: w_in[:, off_a + k * GDN_HEADS:off_a + (k + 1) * GDN_HEADS]
    pad_ab = LANE - 2 * GDN_HEADS
    cols = [w_in[:, :off_a], w_in[:, off_cq:off_kr + ROPE_DIM], z(LANE - ROPE_DIM),
            a(0), a(2), z(pad_ab), a(1), a(3), z(pad_ab)]
    w = jnp.concatenate(cols, axis=1)
    return jnp.concatenate([w, z(P_WIDTH - w.shape[1])], axis=1).astype(BF16)


def _rope_tables(rows, n_ctx):
    row = jnp.repeat(jnp.arange(rows, dtype=F32), GRID_W)
    col = jnp.tile(jnp.arange(GRID_W, dtype=F32), rows)
    pairs = ROPE_DIM // 4
    inv_freq = ROPE_THETA ** (-jnp.arange(pairs, dtype=F32) / pairs)
    ang_r = row[:, None] * inv_freq
    ang_c = col[:, None] * inv_freq
    ang = jnp.concatenate([ang_r, ang_r, ang_c, ang_c], axis=-1)
    ang = jnp.concatenate([jnp.zeros((n_ctx, ROPE_DIM), F32), ang], axis=0)
    cos, sin = jnp.cos(ang), jnp.sin(ang)
    return jnp.concatenate([cos, cos], axis=1), jnp.concatenate([sin, sin], axis=1)


def kernel(x, c, ctx, c_ctx, w_mod, b_mod, norm_mix_g, norm_ffn_g, w_in, conv_qkv, a_log_fwd, a_log_bwd, dt_bias_fwd, dt_bias_bwd, gdn_out_g, q_a_norm_g, w_uq, kv_a_norm_g, w_ukv, q_norm_g, k_norm_g, w_out, w_pool, pool_scale, w_router, router_bias, w_gate_up, w_down):
    bsz, seq, d = x.shape
    n_ctx = ctx.shape[1]
    depth = w_mod.shape[0]
    assert depth == 2, "the context stream is only read: no layer after the first even layer reads it"
    nct_tiles = n_ctx // SEQ_TILE

    cc = jnp.concatenate([c, c_ctx[None, :], jnp.zeros((8 - bsz - 1, d), F32)], axis=0)
    mod = _modulation(cc, w_mod, b_mod).reshape(depth, 8, 6, d)

    def mods(layer):
        m = mod[layer]
        return [m[:bsz, k][:, None, :] for k in range(6)], [m[bsz:bsz + 1, k] for k in range(6)]

    wr = jnp.pad(w_router, ((0, 0), (0, LANE - N_EXPERTS)))
    wrh = wr.astype(BF16)
    wrl = (wr - wrh.astype(F32)).astype(BF16)
    bias_col = jnp.broadcast_to(router_bias.astype(F32)[:, None], (N_EXPERTS, LANE))
    row = lambda v: v.astype(F32)[None, :]

    for layer in range(depth):
        j = layer // 2
        (sh1, sc1, g1, sh2, sc2, g2), (csh1, csc1, _, _, _, _) = mods(layer)
        gm = row(norm_mix_g[layer])
        gf = row(norm_ffn_g[layer])
        if layer % 2 == 0:
            p_all = _in_projection(ctx, x, gm, csh1, csc1, sh1, sc1, _permute_w_in(w_in[j]))
            conv_w = conv_qkv[j].astype(F32)
            gate_params = jnp.zeros((2, 8, LANE), F32)
            gate_params = gate_params.at[0, 0, :GDN_HEADS].set(a_log_fwd[j]).at[0, 1, :GDN_HEADS].set(dt_bias_fwd[j])
            gate_params = gate_params.at[1, 0, :GDN_HEADS].set(a_log_bwd[j]).at[1, 1, :GDN_HEADS].set(dt_bias_bwd[j])
            qkv_all, gb_all = _gdn_front(p_all, conv_w, gate_params, nct_tiles)
            o2 = _gdn_scan(qkv_all, gb_all, n_ctx)

            cos, sin = _rope_tables(seq // GRID_W, n_ctx)
            gq = q_norm_g[j].astype(F32)
            gk = k_norm_g[j].astype(F32)
            wq = w_uq[j].reshape(Q_LORA, MLA_HEADS, QK_HEAD)
            wq = jnp.concatenate([wq[:, :, :NOPE_DIM].reshape(Q_LORA, -1),
                                  wq[:, :, NOPE_DIM:].reshape(Q_LORA, -1)], axis=1).astype(BF16)
            q = _mla_queries(p_all, row(q_a_norm_g[j]), wq, gq[None, :NOPE_DIM],
                             jnp.tile(gq[NOPE_DIM:], 2)[None, :], cos, sin, nct_tiles, seq)
            gk_rope = jnp.concatenate([gk[NOPE_DIM:], jnp.zeros((LANE - ROPE_DIM,), F32)])[None, :]
            k_all, v_all = _mla_keys_values(p_all, row(kv_a_norm_g[j]), w_ukv[j].astype(BF16),
                                            gk[None, :NOPE_DIM], gk_rope, cos, sin)
            ymla = _attention(q, k_all, v_all)
            x, hp, logits = _mix_out(o2, p_all, ymla, x, row(gdn_out_g[j]), w_out[j].astype(BF16), g1,
                                     gf, sh2, sc2, wrh, wrl, nct_tiles)
        else:
            x, hp, logits = _pool_mixer(x, gm, sh1, sc1, w_pool[j].astype(BF16), row(pool_scale[j]), g1,
                                        gf, sh2, sc2, wrh, wrl)
        x = _moe(layer, x, hp, logits, g2, bias_col, w_gate_up, w_down)
    return x
```

```python
import functools
import math

import jax
import jax.numpy as jnp
from jax import lax
from jax.experimental import pallas as pl
from jax.experimental.pallas import tpu as pltpu

F32 = jnp.float32
BF16 = jnp.bfloat16
U32 = jnp.uint32
I32 = jnp.int32

EPS = 1e-6
GRID_W = 64
GDN_HEADS = 8
GDN_DK = 128
MLA_HEADS = 8
NOPE_DIM = 128
ROPE_DIM = 64
QK_HEAD = NOPE_DIM + ROPE_DIM
QK_PAD = 256
ROPE_THETA = 10000.0
Q_LORA = 512
KV_LORA = 256
POOL_WINDOWS = (2, 4, 8, 16)
N_EXPERTS = 64
EXPERTS_PER_GROUP = 8
N_GROUPS = N_EXPERTS // EXPERTS_PER_GROUP
TOP_K = 2

LANE = 128
V7X_VMEM_LIMIT = 56 * 1024 * 1024

SEQ_TILE = 256
GDN_CHUNK = 128
MOE_ROWS = 256
DMA_UNROLL = 8
ROUTE_TILE = 512

P_QKV = 0
P_Z = 3072
P_CQ = 4096
P_CKV = 4608
P_KR = 4864
P_AB = 4992
P_WIDTH = 5376
P_NTILE = 1792


def _params(sem):
    return pltpu.CompilerParams(dimension_semantics=sem, vmem_limit_bytes=V7X_VMEM_LIMIT)


def _dot(a, b):
    return jnp.dot(a, b, preferred_element_type=F32)


def _dot_nt(a, b):
    return lax.dot_general(a, b, (((1,), (1,)), ((), ())), preferred_element_type=F32)


def _dot_tn(a, b):
    return lax.dot_general(a, b, (((0,), (0,)), ((), ())), preferred_element_type=F32)


def _silu(x):
    return x * jax.nn.sigmoid(x)


def _rms(x, g):
    return x * lax.rsqrt(jnp.mean(x * x, axis=-1, keepdims=True) + EPS) * g


def _split3(x):
    hi = x.astype(BF16)
    r = x - hi.astype(F32)
    mid = r.astype(BF16)
    lo = (r - mid.astype(F32)).astype(BF16)
    return hi, mid, lo


def _mod_kernel(c_ref, w_ref, b_ref, o_ref):
    s = _silu(c_ref[...]).astype(BF16)
    o_ref[...] = _dot(s, w_ref[...].astype(BF16)) + b_ref[...]


def _modulation(cc, w_mod, b_mod):
    depth, d, n6 = w_mod.shape
    tn = 1024
    return pl.pallas_call(
        _mod_kernel,
        grid=(depth, n6 // tn),
        in_specs=[pl.BlockSpec((8, d), lambda l, j: (0, 0)),
                  pl.BlockSpec((None, d, tn), lambda l, j: (l, 0, j)),
                  pl.BlockSpec((None, 1, tn), lambda l, j: (l, 0, j))],
        out_specs=pl.BlockSpec((None, 8, tn), lambda l, j: (l, 0, j)),
        out_shape=jax.ShapeDtypeStruct((depth, 8, n6), F32),
        compiler_params=_params(("parallel", "parallel")),
    )(cc, w_mod, b_mod.reshape(depth, 1, n6))


def _inproj_kernel(nct, xc_ref, x_ref, g_ref, shc_ref, scc_ref, sh_ref, sc_ref, w_ref, o_ref):
    i = pl.program_id(2)

    def run(xr, shr, scr):
        h = _rms(xr[...], g_ref[...]) * (1.0 + scr[...]) + shr[...]
        o_ref[...] = _dot(h.astype(BF16), w_ref[...]).astype(o_ref.dtype)

    @pl.when(i < nct)
    def _():
        run(xc_ref, shc_ref, scc_ref)

    @pl.when(i >= nct)
    def _():
        run(x_ref, sh_ref, sc_ref)


def _in_projection(ctx, x, g, shc, scc, sh, sc, w):
    bsz, seq, d = x.shape
    tm = SEQ_TILE
    nct = ctx.shape[1] // tm
    nt = nct + seq // tm
    width = w.shape[1]
    tn = P_NTILE
    return pl.pallas_call(
        functools.partial(_inproj_kernel, nct),
        grid=(width // tn, bsz, nt),
        in_specs=[pl.BlockSpec((None, tm, d), lambda j, b, i: (b, jnp.minimum(i, nct - 1), 0)),
                  pl.BlockSpec((None, tm, d), lambda j, b, i: (b, jnp.maximum(i - nct, 0), 0)),
                  pl.BlockSpec((1, d), lambda j, b, i: (0, 0)),
                  pl.BlockSpec((1, d), lambda j, b, i: (0, 0)),
                  pl.BlockSpec((1, d), lambda j, b, i: (0, 0)),
                  pl.BlockSpec((None, 1, d), lambda j, b, i: (b, 0, 0)),
                  pl.BlockSpec((None, 1, d), lambda j, b, i: (b, 0, 0)),
                  pl.BlockSpec((d, tn), lambda j, b, i: (0, j))],
        out_specs=pl.BlockSpec((None, tm, tn), lambda j, b, i: (b, i, j)),
        out_shape=jax.ShapeDtypeStruct((bsz, nt * tm, width), BF16),
        compiler_params=_params(("parallel", "parallel", "parallel")),
    )(ctx, x, g, shc, scc, sh, sc, w)


GDN_HALO = 16


def _gdn_front_kernel(nct, pm_ref, pp_ref, pn_ref, cw_ref, ab0_ref, ab1_ref, gp_ref,
                      qkv_ref, gb_ref, ext):
    i = pl.program_id(1)
    n = pl.num_programs(1)
    tm = pm_ref.shape[0]
    hl = GDN_HALO
    zero_prev = jnp.logical_or(i == 0, i == nct)
    zero_next = jnp.logical_or(i == nct - 1, i == n - 1)
    ext[hl:hl + tm, :] = pm_ref[...].astype(F32)
    ext[0:hl, :] = jnp.where(zero_prev, 0.0, pp_ref[...].astype(F32))
    ext[hl + tm:2 * hl + tm, :] = jnp.where(zero_next, 0.0, pn_ref[...].astype(F32))
    taps = cw_ref.shape[0]
    nqk = 2 * GDN_HEADS
    for cb in range(qkv_ref.shape[1] // LANE):
        cs = slice(cb * LANE, (cb + 1) * LANE)
        acc = None
        for j in range(taps):
            off = hl - taps // 2 + j
            term = cw_ref[j:j + 1, cs] * ext[off:off + tm, cs]
            acc = term if acc is None else acc + term
        y = _silu(acc)
        if cb < nqk:
            inv = lax.rsqrt(jnp.sum(y * y, axis=-1, keepdims=True) + EPS)
            if cb < GDN_HEADS:
                inv = inv * (GDN_DK ** -0.5)
            y = y * inv
        qkv_ref[:, cs] = y.astype(BF16)
    for d, ab_ref in enumerate((ab0_ref, ab1_ref)):
        a = ab_ref[...].astype(F32)
        lane = lax.broadcasted_iota(I32, a.shape, 1)
        neg_decay = -jnp.exp(gp_ref[d, 0:1, :])
        xx = a + gp_ref[d, 1:2, :]
        softplus = jnp.maximum(xx, 0.0) + jnp.log1p(jnp.exp(-jnp.abs(xx)))
        gate = jnp.where(lane < GDN_HEADS, neg_decay * softplus, jax.nn.sigmoid(a))
        gb_ref[d] = jnp.where(lane < 2 * GDN_HEADS, gate, 0.0)


def _gdn_front(p_all, conv_w, gate_params, nct):
    bsz, ttot, _ = p_all.shape
    tm = SEQ_TILE
    nt = ttot // tm
    cq = P_Z
    hl = GDN_HALO
    r = tm // hl
    nh = ttot // hl
    return pl.pallas_call(
        functools.partial(_gdn_front_kernel, nct),
        grid=(bsz, nt),
        in_specs=[pl.BlockSpec((None, tm, cq), lambda b, i: (b, i, 0)),
                  pl.BlockSpec((None, hl, cq), lambda b, i: (b, jnp.maximum(i * r - 1, 0), 0)),
                  pl.BlockSpec((None, hl, cq), lambda b, i: (b, jnp.minimum((i + 1) * r, nh - 1), 0)),
                  pl.BlockSpec(conv_w.shape, lambda b, i: (0, 0)),
                  pl.BlockSpec((None, tm, LANE), lambda b, i: (b, i, P_AB // LANE)),
                  pl.BlockSpec((None, tm, LANE), lambda b, i: (b, i, P_AB // LANE + 1)),
                  pl.BlockSpec(gate_params.shape, lambda b, i: (0, 0, 0))],
        out_specs=[pl.BlockSpec((None, tm, cq), lambda b, i: (b, i, 0)),
                   pl.BlockSpec((2, None, tm, LANE), lambda b, i: (0, b, i, 0))],
        out_shape=[jax.ShapeDtypeStruct((bsz, ttot, cq), BF16),
                   jax.ShapeDtypeStruct((2, bsz, ttot, LANE), F32)],
        scratch_shapes=[pltpu.VMEM((tm + 2 * hl, cq), F32)],
        compiler_params=_params(("parallel", "parallel")),
    )(p_all, p_all, p_all, conv_w, p_all, p_all, gate_params)


def _gdn_scan_kernel(q_ref, k_ref, v_ref, gb_ref, o_ref, s_scr):
    d = pl.program_id(1)
    s = pl.program_id(2)
    c = GDN_CHUNK
    n_sub = q_ref.shape[0] // c
    dk = GDN_DK

    @pl.when(s == 0)
    def _():
        s_scr[...] = jnp.zeros_like(s_scr)

    ri = lax.broadcasted_iota(I32, (c, c), 0)
    ci = lax.broadcasted_iota(I32, (c, c), 1)
    incl = jnp.where(d == 0, ri - ci, ci - ri) >= 0
    strict = jnp.logical_and(incl, ci != ri)
    m_incl = incl.astype(BF16)
    ones = jnp.ones((c, c), BF16)
    eye = (ri == ci).astype(F32)
    pair_masks = []
    for lb in range(int(math.log2(c))):
        same_pair = jnp.right_shift(ri, lb + 1) == jnp.right_shift(ci, lb + 1)
        same_block = jnp.right_shift(ri, lb) == jnp.right_shift(ci, lb)
        pair_masks.append(jnp.logical_and(same_pair, jnp.logical_not(same_block)))
    heads = range(GDN_HEADS)
    col = lambda a, h: a[:, h:h + 1]

    rows = [pl.ds(pl.multiple_of(jnp.where(d == 0, sub, n_sub - 1 - sub) * c, c), c) for sub in range(n_sub)]
    subs = range(n_sub)
    pairs = [(sub, h) for sub in subs for h in heads]
    hcols = lambda ref, sub, h: ref[rows[sub], h * dk:(h + 1) * dk]
    gb = [gb_ref[rows[sub], :] for sub in subs]
    lane = lax.broadcasted_iota(I32, gb[0].shape, 1)
    gsplit = [_split3(jnp.where(lane < GDN_HEADS, gb[sub], 0.0)) for sub in subs]
    gc = [sum(_dot(m_incl, t) for t in gsplit[sub]) for sub in subs]
    gl = [sum(_dot(ones, t) for t in gsplit[sub]) for sub in subs]
    gc_t = [gc[sub].T for sub in subs]
    e_gc = [jnp.exp(gc[sub]) for sub in subs]
    e_rest = [jnp.exp(gl[sub] - gc[sub]) for sub in subs]
    e_all = [jnp.exp(gl[sub]) for sub in subs]
    qb = [hcols(q_ref, sub, h) for sub, h in pairs]
    kb16 = [hcols(k_ref, sub, h) for sub, h in pairs]
    k = [a.astype(F32) for a in kb16]
    beta = [col(gb[sub], GDN_HEADS + h) for sub, h in pairs]
    npair = range(len(pairs))
    kbeta = [k[i] * beta[i] for i in npair]
    kk = [_dot_nt(kbeta[i].astype(BF16), kb16[i]) for i in npair]
    qk = [_dot_nt(qb[i], kb16[i]) for i in npair]
    dec = [jnp.where(incl, jnp.exp(jnp.where(incl, col(gc[sub], h) - gc_t[sub][h:h + 1, :], 0.0)), 0.0)
           for sub, h in pairs]
    low = [jnp.where(strict, kk[i] * dec[i], 0.0) for i in npair]
    r = [-jnp.where(pair_masks[0], low[i], 0.0) for i in npair]
    for mask in pair_masks[1:]:
        dmat = [(r[i] + eye).astype(BF16) for i in npair]
        cd = [_dot(jnp.where(mask, low[i], 0.0).astype(BF16), dmat[i]) for i in npair]
        r = [r[i] - _dot(dmat[i], cd[i].astype(BF16)) for i in npair]
    rhs = [jnp.concatenate([hcols(v_ref, sub, h).astype(F32) * beta[i], kbeta[i] * col(e_gc[sub], h)], axis=1)
           for i, (sub, h) in enumerate(pairs)]
    uw16 = [(rhs[i] + _dot(r[i].astype(BF16), rhs[i].astype(BF16))).astype(BF16) for i in npair]
    kt = [_dot_tn((k[i] * col(e_rest[sub], h)).astype(BF16), uw16[i])
          for i, (sub, h) in enumerate(pairs)]
    qu = [_dot((qk[i] * dec[i]).astype(BF16), uw16[i]) for i in npair]
    qeff = [(qb[i].astype(F32) * col(e_gc[sub], h) - qu[i][:, dk:]).astype(BF16)
            for i, (sub, h) in enumerate(pairs)]
    st = [s_scr[h] for h in heads]
    for sub in subs:
        st16 = [a.astype(BF16) for a in st]
        for h in heads:
            i = sub * GDN_HEADS + h
            o_ref[rows[sub], h * dk:(h + 1) * dk] = _dot(qeff[i], st16[h]) + qu[i][:, :dk]
        st = [st[h] * e_all[sub][0:1, h:h + 1] + kt[sub * GDN_HEADS + h][:, :dk]
              - _dot(kt[sub * GDN_HEADS + h][:, dk:].astype(BF16), st16[h]) for h in heads]
    for h in heads:
        s_scr[h] = st[h]


GDN_CHUNKS_PER_STEP = 2


def _gdn_scan(qkv_all, gb_all, nct_tokens):
    bsz, ttot, _ = qkv_all.shape
    c = GDN_CHUNK * GDN_CHUNKS_PER_STEP
    assert nct_tokens % c == 0 and ttot % c == 0
    nct = nct_tokens // c
    ns = ttot // c
    nlat = ns - nct
    hv = GDN_HEADS * GDN_DK

    def tmap(d, s):
        rev = jnp.where(s < nct, nct - 1 - s, 2 * nct + nlat - 1 - s)
        return jnp.where(d == 0, s, rev)

    def omap(d, s):
        first = jnp.where(d == 0, 0, nlat - 1)
        return jnp.where(s < nct, first, tmap(d, s) - nct)

    return pl.pallas_call(
        _gdn_scan_kernel,
        grid=(bsz, 2, ns),
        in_specs=[pl.BlockSpec((None, c, hv), lambda b, d, s: (b, tmap(d, s), 0)),
                  pl.BlockSpec((None, c, hv), lambda b, d, s: (b, tmap(d, s), 1)),
                  pl.BlockSpec((None, c, hv), lambda b, d, s: (b, tmap(d, s), 2)),
                  pl.BlockSpec((None, None, c, LANE), lambda b, d, s: (d, b, tmap(d, s), 0))],
        out_specs=pl.BlockSpec((None, None, c, hv), lambda b, d, s: (d, b, omap(d, s), 0)),
        out_shape=jax.ShapeDtypeStruct((2, bsz, nlat * c, hv), F32),
        scratch_shapes=[pltpu.VMEM((GDN_HEADS, GDN_DK, GDN_DK), F32)],
        compiler_params=_params(("parallel", "parallel", "arbitrary")),
    )(qkv_all, qkv_all, qkv_all, gb_all)


def _rope_tile(xr, cos, sin):
    lane = lax.broadcasted_iota(I32, xr.shape, 1)
    first_half = (lane % 32) < 16
    rot = jnp.where(first_half, -pltpu.roll(xr, LANE - 16, 1), pltpu.roll(xr, 16, 1))
    return xr * cos + rot * sin


def _mla_q_kernel(c_ref, ga_ref, w_ref, gn_ref, gr_ref, cos_ref, sin_ref, q_ref):
    cn = _rms(c_ref[...].astype(F32), ga_ref[...]).astype(BF16)
    qf = _dot(cn, w_ref[...])
    tm = qf.shape[0]
    lane = lax.broadcasted_iota(I32, (tm, LANE), 1)
    left = lane < ROPE_DIM
    rope_base = MLA_HEADS * NOPE_DIM
    scale = QK_HEAD ** -0.5 * math.log2(math.e)
    cos = cos_ref[...]
    sin = sin_ref[...]
    for hp in range(MLA_HEADS // 2):
        rt = qf[:, rope_base + hp * LANE:rope_base + (hp + 1) * LANE]
        rsq = rt * rt
        ss_left = jnp.sum(jnp.where(left, rsq, 0.0), axis=-1, keepdims=True)
        ss_right = jnp.sum(jnp.where(left, 0.0, rsq), axis=-1, keepdims=True)
        invs = []
        for par, ss_r in ((0, ss_left), (1, ss_right)):
            h = 2 * hp + par
            nope = qf[:, h * NOPE_DIM:(h + 1) * NOPE_DIM]
            ss = jnp.sum(nope * nope, axis=-1, keepdims=True) + ss_r
            inv = lax.rsqrt(ss * (1.0 / QK_HEAD) + EPS)
            invs.append(inv)
            q_ref[h, :, 0:NOPE_DIM] = (nope * inv * gn_ref[...] * scale).astype(BF16)
        inv_lane = jnp.where(left, invs[0], invs[1])
        xr = _rope_tile(rt * inv_lane * gr_ref[...], cos, sin) * scale
        q_ref[2 * hp, :, NOPE_DIM:QK_PAD] = jnp.where(left, xr, 0.0).astype(BF16)
        q_ref[2 * hp + 1, :, NOPE_DIM:QK_PAD] = jnp.where(left, pltpu.roll(xr, ROPE_DIM, 1), 0.0).astype(BF16)


def _mla_queries(p_all, ga, w, gn, gr, cos, sin, nct_tiles, seq):
    bsz = p_all.shape[0]
    tm = SEQ_TILE
    return pl.pallas_call(
        _mla_q_kernel,
        grid=(bsz, seq // tm),
        in_specs=[pl.BlockSpec((None, tm, Q_LORA), lambda b, i: (b, i + nct_tiles, P_CQ // Q_LORA)),
                  pl.BlockSpec(ga.shape, lambda b, i: (0, 0)),
                  pl.BlockSpec(w.shape, lambda b, i: (0, 0)),
                  pl.BlockSpec(gn.shape, lambda b, i: (0, 0)),
                  pl.BlockSpec(gr.shape, lambda b, i: (0, 0)),
                  pl.BlockSpec((tm, LANE), lambda b, i: (i + nct_tiles, 0)),
                  pl.BlockSpec((tm, LANE), lambda b, i: (i + nct_tiles, 0))],
        out_specs=pl.BlockSpec((None, MLA_HEADS, tm, QK_PAD), lambda b, i: (b, 0, i, 0)),
        out_shape=jax.ShapeDtypeStruct((bsz, MLA_HEADS, seq, QK_PAD), BF16),
        compiler_params=_params(("parallel", "parallel")),
    )(p_all, ga, w, gn, gr, cos, sin)


def _mla_kv_kernel(c_ref, kr_ref, ga_ref, w_ref, gn_ref, gr_ref, cos_ref, sin_ref, k_ref, v_ref):
    cn = _rms(c_ref[...].astype(F32), ga_ref[...]).astype(BF16)
    kv = _dot(cn, w_ref[...])
    kr = kr_ref[...].astype(F32)
    ss_r = jnp.sum(kr * kr, axis=-1, keepdims=True)
    kr_rot = _rope_tile(kr * gr_ref[...], cos_ref[...], sin_ref[...])
    width = NOPE_DIM + LANE
    ones_col = (lax.broadcasted_iota(I32, kr.shape, 1) == 0).astype(BF16)
    for h in range(MLA_HEADS):
        nope = kv[:, h * width:h * width + NOPE_DIM]
        ss = jnp.sum(nope * nope, axis=-1, keepdims=True) + ss_r
        inv = lax.rsqrt(ss * (1.0 / QK_HEAD) + EPS)
        k_ref[h, :, 0:NOPE_DIM] = (nope * inv * gn_ref[...]).astype(BF16)
        k_ref[h, :, NOPE_DIM:QK_PAD] = (kr_rot * inv).astype(BF16)
        v_ref[h, :, 0:LANE] = kv[:, h * width + NOPE_DIM:(h + 1) * width].astype(BF16)
        v_ref[h, :, LANE:2 * LANE] = ones_col


def _mla_keys_values(p_all, ga, w, gn, gr, cos, sin):
    bsz, ttot, _ = p_all.shape
    tm = SEQ_TILE
    return pl.pallas_call(
        _mla_kv_kernel,
        grid=(bsz, ttot // tm),
        in_specs=[pl.BlockSpec((None, tm, KV_LORA), lambda b, i: (b, i, P_CKV // KV_LORA)),
                  pl.BlockSpec((None, tm, LANE), lambda b, i: (b, i, P_KR // LANE)),
                  pl.BlockSpec(ga.shape, lambda b, i: (0, 0)),
                  pl.BlockSpec(w.shape, lambda b, i: (0, 0)),
                  pl.BlockSpec(gn.shape, lambda b, i: (0, 0)),
                  pl.BlockSpec(gr.shape, lambda b, i: (0, 0)),
                  pl.BlockSpec((tm, LANE), lambda b, i: (i, 0)),
                  pl.BlockSpec((tm, LANE), lambda b, i: (i, 0))],
        out_specs=[pl.BlockSpec((None, MLA_HEADS, tm, QK_PAD), lambda b, i: (b, 0, i, 0)),
                   pl.BlockSpec((None, MLA_HEADS, tm, 2 * LANE), lambda b, i: (b, 0, i, 0))],
        out_shape=[jax.ShapeDtypeStruct((bsz, MLA_HEADS, ttot, QK_PAD), BF16),
                   jax.ShapeDtypeStruct((bsz, MLA_HEADS, ttot, 2 * LANE), BF16)],
        compiler_params=_params(("parallel", "parallel")),
    )(p_all, p_all, ga, w, gn, gr, cos, sin)


FLASH_PARTS = 2
FLASH_ROWS = 64


def _flash_kernel(tk, q_ref, k_ref, v_ref, o_ref, s_a, s_b, p_a, p_b, m_scr, a_scr, acc_scr):
    nk = k_ref.shape[0] // tk
    tq = q_ref.shape[0]
    q = q_ref[...]
    m_scr[...] = jnp.full_like(m_scr, -jnp.inf)
    acc_scr[...] = jnp.zeros_like(acc_scr)

    def keys(ref, j):
        return ref[pl.ds(pl.multiple_of(j * tk, tk), tk), :]

    def scores(j):
        return _dot_nt(q, keys(k_ref, j))

    def update(s_ref, p_ref, j):
        vj = keys(v_ref, j)
        half = tq // FLASH_PARTS
        for part in range(FLASH_PARTS):
            for r in range(half // FLASH_ROWS):
                r0 = part * half + r * FLASH_ROWS
                rs = slice(r0, r0 + FLASH_ROWS)
                s = s_ref[rs, :]
                m_prev = m_scr[rs, :]
                m_new = jnp.maximum(m_prev, jnp.max(s, axis=-1, keepdims=True))
                m_scr[rs, :] = m_new
                a_scr[rs, :] = jnp.exp2(m_prev - m_new)
                p_ref[rs, :] = jnp.exp2((s - m_new).astype(BF16))
            hs = slice(part * half, (part + 1) * half)
            acc_scr[hs, :] = a_scr[hs, :] * acc_scr[hs, :] + _dot(p_ref[hs, :], vj)

    s_a[...] = scores(0)

    def body(i, carry):
        s_b[...] = scores(2 * i + 1)
        update(s_a, p_a, 2 * i)
        s_a[...] = scores(2 * i + 2)
        update(s_b, p_b, 2 * i + 1)
        return carry

    lax.fori_loop(0, (nk - 1) // 2, body, 0)
    if nk % 2 == 1:
        update(s_a, p_a, nk - 1)
    else:
        s_b[...] = scores(nk - 1)
        update(s_a, p_a, nk - 2)
        update(s_b, p_b, nk - 1)
    o_ref[...] = (acc_scr[:, 0:LANE] / acc_scr[:, LANE:LANE + 1]).astype(o_ref.dtype)


def _key_tile(ttot):
    for cand in (1280, 1024, 768, 512, 256, 128):
        if ttot % cand == 0:
            return cand
    raise ValueError("key length must be a multiple of 128")


def _attention(q, k, v):
    bsz, heads, seq, _ = q.shape
    ttot = k.shape[2]
    tq = 1024
    tk = _key_tile(ttot)
    return pl.pallas_call(
        functools.partial(_flash_kernel, tk),
        grid=(bsz, heads, seq // tq),
        in_specs=[pl.BlockSpec((None, None, tq, QK_PAD), lambda b, h, i: (b, h, i, 0)),
                  pl.BlockSpec((None, None, ttot, QK_PAD), lambda b, h, i: (b, h, 0, 0),
                               pipeline_mode=pl.Buffered(1)),
                  pl.BlockSpec((None, None, ttot, 2 * LANE), lambda b, h, i: (b, h, 0, 0),
                               pipeline_mode=pl.Buffered(1))],
        out_specs=pl.BlockSpec((None, tq, LANE), lambda b, h, i: (b, i, h)),
        out_shape=jax.ShapeDtypeStruct((bsz, seq, heads * LANE), BF16),
        scratch_shapes=[pltpu.VMEM((tq, tk), F32), pltpu.VMEM((tq, tk), F32),
                        pltpu.VMEM((tq, tk), BF16), pltpu.VMEM((tq, tk), BF16),
                        pltpu.VMEM((tq, 1), F32), pltpu.VMEM((tq, 1), F32),
                        pltpu.VMEM((tq, 2 * LANE), F32)],
        compiler_params=_params(("parallel", "parallel", "arbitrary")),
    )(q, k, v)


def _pack_halves(x):
    w = x.shape[1] // 2
    bits = lax.bitcast_convert_type(x.astype(BF16).astype(F32), U32)
    return jnp.bitwise_or(jnp.right_shift(bits[:, :w], jnp.uint32(16)), bits[:, w:])


def _unpack_halves(wd):
    lo = lax.bitcast_convert_type(jnp.left_shift(wd, jnp.uint32(16)), F32)
    hi = lax.bitcast_convert_type(jnp.bitwise_and(wd, jnp.uint32(0xFFFF0000)), F32)
    return lo, hi


def _ffn_front(xn, gf_ref, sh_ref, sc_ref, wrh_ref, wrl_ref, hp_ref, lg_ref):
    h2 = _rms(xn, gf_ref[...]) * (1.0 + sc_ref[...]) + sh_ref[...]
    hh = h2.astype(BF16)
    hl = (h2 - hh.astype(F32)).astype(BF16)
    hp_ref[...] = _pack_halves(h2)
    both = _dot(hh, wrl_ref[...])
    lg_ref[...] = both[:, 0:LANE] + both[:, LANE:] + _dot(hl, wrh_ref[...])


def _mix_out_kernel(of_ref, ob_ref, z_ref, ym_ref, x_ref, gog_ref, wout_ref, g1_ref,
                    gf_ref, sh_ref, sc_ref, wrh_ref, wrl_ref, xo_ref, hp_ref, lg_ref, mix):
    dv = GDN_DK
    o = of_ref[...] + ob_ref[...]
    for h in range(GDN_HEADS):
        hs = slice(h * dv, (h + 1) * dv)
        y = _rms(o[:, hs], gog_ref[...])
        mix[:, hs] = (y * _silu(z_ref[:, hs].astype(F32))).astype(BF16)
    hv = GDN_HEADS * dv
    mix[:, hv:] = ym_ref[...]
    xn = x_ref[...] + g1_ref[...] * _dot(mix[...], wout_ref[...])
    xo_ref[...] = xn
    _ffn_front(xn, gf_ref, sh_ref, sc_ref, wrh_ref, wrl_ref, hp_ref, lg_ref)


def _mix_out(o2, p_all, ymla, x, gog, wout, g1, gf, sh2, sc2, wrh, wrl, nct_tiles):
    bsz, seq, d = x.shape
    tm = SEQ_TILE
    hv = GDN_HEADS * GDN_DK
    vec = lambda: pl.BlockSpec((None, 1, d), lambda b, i: (b, 0, 0))
    full = lambda a: pl.BlockSpec(a.shape, lambda b, i: (0,) * a.ndim)
    return pl.pallas_call(
        _mix_out_kernel,
        grid=(bsz, seq // tm),
        in_specs=[pl.BlockSpec((None, None, tm, hv), lambda b, i: (0, b, i, 0)),
                  pl.BlockSpec((None, None, tm, hv), lambda b, i: (1, b, i, 0)),
                  pl.BlockSpec((None, tm, hv), lambda b, i: (b, i + nct_tiles, P_Z // hv)),
                  pl.BlockSpec((None, tm, hv), lambda b, i: (b, i, 0)),
                  pl.BlockSpec((None, tm, d), lambda b, i: (b, i, 0)),
                  full(gog), full(wout), vec(), full(gf), vec(), vec(), full(wrh), full(wrl)],
        out_specs=[pl.BlockSpec((None, tm, d), lambda b, i: (b, i, 0)),
                   pl.BlockSpec((None, tm, d // 2), lambda b, i: (b, i, 0)),
                   pl.BlockSpec((None, tm, LANE), lambda b, i: (b, i, 0))],
        out_shape=[jax.ShapeDtypeStruct((bsz, seq, d), F32),
                   jax.ShapeDtypeStruct((bsz, seq, d // 2), U32),
                   jax.ShapeDtypeStruct((bsz, seq, LANE), F32)],
        scratch_shapes=[pltpu.VMEM((tm, wout.shape[0]), BF16)],
        compiler_params=_params(("parallel", "parallel")),
    )(o2, o2, p_all, ymla, x, gog, wout, g1, gf, sh2, sc2, wrh, wrl)


POOL_HALO = 8


def _pool_kernel(seq, xm_ref, xp_ref, xn_ref, gm_ref, sh1_ref, sc1_ref, wp_ref, ps_ref, g1_ref,
                 gf_ref, sh_ref, sc_ref, wrh_ref, wrl_ref, xo_ref, hp_ref, lg_ref, ext):
    i = pl.program_id(1)
    n = pl.num_programs(1)
    tm = xm_ref.shape[0]
    hl = POOL_HALO

    def normed(ref):
        return _rms(ref[...], gm_ref[...]) * (1.0 + sc1_ref[...]) + sh1_ref[...]

    ext[hl:hl + tm, :] = normed(xm_ref)
    ext[0:hl, :] = jnp.where(i == 0, 0.0, normed(xp_ref))
    ext[hl + tm:2 * hl + tm, :] = jnp.where(i == n - 1, 0.0, normed(xn_ref))
    t = i * tm + lax.broadcasted_iota(I32, (tm, 1), 0)
    gw = xm_ref.shape[1] // len(POOL_WINDOWS)
    for gi, win in enumerate(POOL_WINDOWS):
        cs = slice(gi * gw, (gi + 1) * gw)
        half = win // 2
        acc = None
        for off in range(-half, win - half):
            term = ext[hl + off:hl + off + tm, cs]
            acc = term if acc is None else acc + term
        lo = jnp.clip(t - half, 0, seq)
        hi = jnp.clip(t - half + win, 0, seq)
        pooled = acc / (hi - lo).astype(F32) - ext[hl:hl + tm, cs]
        y = _dot(pooled.astype(BF16), wp_ref[gi]) * ps_ref[:, cs]
        xo_ref[:, cs] = xm_ref[:, cs] + g1_ref[:, cs] * y
    _ffn_front(xo_ref[...], gf_ref, sh_ref, sc_ref, wrh_ref, wrl_ref, hp_ref, lg_ref)


def _pool_mixer(x, gm, sh1, sc1, wp, ps, g1, gf, sh2, sc2, wrh, wrl):
    bsz, seq, d = x.shape
    tm = SEQ_TILE
    hl = POOL_HALO
    r = tm // hl
    nh = seq // hl
    vec = lambda: pl.BlockSpec((None, 1, d), lambda b, i: (b, 0, 0))
    full = lambda a: pl.BlockSpec(a.shape, lambda b, i: (0,) * a.ndim)
    return pl.pallas_call(
        functools.partial(_pool_kernel, seq),
        grid=(bsz, seq // tm),
        in_specs=[pl.BlockSpec((None, tm, d), lambda b, i: (b, i, 0)),
                  pl.BlockSpec((None, hl, d), lambda b, i: (b, jnp.maximum(i * r - 1, 0), 0)),
                  pl.BlockSpec((None, hl, d), lambda b, i: (b, jnp.minimum((i + 1) * r, nh - 1), 0)),
                  full(gm), vec(), vec(), full(wp), full(ps), vec(),
                  full(gf), vec(), vec(), full(wrh), full(wrl)],
        out_specs=[pl.BlockSpec((None, tm, d), lambda b, i: (b, i, 0)),
                   pl.BlockSpec((None, tm, d // 2), lambda b, i: (b, i, 0)),
                   pl.BlockSpec((None, tm, LANE), lambda b, i: (b, i, 0))],
        out_shape=[jax.ShapeDtypeStruct((bsz, seq, d), F32),
                   jax.ShapeDtypeStruct((bsz, seq, d // 2), U32),
                   jax.ShapeDtypeStruct((bsz, seq, LANE), F32)],
        scratch_shapes=[pltpu.VMEM((tm + 2 * hl, d), F32)],
        compiler_params=_params(("parallel", "parallel")),
    )(x, x, x, gm, sh1, sc1, wp, ps, g1, gf, sh2, sc2, wrh, wrl)


def _first_max(vals, idx, sentinel):
    m = jnp.max(vals, axis=0, keepdims=True)
    first = jnp.min(jnp.where(vals == m, idx, sentinel), axis=0, keepdims=True)
    return m, first


def _route_kernel(rows, lg_ref, bias_ref, dest_ref, gates_ref, cnt_ref, carry):
    phase = pl.program_id(0)
    step = pl.program_id(1)

    @pl.when(jnp.logical_and(phase == 0, step == 0))
    def _():
        carry[...] = jnp.zeros_like(carry)

    @pl.when(jnp.logical_and(phase == 1, step == 0))
    def _():
        shift = int(math.log2(rows))
        counts_row = carry[...].T.astype(I32)
        padded = jnp.left_shift(jnp.right_shift(counts_row + (rows - 1), shift), shift).astype(F32)
        i_exp = lax.broadcasted_iota(I32, padded.shape, 0)
        j_exp = lax.broadcasted_iota(I32, padded.shape, 1)
        first_row = jnp.sum(jnp.where(j_exp < i_exp, padded, 0.0), axis=1, keepdims=True)
        carry[...] = jnp.broadcast_to(first_row, carry.shape)

    tm = lg_ref.shape[0]
    epg = EXPERTS_PER_GROUP
    scores = jax.nn.sigmoid(lg_ref[...].T[0:N_EXPERTS, :])
    biased = scores + bias_ref[:, 0:1]
    eidx = lax.broadcasted_iota(I32, (epg, tm), 0)
    best_score = best_grp = best_i1 = best_i2 = None
    for g in range(N_GROUPS):
        blk = biased[g * epg:(g + 1) * epg, :]
        m1, i1 = _first_max(blk, eidx, epg)
        m2, i2 = _first_max(jnp.where(eidx == i1, -jnp.inf, blk), eidx, epg)
        gs = m1 + m2
        if g == 0:
            best_score, best_grp, best_i1, best_i2 = gs, jnp.zeros_like(i1), i1, i2
        else:
            better = gs > best_score
            best_score = jnp.where(better, gs, best_score)
            best_grp = jnp.where(better, g, best_grp)
            best_i1 = jnp.where(better, i1, best_i1)
            best_i2 = jnp.where(better, i2, best_i2)
    e1 = best_grp * epg + best_i1
    e2 = best_grp * epg + best_i2
    eall = lax.broadcasted_iota(I32, (LANE, tm), 0)
    oh1 = eall == e1
    oh2 = eall == e2
    oh = jnp.logical_or(oh1, oh2).astype(BF16)
    picked = jnp.sum(oh.astype(F32), axis=1, keepdims=True)

    @pl.when(phase == 0)
    def _():
        carry[...] = carry[...] + picked
        cnt_ref[...] = carry[...]

    @pl.when(phase == 1)
    def _():
        s1 = jnp.sum(jnp.where(oh1[0:N_EXPERTS], scores, 0.0), axis=0, keepdims=True)
        s2 = jnp.sum(jnp.where(oh2[0:N_EXPERTS], scores, 0.0), axis=0, keepdims=True)
        denom = s1 + s2
        earlier = (lax.broadcasted_iota(I32, (tm, tm), 0) < lax.broadcasted_iota(I32, (tm, tm), 1)).astype(BF16)
        row = _dot(oh, earlier) + carry[:, 0:1]
        d1 = jnp.sum(jnp.where(oh1, row, 0.0), axis=0, keepdims=True)
        d2 = jnp.sum(jnp.where(oh2, row, 0.0), axis=0, keepdims=True)
        carry[...] = carry[...] + picked
        dest_ref[...] = jnp.concatenate([d1.astype(I32), d2.astype(I32), jnp.zeros((6, tm), I32)], axis=0)
        gates = jnp.concatenate([s1 / denom, s2 / denom, jnp.zeros((LANE - 2, tm), F32)], axis=0)
        gates_ref[...] = gates.T


def _route(logits, bias_col):
    n = logits.shape[0]
    tm = ROUTE_TILE
    rows = MOE_ROWS
    assert rows & (rows - 1) == 0
    return pl.pallas_call(
        functools.partial(_route_kernel, rows),
        grid=(2, n // tm),
        in_specs=[pl.BlockSpec((tm, LANE), lambda p, t: (t, 0)),
                  pl.BlockSpec(bias_col.shape, lambda p, t: (0, 0))],
        out_specs=[pl.BlockSpec((8, tm), lambda p, t: (0, p * t)),
                   pl.BlockSpec((tm, LANE), lambda p, t: (p * t, 0)),
                   pl.BlockSpec((LANE, LANE), lambda p, t: (0, 0))],
        out_shape=[jax.ShapeDtypeStruct((8, n), I32),
                   jax.ShapeDtypeStruct((n, LANE), F32),
                   jax.ShapeDtypeStruct((LANE, LANE), F32)],
        scratch_shapes=[pltpu.VMEM((LANE, LANE), F32)],
        compiler_params=_params(("arbitrary", "arbitrary")),
    )(logits, bias_col)


def _dispatch_copy(hp_ref, xs_hbm, sem, j, dst):
    return pltpu.make_async_copy(hp_ref.at[pl.ds(j, 1), :], xs_hbm.at[pl.ds(dst, 1), :], sem)


def _zero_block_copy(zbuf, xs_hbm, zsem, start):
    return pltpu.make_async_copy(zbuf, xs_hbm.at[pl.ds(start, zbuf.shape[0]), :], zsem)


def _dispatch_kernel(tm, pad_end_ref, nu_ref, d0_ref, d1_ref, hp_ref, xs_hbm, zbuf, sem, zsem):
    rows = zbuf.shape[0]
    n_blocks = xs_hbm.shape[0] // rows

    @pl.when(pl.program_id(0) == 0)
    def _():
        zbuf[...] = jnp.zeros_like(zbuf)

        def expert_tail(e, carry):
            start = jnp.maximum(pad_end_ref[e] - rows, 0)
            _zero_block_copy(zbuf, xs_hbm, zsem, pl.multiple_of(start, rows)).start()
            return carry

        lax.fori_loop(0, N_EXPERTS, expert_tail, 0)

        def unused_block(b, carry):
            _zero_block_copy(zbuf, xs_hbm, zsem, pl.multiple_of(b * rows, rows)).start()
            return carry

        lax.fori_loop(nu_ref[0], n_blocks, unused_block, 0)

        def drain(b, carry):
            _zero_block_copy(zbuf, xs_hbm, zsem, 0).wait()
            return carry

        lax.fori_loop(0, N_EXPERTS + n_blocks - nu_ref[0], drain, 0)

    def issue(j, carry):
        _dispatch_copy(hp_ref, xs_hbm, sem, j, d0_ref[0, j]).start()
        _dispatch_copy(hp_ref, xs_hbm, sem, j, d1_ref[0, j]).start()
        return carry

    lax.fori_loop(0, tm, issue, 0, unroll=DMA_UNROLL)
    for _ in range(2):
        pltpu.make_async_copy(hp_ref, xs_hbm.at[pl.ds(0, tm), :], sem).wait()


def _dispatch(hp, dest, pad_end, n_used, n_rows):
    n, half = hp.shape
    tm = ROUTE_TILE
    nt = n // tm
    idx = lambda k: pl.BlockSpec((None, None, 1, tm), lambda i, pe, nu: (k, i, 0, 0), memory_space=pltpu.SMEM)
    dest4 = dest.reshape(dest.shape[0], nt, 1, tm)
    grid_spec = pltpu.PrefetchScalarGridSpec(
        num_scalar_prefetch=2,
        grid=(nt,),
        in_specs=[idx(0), idx(1), pl.BlockSpec((tm, half), lambda i, pe, nu: (i, 0))],
        out_specs=pl.BlockSpec(memory_space=pl.ANY),
        scratch_shapes=[pltpu.VMEM((MOE_ROWS, half), U32), pltpu.SemaphoreType.DMA(()),
                        pltpu.SemaphoreType.DMA(())],
    )
    return pl.pallas_call(
        functools.partial(_dispatch_kernel, tm),
        grid_spec=grid_spec,
        out_shape=jax.ShapeDtypeStruct((n_rows, half), U32),
        compiler_params=_params(("arbitrary",)),
    )(pad_end, n_used, dest4, dest4, hp)


def _experts_kernel(be_ref, nu_ref, xs_ref, wgu_ref, wd_ref, y_ref, wgu16, wd16):
    i = pl.program_id(0)

    @pl.when(i < nu_ref[0])
    def _():
        @pl.when(jnp.logical_or(i == 0, be_ref[i] != be_ref[jnp.maximum(i - 1, 0)]))
        def _():
            wgu16[...] = wgu_ref[...].astype(BF16)
            wd16[...] = wd_ref[...].astype(BF16)

        lo, hi = _unpack_halves(xs_ref[...])
        half = lo.shape[1]
        gu = _dot(lo.astype(BF16), wgu16[0:half, :]) + _dot(hi.astype(BF16), wgu16[half:, :])
        f = gu.shape[1] // 2
        hid = _silu(gu[:, :f]) * gu[:, f:]
        y_ref[...] = _pack_halves(_dot(hid.astype(BF16), wd16[...]))

    @pl.when(i >= nu_ref[0])
    def _():
        y_ref[...] = jnp.zeros_like(y_ref)


def _experts(layer, block_expert, n_used, xs, w_gate_up, w_down):
    n_rows, half = xs.shape
    rows = MOE_ROWS
    d = 2 * half
    f2 = w_gate_up.shape[3]
    grid_spec = pltpu.PrefetchScalarGridSpec(
        num_scalar_prefetch=2,
        grid=(n_rows // rows,),
        in_specs=[pl.BlockSpec((rows, half), lambda i, be, nu: (i, 0)),
                  pl.BlockSpec((None, None, d, f2), lambda i, be, nu: (layer, be[i], 0, 0)),
                  pl.BlockSpec((None, None, f2 // 2, d), lambda i, be, nu: (layer, be[i], 0, 0))],
        out_specs=pl.BlockSpec((rows, half), lambda i, be, nu: (i, 0)),
        scratch_shapes=[pltpu.VMEM((d, f2), BF16), pltpu.VMEM((f2 // 2, d), BF16)],
    )
    return pl.pallas_call(
        _experts_kernel,
        grid_spec=grid_spec,
        out_shape=jax.ShapeDtypeStruct((n_rows, half), U32),
        compiler_params=_params(("arbitrary",)),
    )(block_expert, n_used, xs, w_gate_up, w_down)


def _combine_copy(y_hbm, buf, sem, slot, k, src, j):
    return pltpu.make_async_copy(y_hbm.at[pl.ds(src, 1), :], buf.at[slot, k, pl.ds(j, 1), :], sem.at[slot])


def _combine_kernel(tm, d0_ref, d1_ref, d0n_ref, d1n_ref, y_hbm, gates_ref, x_ref, g2_ref, xo_ref, buf, sem):
    step = pl.program_id(0) * pl.num_programs(1) + pl.program_id(1)
    total = pl.num_programs(0) * pl.num_programs(1)
    slot = step % 2

    def start_gather(r0_ref, r1_ref, s):
        def issue(j, carry):
            _combine_copy(y_hbm, buf, sem, s, 0, r0_ref[0, j], j).start()
            _combine_copy(y_hbm, buf, sem, s, 1, r1_ref[0, j], j).start()
            return carry
        lax.fori_loop(0, tm, issue, 0, unroll=DMA_UNROLL)

    @pl.when(step == 0)
    def _():
        start_gather(d0_ref, d1_ref, 0)

    @pl.when(step + 1 < total)
    def _():
        start_gather(d0n_ref, d1n_ref, 1 - slot)

    for k in range(2):
        pltpu.make_async_copy(y_hbm.at[pl.ds(0, tm), :], buf.at[slot, k], sem.at[slot]).wait()
    lo0, hi0 = _unpack_halves(buf[slot, 0])
    lo1, hi1 = _unpack_halves(buf[slot, 1])
    half = lo0.shape[1]
    ga = gates_ref[:, 0:1]
    gb = gates_ref[:, 1:2]
    xo_ref[:, 0:half] = x_ref[:, 0:half] + g2_ref[:, 0:half] * (ga * lo0 + gb * lo1)
    xo_ref[:, half:] = x_ref[:, half:] + g2_ref[:, half:] * (ga * hi0 + gb * hi1)


def _combine(y, dest, gates, x, g2):
    bsz, seq, d = x.shape
    tm = SEQ_TILE
    nt = seq // tm
    half = d // 2
    last = bsz * nt - 1
    idx = lambda k, ahead: pl.BlockSpec(
        (None, None, 1, tm), lambda b, i: (k, jnp.minimum(b * nt + i + ahead, last), 0, 0),
        memory_space=pltpu.SMEM)
    dest4 = dest.reshape(dest.shape[0], bsz * nt, 1, tm)
    return pl.pallas_call(
        functools.partial(_combine_kernel, tm),
        grid=(bsz, nt),
        in_specs=[idx(0, 0), idx(1, 0), idx(0, 1), idx(1, 1),
                  pl.BlockSpec(memory_space=pl.ANY),
                  pl.BlockSpec((tm, LANE), lambda b, i: (b * nt + i, 0)),
                  pl.BlockSpec((None, tm, d), lambda b, i: (b, i, 0)),
                  pl.BlockSpec((None, 1, d), lambda b, i: (b, 0, 0))],
        out_specs=pl.BlockSpec((None, tm, d), lambda b, i: (b, i, 0)),
        out_shape=jax.ShapeDtypeStruct((bsz, seq, d), F32),
        scratch_shapes=[pltpu.VMEM((2, 2, tm, half), U32), pltpu.SemaphoreType.DMA((2,))],
        compiler_params=_params(("arbitrary", "arbitrary")),
    )(dest4, dest4, dest4, dest4, y, gates, x, g2)


def _moe(layer, x, hp, logits, g2, bias_col, w_gate_up, w_down):
    bsz, seq, d = x.shape
    n = bsz * seq
    rows = MOE_ROWS
    dest, gates, cnt = _route(logits.reshape(n, LANE), bias_col)
    counts = cnt[:N_EXPERTS, 0].astype(I32)
    pad_end = jnp.cumsum((counts + rows - 1) // rows * rows)
    n_rows = -(-(n * TOP_K + N_EXPERTS * (rows - 1)) // rows) * rows
    block_start = jnp.arange(n_rows // rows, dtype=I32) * rows
    block_expert = jnp.minimum(jnp.sum((pad_end[None, :] <= block_start[:, None]).astype(I32), axis=1),
                               N_EXPERTS - 1)
    n_used = pad_end[-1:] // rows
    xs = _dispatch(hp.reshape(n, d // 2), dest, pad_end, n_used, n_rows)
    y = _experts(layer, block_expert, n_used, xs, w_gate_up, w_down)
    return _combine(y, dest, gates, x, g2)


def _permute_w_in(w_in):
    d = w_in.shape[0]
    hq = GDN_HEADS * GDN_DK
    off_a = 4 * hq
    off_cq = off_a + 4 * GDN_HEADS
    off_ckv = off_cq + Q_LORA
    off_kr = off_ckv + KV_LORA
    z = lambda n: jnp.zeros((d, n), w_in.dtype)
    a = lambda k: w_in[:, off_a + k * GDN_HEADS:off_a + (k + 1) * GDN_HEADS]
    pad_ab = LANE - 2 * GDN_HEADS
    cols = [w_in[:, :off_a], w_in[:, off_cq:off_kr + ROPE_DIM], z(LANE - ROPE_DIM),
            a(0), a(2), z(pad_ab), a(1), a(3), z(pad_ab)]
    w = jnp.concatenate(cols, axis=1)
    return jnp.concatenate([w, z(P_WIDTH - w.shape[1])], axis=1).astype(BF16)


def _rope_tables(rows, n_ctx):
    row = jnp.repeat(jnp.arange(rows, dtype=F32), GRID_W)
    col = jnp.tile(jnp.arange(GRID_W, dtype=F32), rows)
    pairs = ROPE_DIM // 4
    inv_freq = ROPE_THETA ** (-jnp.arange(pairs, dtype=F32) / pairs)
    ang_r = row[:, None] * inv_freq
    ang_c = col[:, None] * inv_freq
    ang = jnp.concatenate([ang_r, ang_r, ang_c, ang_c], axis=-1)
    ang = jnp.concatenate([jnp.zeros((n_ctx, ROPE_DIM), F32), ang], axis=0)
    cos, sin = jnp.cos(ang), jnp.sin(ang)
    return jnp.concatenate([cos, cos], axis=1), jnp.concatenate([sin, sin], axis=1)


def kernel(x, c, ctx, c_ctx, w_mod, b_mod, norm_mix_g, norm_ffn_g, w_in, conv_qkv, a_log_fwd, a_log_bwd, dt_bias_fwd, dt_bias_bwd, gdn_out_g, q_a_norm_g, w_uq, kv_a_norm_g, w_ukv, q_norm_g, k_norm_g, w_out, w_pool, pool_scale, w_router, router_bias, w_gate_up, w_down):
    bsz, seq, d = x.shape
    n_ctx = ctx.shape[1]
    depth = w_mod.shape[0]
    assert depth == 2, "the context stream is only read: no layer after the first even layer reads it"
    nct_tiles = n_ctx // SEQ_TILE

    cc = jnp.concatenate([c, c_ctx[None, :], jnp.zeros((8 - bsz - 1, d), F32)], axis=0)
    mod = _modulation(cc, w_mod, b_mod).reshape(depth, 8, 6, d)

    def mods(layer):
        m = mod[layer]
        return [m[:bsz, k][:, None, :] for k in range(6)], [m[bsz:bsz + 1, k] for k in range(6)]

    wr = jnp.pad(w_router, ((0, 0), (0, LANE - N_EXPERTS)))
    wrh = wr.astype(BF16)
    wrl = jnp.concatenate([wrh, (wr - wrh.astype(F32)).astype(BF16)], axis=1)
    bias_col = jnp.broadcast_to(router_bias.astype(F32)[:, None], (N_EXPERTS, LANE))
    row = lambda v: v.astype(F32)[None, :]

    for layer in range(depth):
        j = layer // 2
        (sh1, sc1, g1, sh2, sc2, g2), (csh1, csc1, _, _, _, _) = mods(layer)
        gm = row(norm_mix_g[layer])
        gf = row(norm_ffn_g[layer])
        if layer % 2 == 0:
            p_all = _in_projection(ctx, x, gm, csh1, csc1, sh1, sc1, _permute_w_in(w_in[j]))
            conv_w = conv_qkv[j].astype(F32)
            gate_params = jnp.zeros((2, 8, LANE), F32)
            gate_params = gate_params.at[0, 0, :GDN_HEADS].set(a_log_fwd[j]).at[0, 1, :GDN_HEADS].set(dt_bias_fwd[j])
            gate_params = gate_params.at[1, 0, :GDN_HEADS].set(a_log_bwd[j]).at[1, 1, :GDN_HEADS].set(dt_bias_bwd[j])
            qkv_all, gb_all = _gdn_front(p_all, conv_w, gate_params, nct_tiles)
            o2 = _gdn_scan(qkv_all, gb_all, n_ctx)

            cos, sin = _rope_tables(seq // GRID_W, n_ctx)
            gq = q_norm_g[j].astype(F32)
            gk = k_norm_g[j].astype(F32)
            wq = w_uq[j].reshape(Q_LORA, MLA_HEADS, QK_HEAD)
            wq = jnp.concatenate([wq[:, :, :NOPE_DIM].reshape(Q_LORA, -1),
                                  wq[:, :, NOPE_DIM:].reshape(Q_LORA, -1)], axis=1).astype(BF16)
            q = _mla_queries(p_all, row(q_a_norm_g[j]), wq, gq[None, :NOPE_DIM],
                             jnp.tile(gq[NOPE_DIM:], 2)[None, :], cos, sin, nct_tiles, seq)
            gk_rope = jnp.concatenate([gk[NOPE_DIM:], jnp.zeros((LANE - ROPE_DIM,), F32)])[None, :]
            k_all, v_all = _mla_keys_values(p_all, row(kv_a_norm_g[j]), w_ukv[j].astype(BF16),
                                            gk[None, :NOPE_DIM], gk_rope, cos, sin)
            ymla = _attention(q, k_all, v_all)
            x, hp, logits = _mix_out(o2, p_all, ymla, x, row(gdn_out_g[j]), w_out[j].astype(BF16), g1,
                                     gf, sh2, sc2, wrh, wrl, nct_tiles)
        else:
            x, hp, logits = _pool_mixer(x, gm, sh1, sc1, w_pool[j].astype(BF16), row(pool_scale[j]), g1,
                                        gf, sh2, sc2, wrh, wrl)
        x = _moe(layer, x, hp, logits, g2, bias_col, w_gate_up, w_down)
    return x
```

```python
import functools
import math

import jax
import jax.numpy as jnp
from jax import lax
from jax.experimental import pallas as pl
from jax.experimental.pallas import tpu as pltpu

F32 = jnp.float32
BF16 = jnp.bfloat16
U32 = jnp.uint32
I32 = jnp.int32

EPS = 1e-6
GRID_W = 64
GDN_HEADS = 8
GDN_DK = 128
MLA_HEADS = 8
NOPE_DIM = 128
ROPE_DIM = 64
QK_HEAD = NOPE_DIM + ROPE_DIM
QK_PAD = 256
ROPE_THETA = 10000.0
Q_LORA = 512
KV_LORA = 256
POOL_WINDOWS = (2, 4, 8, 16)
N_EXPERTS = 64
EXPERTS_PER_GROUP = 8
N_GROUPS = N_EXPERTS // EXPERTS_PER_GROUP
TOP_K = 2

LANE = 128
V7X_VMEM_LIMIT = 56 * 1024 * 1024

SEQ_TILE = 256
GDN_CHUNK = 128
MOE_ROWS = 256
DMA_UNROLL = 8
ROUTE_TILE = 512

P_QKV = 0
P_Z = 3072
P_CQ = 4096
P_CKV = 4608
P_KR = 4864
P_AB = 4992
P_WIDTH = 5376
P_NTILE = 1792


def _params(sem):
    return pltpu.CompilerParams(dimension_semantics=sem, vmem_limit_bytes=V7X_VMEM_LIMIT)


def _dot(a, b):
    return jnp.dot(a, b, preferred_element_type=F32)


def _dot_nt(a, b):
    return lax.dot_general(a, b, (((1,), (1,)), ((), ())), preferred_element_type=F32)


def _dot_tn(a, b):
    return lax.dot_general(a, b, (((0,), (0,)), ((), ())), preferred_element_type=F32)


def _silu(x):
    return x * jax.nn.sigmoid(x)


def _rms(x, g):
    return x * lax.rsqrt(jnp.mean(x * x, axis=-1, keepdims=True) + EPS) * g


def _split3(x):
    hi = x.astype(BF16)
    r = x - hi.astype(F32)
    mid = r.astype(BF16)
    lo = (r - mid.astype(F32)).astype(BF16)
    return hi, mid, lo


def _mod_kernel(c_ref, w_ref, b_ref, o_ref):
    s = _silu(c_ref[...]).astype(BF16)
    o_ref[...] = _dot(s, w_ref[...].astype(BF16)) + b_ref[...]


def _modulation(cc, w_mod, b_mod):
    depth, d, n6 = w_mod.shape
    tn = 1024
    return pl.pallas_call(
        _mod_kernel,
        grid=(depth, n6 // tn),
        in_specs=[pl.BlockSpec((8, d), lambda l, j: (0, 0)),
                  pl.BlockSpec((None, d, tn), lambda l, j: (l, 0, j)),
                  pl.BlockSpec((None, 1, tn), lambda l, j: (l, 0, j))],
        out_specs=pl.BlockSpec((None, 8, tn), lambda l, j: (l, 0, j)),
        out_shape=jax.ShapeDtypeStruct((depth, 8, n6), F32),
        compiler_params=_params(("parallel", "parallel")),
    )(cc, w_mod, b_mod.reshape(depth, 1, n6))


def _inproj_kernel(nct, xc_ref, x_ref, g_ref, shc_ref, scc_ref, sh_ref, sc_ref, w_ref, o_ref):
    i = pl.program_id(2)

    def run(xr, shr, scr):
        h = _rms(xr[...], g_ref[...]) * (1.0 + scr[...]) + shr[...]
        o_ref[...] = _dot(h.astype(BF16), w_ref[...]).astype(o_ref.dtype)

    @pl.when(i < nct)
    def _():
        run(xc_ref, shc_ref, scc_ref)

    @pl.when(i >= nct)
    def _():
        run(x_ref, sh_ref, sc_ref)


def _in_projection(ctx, x, g, shc, scc, sh, sc, w):
    bsz, seq, d = x.shape
    tm = SEQ_TILE
    nct = ctx.shape[1] // tm
    nt = nct + seq // tm
    width = w.shape[1]
    tn = P_NTILE
    return pl.pallas_call(
        functools.partial(_inproj_kernel, nct),
        grid=(width // tn, bsz, nt),
        in_specs=[pl.BlockSpec((None, tm, d), lambda j, b, i: (b, jnp.minimum(i, nct - 1), 0)),
                  pl.BlockSpec((None, tm, d), lambda j, b, i: (b, jnp.maximum(i - nct, 0), 0)),
                  pl.BlockSpec((1, d), lambda j, b, i: (0, 0)),
                  pl.BlockSpec((1, d), lambda j, b, i: (0, 0)),
                  pl.BlockSpec((1, d), lambda j, b, i: (0, 0)),
                  pl.BlockSpec((None, 1, d), lambda j, b, i: (b, 0, 0)),
                  pl.BlockSpec((None, 1, d), lambda j, b, i: (b, 0, 0)),
                  pl.BlockSpec((d, tn), lambda j, b, i: (0, j))],
        out_specs=pl.BlockSpec((None, tm, tn), lambda j, b, i: (b, i, j)),
        out_shape=jax.ShapeDtypeStruct((bsz, nt * tm, width), BF16),
        compiler_params=_params(("parallel", "parallel", "parallel")),
    )(ctx, x, g, shc, scc, sh, sc, w)


GDN_HALO = 16


def _gdn_front_kernel(nct, pm_ref, pp_ref, pn_ref, cw_ref, ab0_ref, ab1_ref, gp_ref,
                      qkv_ref, gb_ref, ext):
    i = pl.program_id(1)
    n = pl.num_programs(1)
    tm = pm_ref.shape[0]
    hl = GDN_HALO
    zero_prev = jnp.logical_or(i == 0, i == nct)
    zero_next = jnp.logical_or(i == nct - 1, i == n - 1)
    ext[hl:hl + tm, :] = pm_ref[...].astype(F32)
    ext[0:hl, :] = jnp.where(zero_prev, 0.0, pp_ref[...].astype(F32))
    ext[hl + tm:2 * hl + tm, :] = jnp.where(zero_next, 0.0, pn_ref[...].astype(F32))
    taps = cw_ref.shape[0]
    nqk = 2 * GDN_HEADS
    for cb in range(qkv_ref.shape[1] // LANE):
        cs = slice(cb * LANE, (cb + 1) * LANE)
        acc = None
        for j in range(taps):
            off = hl - taps // 2 + j
            term = cw_ref[j:j + 1, cs] * ext[off:off + tm, cs]
            acc = term if acc is None else acc + term
        y = _silu(acc)
        if cb < nqk:
            inv = lax.rsqrt(jnp.sum(y * y, axis=-1, keepdims=True) + EPS)
            if cb < GDN_HEADS:
                inv = inv * (GDN_DK ** -0.5)
            y = y * inv
        qkv_ref[:, cs] = y.astype(BF16)
    for d, ab_ref in enumerate((ab0_ref, ab1_ref)):
        a = ab_ref[...].astype(F32)
        lane = lax.broadcasted_iota(I32, a.shape, 1)
        neg_decay = -jnp.exp(gp_ref[d, 0:1, :])
        xx = a + gp_ref[d, 1:2, :]
        softplus = jnp.maximum(xx, 0.0) + jnp.log1p(jnp.exp(-jnp.abs(xx)))
        gate = jnp.where(lane < GDN_HEADS, neg_decay * softplus, jax.nn.sigmoid(a))
        gb_ref[d] = jnp.where(lane < 2 * GDN_HEADS, gate, 0.0)


def _gdn_front(p_all, conv_w, gate_params, nct):
    bsz, ttot, _ = p_all.shape
    tm = SEQ_TILE
    nt = ttot // tm
    cq = P_Z
    hl = GDN_HALO
    r = tm // hl
    nh = ttot // hl
    return pl.pallas_call(
        functools.partial(_gdn_front_kernel, nct),
        grid=(bsz, nt),
        in_specs=[pl.BlockSpec((None, tm, cq), lambda b, i: (b, i, 0)),
                  pl.BlockSpec((None, hl, cq), lambda b, i: (b, jnp.maximum(i * r - 1, 0), 0)),
                  pl.BlockSpec((None, hl, cq), lambda b, i: (b, jnp.minimum((i + 1) * r, nh - 1), 0)),
                  pl.BlockSpec(conv_w.shape, lambda b, i: (0, 0)),
                  pl.BlockSpec((None, tm, LANE), lambda b, i: (b, i, P_AB // LANE)),
                  pl.BlockSpec((None, tm, LANE), lambda b, i: (b, i, P_AB // LANE + 1)),
                  pl.BlockSpec(gate_params.shape, lambda b, i: (0, 0, 0))],
        out_specs=[pl.BlockSpec((None, tm, cq), lambda b, i: (b, i, 0)),
                   pl.BlockSpec((2, None, tm, LANE), lambda b, i: (0, b, i, 0))],
        out_shape=[jax.ShapeDtypeStruct((bsz, ttot, cq), BF16),
                   jax.ShapeDtypeStruct((2, bsz, ttot, LANE), F32)],
        scratch_shapes=[pltpu.VMEM((tm + 2 * hl, cq), F32)],
        compiler_params=_params(("parallel", "parallel")),
    )(p_all, p_all, p_all, conv_w, p_all, p_all, gate_params)


def _gdn_scan_kernel(q_ref, k_ref, v_ref, gb_ref, o_ref, s_scr):
    d = pl.program_id(1)
    s = pl.program_id(2)
    c = GDN_CHUNK
    n_sub = q_ref.shape[0] // c
    dk = GDN_DK

    @pl.when(s == 0)
    def _():
        s_scr[...] = jnp.zeros_like(s_scr)

    ri = lax.broadcasted_iota(I32, (c, c), 0)
    ci = lax.broadcasted_iota(I32, (c, c), 1)
    incl = jnp.where(d == 0, ri - ci, ci - ri) >= 0
    strict = jnp.logical_and(incl, ci != ri)
    m_incl = incl.astype(BF16)
    ones = jnp.ones((c, c), BF16)
    eye = (ri == ci).astype(F32)
    pair_masks = []
    for lb in range(int(math.log2(c))):
        same_pair = jnp.right_shift(ri, lb + 1) == jnp.right_shift(ci, lb + 1)
        same_block = jnp.right_shift(ri, lb) == jnp.right_shift(ci, lb)
        pair_masks.append(jnp.logical_and(same_pair, jnp.logical_not(same_block)))
    heads = range(GDN_HEADS)
    col = lambda a, h: a[:, h:h + 1]

    rows = [pl.ds(pl.multiple_of(jnp.where(d == 0, sub, n_sub - 1 - sub) * c, c), c) for sub in range(n_sub)]
    subs = range(n_sub)
    pairs = [(sub, h) for sub in subs for h in heads]
    hcols = lambda ref, sub, h: ref[rows[sub], h * dk:(h + 1) * dk]
    gb = [gb_ref[rows[sub], :] for sub in subs]
    lane = lax.broadcasted_iota(I32, gb[0].shape, 1)
    gsplit = [_split3(jnp.where(lane < GDN_HEADS, gb[sub], 0.0)) for sub in subs]
    gc = [sum(_dot(m_incl, t) for t in gsplit[sub]) for sub in subs]
    gl = [sum(_dot(ones, t) for t in gsplit[sub]) for sub in subs]
    gc_t = [gc[sub].T for sub in subs]
    e_gc = [jnp.exp(gc[sub]) for sub in subs]
    e_rest = [jnp.exp(gl[sub] - gc[sub]) for sub in subs]
    e_all = [jnp.exp(gl[sub]) for sub in subs]
    qb = [hcols(q_ref, sub, h) for sub, h in pairs]
    kb16 = [hcols(k_ref, sub, h) for sub, h in pairs]
    k = [a.astype(F32) for a in kb16]
    beta = [col(gb[sub], GDN_HEADS + h) for sub, h in pairs]
    npair = range(len(pairs))
    kbeta = [k[i] * beta[i] for i in npair]
    kk = [_dot_nt(kbeta[i].astype(BF16), kb16[i]) for i in npair]
    qk = [_dot_nt(qb[i], kb16[i]) for i in npair]
    dec = [jnp.where(incl, jnp.exp(jnp.where(incl, col(gc[sub], h) - gc_t[sub][h:h + 1, :], 0.0)), 0.0)
           for sub, h in pairs]
    low = [jnp.where(strict, kk[i] * dec[i], 0.0) for i in npair]
    r = [-jnp.where(pair_masks[0], low[i], 0.0) for i in npair]
    for mask in pair_masks[1:]:
        dmat = [(r[i] + eye).astype(BF16) for i in npair]
        cd = [_dot(jnp.where(mask, low[i], 0.0).astype(BF16), dmat[i]) for i in npair]
        r = [r[i] - _dot(dmat[i], cd[i].astype(BF16)) for i in npair]
    rhs = [jnp.concatenate([hcols(v_ref, sub, h).astype(F32) * beta[i], kbeta[i] * col(e_gc[sub], h)], axis=1)
           for i, (sub, h) in enumerate(pairs)]
    uw16 = [(rhs[i] + _dot(r[i].astype(BF16), rhs[i].astype(BF16))).astype(BF16) for i in npair]
    kt = [_dot_tn((k[i] * col(e_rest[sub], h)).astype(BF16), uw16[i])
          for i, (sub, h) in enumerate(pairs)]
    qu = [_dot((qk[i] * dec[i]).astype(BF16), uw16[i]) for i in npair]
    qeff = [(qb[i].astype(F32) * col(e_gc[sub], h) - qu[i][:, dk:]).astype(BF16)
            for i, (sub, h) in enumerate(pairs)]
    st = [s_scr[h] for h in heads]
    for sub in subs:
        st16 = [a.astype(BF16) for a in st]
        for h in heads:
            i = sub * GDN_HEADS + h
            o_ref[rows[sub], h * dk:(h + 1) * dk] = _dot(qeff[i], st16[h]) + qu[i][:, :dk]
        st = [st[h] * e_all[sub][0:1, h:h + 1] + kt[sub * GDN_HEADS + h][:, :dk]
              - _dot(kt[sub * GDN_HEADS + h][:, dk:].astype(BF16), st16[h]) for h in heads]
    for h in heads:
        s_scr[h] = st[h]


GDN_CHUNKS_PER_STEP = 2


def _gdn_scan(qkv_all, gb_all, nct_tokens):
    bsz, ttot, _ = qkv_all.shape
    c = GDN_CHUNK * GDN_CHUNKS_PER_STEP
    assert nct_tokens % c == 0 and ttot % c == 0
    nct = nct_tokens // c
    ns = ttot // c
    nlat = ns - nct
    hv = GDN_HEADS * GDN_DK

    def tmap(d, s):
        rev = jnp.where(s < nct, nct - 1 - s, 2 * nct + nlat - 1 - s)
        return jnp.where(d == 0, s, rev)

    def omap(d, s):
        first = jnp.where(d == 0, 0, nlat - 1)
        return jnp.where(s < nct, first, tmap(d, s) - nct)

    return pl.pallas_call(
        _gdn_scan_kernel,
        grid=(bsz, 2, ns),
        in_specs=[pl.BlockSpec((None, c, hv), lambda b, d, s: (b, tmap(d, s), 0)),
                  pl.BlockSpec((None, c, hv), lambda b, d, s: (b, tmap(d, s), 1)),
                  pl.BlockSpec((None, c, hv), lambda b, d, s: (b, tmap(d, s), 2)),
                  pl.BlockSpec((None, None, c, LANE), lambda b, d, s: (d, b, tmap(d, s), 0))],
        out_specs=pl.BlockSpec((None, None, c, hv), lambda b, d, s: (d, b, omap(d, s), 0)),
        out_shape=jax.ShapeDtypeStruct((2, bsz, nlat * c, hv), F32),
        scratch_shapes=[pltpu.VMEM((GDN_HEADS, GDN_DK, GDN_DK), F32)],
        compiler_params=_params(("parallel", "parallel", "arbitrary")),
    )(qkv_all, qkv_all, qkv_all, gb_all)


def _rope_tile(xr, cos, sin):
    lane = lax.broadcasted_iota(I32, xr.shape, 1)
    first_half = (lane % 32) < 16
    rot = jnp.where(first_half, -pltpu.roll(xr, LANE - 16, 1), pltpu.roll(xr, 16, 1))
    return xr * cos + rot * sin


def _mla_q_kernel(c_ref, ga_ref, w_ref, gn_ref, gr_ref, cos_ref, sin_ref, q_ref):
    cn = _rms(c_ref[...].astype(F32), ga_ref[...]).astype(BF16)
    qf = _dot(cn, w_ref[...])
    tm = qf.shape[0]
    lane = lax.broadcasted_iota(I32, (tm, LANE), 1)
    left = lane < ROPE_DIM
    rope_base = MLA_HEADS * NOPE_DIM
    scale = QK_HEAD ** -0.5 * math.log2(math.e)
    cos = cos_ref[...]
    sin = sin_ref[...]
    for hp in range(MLA_HEADS // 2):
        rt = qf[:, rope_base + hp * LANE:rope_base + (hp + 1) * LANE]
        rsq = rt * rt
        ss_left = jnp.sum(jnp.where(left, rsq, 0.0), axis=-1, keepdims=True)
        ss_right = jnp.sum(jnp.where(left, 0.0, rsq), axis=-1, keepdims=True)
        invs = []
        for par, ss_r in ((0, ss_left), (1, ss_right)):
            h = 2 * hp + par
            nope = qf[:, h * NOPE_DIM:(h + 1) * NOPE_DIM]
            ss = jnp.sum(nope * nope, axis=-1, keepdims=True) + ss_r
            inv = lax.rsqrt(ss * (1.0 / QK_HEAD) + EPS)
            invs.append(inv)
            q_ref[h, :, 0:NOPE_DIM] = (nope * inv * gn_ref[...] * scale).astype(BF16)
        inv_lane = jnp.where(left, invs[0], invs[1])
        xr = _rope_tile(rt * inv_lane * gr_ref[...], cos, sin) * scale
        q_ref[2 * hp, :, NOPE_DIM:QK_PAD] = jnp.where(left, xr, 0.0).astype(BF16)
        q_ref[2 * hp + 1, :, NOPE_DIM:QK_PAD] = jnp.where(left, pltpu.roll(xr, ROPE_DIM, 1), 0.0).astype(BF16)


def _mla_queries(p_all, ga, w, gn, gr, cos, sin, nct_tiles, seq):
    bsz = p_all.shape[0]
    tm = SEQ_TILE
    return pl.pallas_call(
        _mla_q_kernel,
        grid=(bsz, seq // tm),
        in_specs=[pl.BlockSpec((None, tm, Q_LORA), lambda b, i: (b, i + nct_tiles, P_CQ // Q_LORA)),
                  pl.BlockSpec(ga.shape, lambda b, i: (0, 0)),
                  pl.BlockSpec(w.shape, lambda b, i: (0, 0)),
                  pl.BlockSpec(gn.shape, lambda b, i: (0, 0)),
                  pl.BlockSpec(gr.shape, lambda b, i: (0, 0)),
                  pl.BlockSpec((tm, LANE), lambda b, i: (i + nct_tiles, 0)),
                  pl.BlockSpec((tm, LANE), lambda b, i: (i + nct_tiles, 0))],
        out_specs=pl.BlockSpec((None, MLA_HEADS, tm, QK_PAD), lambda b, i: (b, 0, i, 0)),
        out_shape=jax.ShapeDtypeStruct((bsz, MLA_HEADS, seq, QK_PAD), BF16),
        compiler_params=_params(("parallel", "parallel")),
    )(p_all, ga, w, gn, gr, cos, sin)


def _mla_kv_kernel(c_ref, kr_ref, ga_ref, w_ref, gn_ref, gr_ref, cos_ref, sin_ref, k_ref, v_ref):
    cn = _rms(c_ref[...].astype(F32), ga_ref[...]).astype(BF16)
    kv = _dot(cn, w_ref[...])
    kr = kr_ref[...].astype(F32)
    ss_r = jnp.sum(kr * kr, axis=-1, keepdims=True)
    kr_rot = _rope_tile(kr * gr_ref[...], cos_ref[...], sin_ref[...])
    width = NOPE_DIM + LANE
    ones_col = (lax.broadcasted_iota(I32, kr.shape, 1) == 0).astype(BF16)
    for h in range(MLA_HEADS):
        nope = kv[:, h * width:h * width + NOPE_DIM]
        ss = jnp.sum(nope * nope, axis=-1, keepdims=True) + ss_r
        inv = lax.rsqrt(ss * (1.0 / QK_HEAD) + EPS)
        k_ref[h, 0:NOPE_DIM, :] = (nope * inv * gn_ref[...]).T.astype(BF16)
        k_ref[h, NOPE_DIM:QK_PAD, :] = (kr_rot * inv).T.astype(BF16)
        v_ref[h, :, 0:LANE] = kv[:, h * width + NOPE_DIM:(h + 1) * width].astype(BF16)
        v_ref[h, :, LANE:2 * LANE] = ones_col


def _mla_keys_values(p_all, ga, w, gn, gr, cos, sin):
    bsz, ttot, _ = p_all.shape
    tm = SEQ_TILE
    return pl.pallas_call(
        _mla_kv_kernel,
        grid=(bsz, ttot // tm),
        in_specs=[pl.BlockSpec((None, tm, KV_LORA), lambda b, i: (b, i, P_CKV // KV_LORA)),
                  pl.BlockSpec((None, tm, LANE), lambda b, i: (b, i, P_KR // LANE)),
                  pl.BlockSpec(ga.shape, lambda b, i: (0, 0)),
                  pl.BlockSpec(w.shape, lambda b, i: (0, 0)),
                  pl.BlockSpec(gn.shape, lambda b, i: (0, 0)),
                  pl.BlockSpec(gr.shape, lambda b, i: (0, 0)),
                  pl.BlockSpec((tm, LANE), lambda b, i: (i, 0)),
                  pl.BlockSpec((tm, LANE), lambda b, i: (i, 0))],
        out_specs=[pl.BlockSpec((None, MLA_HEADS, QK_PAD, tm), lambda b, i: (b, 0, 0, i)),
                   pl.BlockSpec((None, MLA_HEADS, tm, 2 * LANE), lambda b, i: (b, 0, i, 0))],
        out_shape=[jax.ShapeDtypeStruct((bsz, MLA_HEADS, QK_PAD, ttot), BF16),
                   jax.ShapeDtypeStruct((bsz, MLA_HEADS, ttot, 2 * LANE), BF16)],
        compiler_params=_params(("parallel", "parallel")),
    )(p_all, p_all, ga, w, gn, gr, cos, sin)


FLASH_PARTS = 2
FLASH_ROWS = 64


def _flash_kernel(tk, q_ref, k_ref, v_ref, o_ref, s_a, s_b, p_a, p_b, m_scr, a_scr, acc_scr):
    nk = v_ref.shape[0] // tk
    tq = q_ref.shape[0]
    q = q_ref[...]
    m_scr[...] = jnp.full_like(m_scr, -jnp.inf)
    acc_scr[...] = jnp.zeros_like(acc_scr)

    def scores(j):
        return _dot(q, k_ref[:, pl.ds(pl.multiple_of(j * tk, tk), tk)])

    def update(s_ref, p_ref, j):
        vj = v_ref[pl.ds(pl.multiple_of(j * tk, tk), tk), :]
        half = tq // FLASH_PARTS
        for part in range(FLASH_PARTS):
            for r in range(half // FLASH_ROWS):
                r0 = part * half + r * FLASH_ROWS
                rs = slice(r0, r0 + FLASH_ROWS)
                s = s_ref[rs, :]
                m_prev = m_scr[rs, :]
                m_new = jnp.maximum(m_prev, jnp.max(s, axis=-1, keepdims=True))
                m_scr[rs, :] = m_new
                a_scr[rs, :] = jnp.exp2(m_prev - m_new)
                p_ref[rs, :] = jnp.exp2(s - m_new).astype(BF16)
            hs = slice(part * half, (part + 1) * half)
            acc_scr[hs, :] = a_scr[hs, :] * acc_scr[hs, :] + _dot(p_ref[hs, :], vj)

    s_a[...] = scores(0)

    def body(i, carry):
        s_b[...] = scores(2 * i + 1)
        update(s_a, p_a, 2 * i)
        s_a[...] = scores(2 * i + 2)
        update(s_b, p_b, 2 * i + 1)
        return carry

    lax.fori_loop(0, (nk - 1) // 2, body, 0)
    if nk % 2 == 1:
        update(s_a, p_a, nk - 1)
    else:
        s_b[...] = scores(nk - 1)
        update(s_a, p_a, nk - 2)
        update(s_b, p_b, nk - 1)
    o_ref[...] = (acc_scr[:, 0:LANE] / acc_scr[:, LANE:LANE + 1]).astype(o_ref.dtype)


def _key_tile(ttot):
    for cand in (1280, 1024, 768, 512, 256, 128):
        if ttot % cand == 0:
            return cand
    raise ValueError("key length must be a multiple of 128")


def _attention(q, k, v):
    bsz, heads, seq, _ = q.shape
    ttot = v.shape[2]
    tq = 1024
    tk = _key_tile(ttot)
    return pl.pallas_call(
        functools.partial(_flash_kernel, tk),
        grid=(bsz, heads, seq // tq),
        in_specs=[pl.BlockSpec((None, None, tq, QK_PAD), lambda b, h, i: (b, h, i, 0)),
                  pl.BlockSpec((None, None, QK_PAD, ttot), lambda b, h, i: (b, h, 0, 0),
                               pipeline_mode=pl.Buffered(1)),
                  pl.BlockSpec((None, None, ttot, 2 * LANE), lambda b, h, i: (b, h, 0, 0),
                               pipeline_mode=pl.Buffered(1))],
        out_specs=pl.BlockSpec((None, tq, LANE), lambda b, h, i: (b, i, h)),
        out_shape=jax.ShapeDtypeStruct((bsz, seq, heads * LANE), BF16),
        scratch_shapes=[pltpu.VMEM((tq, tk), F32), pltpu.VMEM((tq, tk), F32),
                        pltpu.VMEM((tq, tk), BF16), pltpu.VMEM((tq, tk), BF16),
                        pltpu.VMEM((tq, 1), F32), pltpu.VMEM((tq, 1), F32),
                        pltpu.VMEM((tq, 2 * LANE), F32)],
        compiler_params=_params(("parallel", "parallel", "arbitrary")),
    )(q, k, v)


def _pack_halves(x):
    w = x.shape[1] // 2
    bits = lax.bitcast_convert_type(x.astype(BF16).astype(F32), U32)
    return jnp.bitwise_or(jnp.right_shift(bits[:, :w], jnp.uint32(16)), bits[:, w:])


def _unpack_halves(wd):
    lo = lax.bitcast_convert_type(jnp.left_shift(wd, jnp.uint32(16)), F32)
    hi = lax.bitcast_convert_type(jnp.bitwise_and(wd, jnp.uint32(0xFFFF0000)), F32)
    return lo, hi


def _ffn_front(xn, gf_ref, sh_ref, sc_ref, wrh_ref, wrl_ref, hp_ref, lg_ref):
    h2 = _rms(xn, gf_ref[...]) * (1.0 + sc_ref[...]) + sh_ref[...]
    hh = h2.astype(BF16)
    hl = (h2 - hh.astype(F32)).astype(BF16)
    hp_ref[...] = _pack_halves(h2)
    both = _dot(hh, wrl_ref[...])
    lg_ref[...] = both[:, 0:LANE] + both[:, LANE:] + _dot(hl, wrh_ref[...])


def _mix_out_kernel(of_ref, ob_ref, z_ref, ym_ref, x_ref, gog_ref, wout_ref, g1_ref,
                    gf_ref, sh_ref, sc_ref, wrh_ref, wrl_ref, xo_ref, hp_ref, lg_ref, mix):
    dv = GDN_DK
    o = of_ref[...] + ob_ref[...]
    for h in range(GDN_HEADS):
        hs = slice(h * dv, (h + 1) * dv)
        y = _rms(o[:, hs], gog_ref[...])
        mix[:, hs] = (y * _silu(z_ref[:, hs].astype(F32))).astype(BF16)
    hv = GDN_HEADS * dv
    mix[:, hv:] = ym_ref[...]
    xn = x_ref[...] + g1_ref[...] * _dot(mix[...], wout_ref[...])
    xo_ref[...] = xn
    _ffn_front(xn, gf_ref, sh_ref, sc_ref, wrh_ref, wrl_ref, hp_ref, lg_ref)


def _mix_out(o2, p_all, ymla, x, gog, wout, g1, gf, sh2, sc2, wrh, wrl, nct_tiles):
    bsz, seq, d = x.shape
    tm = SEQ_TILE
    hv = GDN_HEADS * GDN_DK
    vec = lambda: pl.BlockSpec((None, 1, d), lambda b, i: (b, 0, 0))
    full = lambda a: pl.BlockSpec(a.shape, lambda b, i: (0,) * a.ndim)
    return pl.pallas_call(
        _mix_out_kernel,
        grid=(bsz, seq // tm),
        in_specs=[pl.BlockSpec((None, None, tm, hv), lambda b, i: (0, b, i, 0)),
                  pl.BlockSpec((None, None, tm, hv), lambda b, i: (1, b, i, 0)),
                  pl.BlockSpec((None, tm, hv), lambda b, i: (b, i + nct_tiles, P_Z // hv)),
                  pl.BlockSpec((None, tm, hv), lambda b, i: (b, i, 0)),
                  pl.BlockSpec((None, tm, d), lambda b, i: (b, i, 0)),
                  full(gog), full(wout), vec(), full(gf), vec(), vec(), full(wrh), full(wrl)],
        out_specs=[pl.BlockSpec((None, tm, d), lambda b, i: (b, i, 0)),
                   pl.BlockSpec((None, tm, d // 2), lambda b, i: (b, i, 0)),
                   pl.BlockSpec((None, tm, LANE), lambda b, i: (b, i, 0))],
        out_shape=[jax.ShapeDtypeStruct((bsz, seq, d), F32),
                   jax.ShapeDtypeStruct((bsz, seq, d // 2), U32),
                   jax.ShapeDtypeStruct((bsz, seq, LANE), F32)],
        scratch_shapes=[pltpu.VMEM((tm, wout.shape[0]), BF16)],
        compiler_params=_params(("parallel", "parallel")),
    )(o2, o2, p_all, ymla, x, gog, wout, g1, gf, sh2, sc2, wrh, wrl)


POOL_HALO = 8


def _pool_kernel(seq, xm_ref, xp_ref, xn_ref, gm_ref, sh1_ref, sc1_ref, wp_ref, ps_ref, g1_ref,
                 gf_ref, sh_ref, sc_ref, wrh_ref, wrl_ref, xo_ref, hp_ref, lg_ref, ext):
    i = pl.program_id(1)
    n = pl.num_programs(1)
    tm = xm_ref.shape[0]
    hl = POOL_HALO

    def normed(ref):
        return _rms(ref[...], gm_ref[...]) * (1.0 + sc1_ref[...]) + sh1_ref[...]

    ext[hl:hl + tm, :] = normed(xm_ref)
    ext[0:hl, :] = jnp.where(i == 0, 0.0, normed(xp_ref))
    ext[hl + tm:2 * hl + tm, :] = jnp.where(i == n - 1, 0.0, normed(xn_ref))
    t = i * tm + lax.broadcasted_iota(I32, (tm, 1), 0)
    gw = xm_ref.shape[1] // len(POOL_WINDOWS)
    for gi, win in enumerate(POOL_WINDOWS):
        cs = slice(gi * gw, (gi + 1) * gw)
        half = win // 2
        acc = None
        for off in range(-half, win - half):
            term = ext[hl + off:hl + off + tm, cs]
            acc = term if acc is None else acc + term
        lo = jnp.clip(t - half, 0, seq)
        hi = jnp.clip(t - half + win, 0, seq)
        pooled = acc / (hi - lo).astype(F32) - ext[hl:hl + tm, cs]
        y = _dot(pooled.astype(BF16), wp_ref[gi]) * ps_ref[:, cs]
        xo_ref[:, cs] = xm_ref[:, cs] + g1_ref[:, cs] * y
    _ffn_front(xo_ref[...], gf_ref, sh_ref, sc_ref, wrh_ref, wrl_ref, hp_ref, lg_ref)


def _pool_mixer(x, gm, sh1, sc1, wp, ps, g1, gf, sh2, sc2, wrh, wrl):
    bsz, seq, d = x.shape
    tm = SEQ_TILE
    hl = POOL_HALO
    r = tm // hl
    nh = seq // hl
    vec = lambda: pl.BlockSpec((None, 1, d), lambda b, i: (b, 0, 0))
    full = lambda a: pl.BlockSpec(a.shape, lambda b, i: (0,) * a.ndim)
    return pl.pallas_call(
        functools.partial(_pool_kernel, seq),
        grid=(bsz, seq // tm),
        in_specs=[pl.BlockSpec((None, tm, d), lambda b, i: (b, i, 0)),
                  pl.BlockSpec((None, hl, d), lambda b, i: (b, jnp.maximum(i * r - 1, 0), 0)),
                  pl.BlockSpec((None, hl, d), lambda b, i: (b, jnp.minimum((i + 1) * r, nh - 1), 0)),
                  full(gm), vec(), vec(), full(wp), full(ps), vec(),
                  full(gf), vec(), vec(), full(wrh), full(wrl)],
        out_specs=[pl.BlockSpec((None, tm, d), lambda b, i: (b, i, 0)),
                   pl.BlockSpec((None, tm, d // 2), lambda b, i: (b, i, 0)),
                   pl.BlockSpec((None, tm, LANE), lambda b, i: (b, i, 0))],
        out_shape=[jax.ShapeDtypeStruct((bsz, seq, d), F32),
                   jax.ShapeDtypeStruct((bsz, seq, d // 2), U32),
                   jax.ShapeDtypeStruct((bsz, seq, LANE), F32)],
        scratch_shapes=[pltpu.VMEM((tm + 2 * hl, d), F32)],
        compiler_params=_params(("parallel", "parallel")),
    )(x, x, x, gm, sh1, sc1, wp, ps, g1, gf, sh2, sc2, wrh, wrl)


def _first_max(vals, idx, sentinel):
    m = jnp.max(vals, axis=0, keepdims=True)
    first = jnp.min(jnp.where(vals == m, idx, sentinel), axis=0, keepdims=True)
    return m, first


def _route_kernel(rows, lg_ref, bias_ref, dest_ref, gates_ref, cnt_ref, carry):
    phase = pl.program_id(0)
    step = pl.program_id(1)

    @pl.when(jnp.logical_and(phase == 0, step == 0))
    def _():
        carry[...] = jnp.zeros_like(carry)

    @pl.when(jnp.logical_and(phase == 1, step == 0))
    def _():
        shift = int(math.log2(rows))
        counts_row = carry[...].T.astype(I32)
        padded = jnp.left_shift(jnp.right_shift(counts_row + (rows - 1), shift), shift).astype(F32)
        i_exp = lax.broadcasted_iota(I32, padded.shape, 0)
        j_exp = lax.broadcasted_iota(I32, padded.shape, 1)
        first_row = jnp.sum(jnp.where(j_exp < i_exp, padded, 0.0), axis=1, keepdims=True)
        carry[...] = jnp.broadcast_to(first_row, carry.shape)

    tm = lg_ref.shape[0]
    epg = EXPERTS_PER_GROUP
    scores = jax.nn.sigmoid(lg_ref[...].T[0:N_EXPERTS, :])
    biased = scores + bias_ref[:, 0:1]
    eidx = lax.broadcasted_iota(I32, (epg, tm), 0)
    best_score = best_grp = best_i1 = best_i2 = None
    for g in range(N_GROUPS):
        blk = biased[g * epg:(g + 1) * epg, :]
        m1, i1 = _first_max(blk, eidx, epg)
        m2, i2 = _first_max(jnp.where(eidx == i1, -jnp.inf, blk), eidx, epg)
        gs = m1 + m2
        if g == 0:
            best_score, best_grp, best_i1, best_i2 = gs, jnp.zeros_like(i1), i1, i2
        else:
            better = gs > best_score
            best_score = jnp.where(better, gs, best_score)
            best_grp = jnp.where(better, g, best_grp)
            best_i1 = jnp.where(better, i1, best_i1)
            best_i2 = jnp.where(better, i2, best_i2)
    e1 = best_grp * epg + best_i1
    e2 = best_grp * epg + best_i2
    eall = lax.broadcasted_iota(I32, (LANE, tm), 0)
    oh1 = eall == e1
    oh2 = eall == e2
    oh = jnp.logical_or(oh1, oh2).astype(BF16)
    picked = jnp.sum(oh.astype(F32), axis=1, keepdims=True)

    @pl.when(phase == 0)
    def _():
        carry[...] = carry[...] + picked
        cnt_ref[...] = carry[...]

    @pl.when(phase == 1)
    def _():
        s1 = jnp.sum(jnp.where(oh1[0:N_EXPERTS], scores, 0.0), axis=0, keepdims=True)
        s2 = jnp.sum(jnp.where(oh2[0:N_EXPERTS], scores, 0.0), axis=0, keepdims=True)
        denom = s1 + s2
        earlier = (lax.broadcasted_iota(I32, (tm, tm), 0) < lax.broadcasted_iota(I32, (tm, tm), 1)).astype(BF16)
        row = _dot(oh, earlier) + carry[:, 0:1]
        d1 = jnp.sum(jnp.where(oh1, row, 0.0), axis=0, keepdims=True)
        d2 = jnp.sum(jnp.where(oh2, row, 0.0), axis=0, keepdims=True)
        carry[...] = carry[...] + picked
        dest_ref[...] = jnp.concatenate([d1.astype(I32), d2.astype(I32), jnp.zeros((6, tm), I32)], axis=0)
        gates = jnp.concatenate([s1 / denom, s2 / denom, jnp.zeros((LANE - 2, tm), F32)], axis=0)
        gates_ref[...] = gates.T


def _route(logits, bias_col):
    n = logits.shape[0]
    tm = ROUTE_TILE
    rows = MOE_ROWS
    assert rows & (rows - 1) == 0
    return pl.pallas_call(
        functools.partial(_route_kernel, rows),
        grid=(2, n // tm),
        in_specs=[pl.BlockSpec((tm, LANE), lambda p, t: (t, 0)),
                  pl.BlockSpec(bias_col.shape, lambda p, t: (0, 0))],
        out_specs=[pl.BlockSpec((8, tm), lambda p, t: (0, p * t)),
                   pl.BlockSpec((tm, LANE), lambda p, t: (p * t, 0)),
                   pl.BlockSpec((LANE, LANE), lambda p, t: (0, 0))],
        out_shape=[jax.ShapeDtypeStruct((8, n), I32),
                   jax.ShapeDtypeStruct((n, LANE), F32),
                   jax.ShapeDtypeStruct((LANE, LANE), F32)],
        scratch_shapes=[pltpu.VMEM((LANE, LANE), F32)],
        compiler_params=_params(("arbitrary", "arbitrary")),
    )(logits, bias_col)


def _dispatch_copy(hp_ref, xs_hbm, sem, j, dst):
    return pltpu.make_async_copy(hp_ref.at[pl.ds(j, 1), :], xs_hbm.at[pl.ds(dst, 1), :], sem)


def _zero_block_copy(zbuf, xs_hbm, zsem, start):
    return pltpu.make_async_copy(zbuf, xs_hbm.at[pl.ds(start, zbuf.shape[0]), :], zsem)


def _dispatch_kernel(tm, pad_end_ref, nu_ref, d0_ref, d1_ref, hp_ref, xs_hbm, zbuf, sem, zsem):
    rows = zbuf.shape[0]
    n_blocks = xs_hbm.shape[0] // rows

    @pl.when(pl.program_id(0) == 0)
    def _():
        zbuf[...] = jnp.zeros_like(zbuf)

        def expert_tail(e, carry):
            start = jnp.maximum(pad_end_ref[e] - rows, 0)
            _zero_block_copy(zbuf, xs_hbm, zsem, pl.multiple_of(start, rows)).start()
            return carry

        lax.fori_loop(0, N_EXPERTS, expert_tail, 0)

        def unused_block(b, carry):
            _zero_block_copy(zbuf, xs_hbm, zsem, pl.multiple_of(b * rows, rows)).start()
            return carry

        lax.fori_loop(nu_ref[0], n_blocks, unused_block, 0)

        def drain(b, carry):
            _zero_block_copy(zbuf, xs_hbm, zsem, 0).wait()
            return carry

        lax.fori_loop(0, N_EXPERTS + n_blocks - nu_ref[0], drain, 0)

    def issue(j, carry):
        _dispatch_copy(hp_ref, xs_hbm, sem, j, d0_ref[0, j]).start()
        _dispatch_copy(hp_ref, xs_hbm, sem, j, d1_ref[0, j]).start()
        return carry

    lax.fori_loop(0, tm, issue, 0, unroll=DMA_UNROLL)
    for _ in range(2):
        pltpu.make_async_copy(hp_ref, xs_hbm.at[pl.ds(0, tm), :], sem).wait()


def _dispatch(hp, dest, pad_end, n_used, n_rows):
    n, half = hp.shape
    tm = ROUTE_TILE
    nt = n // tm
    idx = lambda k: pl.BlockSpec((None, None, 1, tm), lambda i, pe, nu: (k, i, 0, 0), memory_space=pltpu.SMEM)
    dest4 = dest.reshape(dest.shape[0], nt, 1, tm)
    grid_spec = pltpu.PrefetchScalarGridSpec(
        num_scalar_prefetch=2,
        grid=(nt,),
        in_specs=[idx(0), idx(1), pl.BlockSpec((tm, half), lambda i, pe, nu: (i, 0))],
        out_specs=pl.BlockSpec(memory_space=pl.ANY),
        scratch_shapes=[pltpu.VMEM((MOE_ROWS, half), U32), pltpu.SemaphoreType.DMA(()),
                        pltpu.SemaphoreType.DMA(())],
    )
    return pl.pallas_call(
        functools.partial(_dispatch_kernel, tm),
        grid_spec=grid_spec,
        out_shape=jax.ShapeDtypeStruct((n_rows, half), U32),
        compiler_params=_params(("arbitrary",)),
    )(pad_end, n_used, dest4, dest4, hp)


def _experts_kernel(be_ref, nu_ref, xs_ref, wgu_ref, wd_ref, y_ref, wgu16, wd16):
    i = pl.program_id(0)

    @pl.when(i < nu_ref[0])
    def _():
        @pl.when(jnp.logical_or(i == 0, be_ref[i] != be_ref[jnp.maximum(i - 1, 0)]))
        def _():
            wgu16[...] = wgu_ref[...].astype(BF16)
            wd16[...] = wd_ref[...].astype(BF16)

        lo, hi = _unpack_halves(xs_ref[...])
        half = lo.shape[1]
        gu = _dot(lo.astype(BF16), wgu16[0:half, :]) + _dot(hi.astype(BF16), wgu16[half:, :])
        f = gu.shape[1] // 2
        hid = _silu(gu[:, :f]) * gu[:, f:]
        y_ref[...] = _pack_halves(_dot(hid.astype(BF16), wd16[...]))

    @pl.when(i >= nu_ref[0])
    def _():
        y_ref[...] = jnp.zeros_like(y_ref)


def _experts(layer, block_expert, n_used, xs, w_gate_up, w_down):
    n_rows, half = xs.shape
    rows = MOE_ROWS
    d = 2 * half
    f2 = w_gate_up.shape[3]
    grid_spec = pltpu.PrefetchScalarGridSpec(
        num_scalar_prefetch=2,
        grid=(n_rows // rows,),
        in_specs=[pl.BlockSpec((rows, half), lambda i, be, nu: (i, 0)),
                  pl.BlockSpec((None, None, d, f2), lambda i, be, nu: (layer, be[i], 0, 0)),
                  pl.BlockSpec((None, None, f2 // 2, d), lambda i, be, nu: (layer, be[i], 0, 0))],
        out_specs=pl.BlockSpec((rows, half), lambda i, be, nu: (i, 0)),
        scratch_shapes=[pltpu.VMEM((d, f2), BF16), pltpu.VMEM((f2 // 2, d), BF16)],
    )
    return pl.pallas_call(
        _experts_kernel,
        grid_spec=grid_spec,
        out_shape=jax.ShapeDtypeStruct((n_rows, half), U32),
        compiler_params=_params(("arbitrary",)),
    )(block_expert, n_used, xs, w_gate_up, w_down)


def _combine_copy(y_hbm, buf, sem, slot, k, src, j):
    return pltpu.make_async_copy(y_hbm.at[pl.ds(src, 1), :], buf.at[slot, k, pl.ds(j, 1), :], sem.at[slot])


def _combine_kernel(tm, d0_ref, d1_ref, d0n_ref, d1n_ref, y_hbm, gates_ref, x_ref, g2_ref, xo_ref, buf, sem):
    step = pl.program_id(0) * pl.num_programs(1) + pl.program_id(1)
    total = pl.num_programs(0) * pl.num_programs(1)
    slot = step % 2

    def start_gather(r0_ref, r1_ref, s):
        def issue(j, carry):
            _combine_copy(y_hbm, buf, sem, s, 0, r0_ref[0, j], j).start()
            _combine_copy(y_hbm, buf, sem, s, 1, r1_ref[0, j], j).start()
            return carry
        lax.fori_loop(0, tm, issue, 0, unroll=DMA_UNROLL)

    @pl.when(step == 0)
    def _():
        start_gather(d0_ref, d1_ref, 0)

    @pl.when(step + 1 < total)
    def _():
        start_gather(d0n_ref, d1n_ref, 1 - slot)

    for k in range(2):
        pltpu.make_async_copy(y_hbm.at[pl.ds(0, tm), :], buf.at[slot, k], sem.at[slot]).wait()
    lo0, hi0 = _unpack_halves(buf[slot, 0])
    lo1, hi1 = _unpack_halves(buf[slot, 1])
    half = lo0.shape[1]
    ga = gates_ref[:, 0:1]
    gb = gates_ref[:, 1:2]
    xo_ref[:, 0:half] = x_ref[:, 0:half] + g2_ref[:, 0:half] * (ga * lo0 + gb * lo1)
    xo_ref[:, half:] = x_ref[:, half:] + g2_ref[:, half:] * (ga * hi0 + gb * hi1)


def _combine(y, dest, gates, x, g2):
    bsz, seq, d = x.shape
    tm = SEQ_TILE
    nt = seq // tm
    half = d // 2
    last = bsz * nt - 1
    idx = lambda k, ahead: pl.BlockSpec(
        (None, None, 1, tm), lambda b, i: (k, jnp.minimum(b * nt + i + ahead, last), 0, 0),
        memory_space=pltpu.SMEM)
    dest4 = dest.reshape(dest.shape[0], bsz * nt, 1, tm)
    return pl.pallas_call(
        functools.partial(_combine_kernel, tm),
        grid=(bsz, nt),
        in_specs=[idx(0, 0), idx(1, 0), idx(0, 1), idx(1, 1),
                  pl.BlockSpec(memory_space=pl.ANY),
                  pl.BlockSpec((tm, LANE), lambda b, i: (b * nt + i, 0)),
                  pl.BlockSpec((None, tm, d), lambda b, i: (b, i, 0)),
                  pl.BlockSpec((None, 1, d), lambda b, i: (b, 0, 0))],
        out_specs=pl.BlockSpec((None, tm, d), lambda b, i: (b, i, 0)),
        out_shape=jax.ShapeDtypeStruct((bsz, seq, d), F32),
        scratch_shapes=[pltpu.VMEM((2, 2, tm, half), U32), pltpu.SemaphoreType.DMA((2,))],
        compiler_params=_params(("arbitrary", "arbitrary")),
    )(dest4, dest4, dest4, dest4, y, gates, x, g2)


def _moe(layer, x, hp, logits, g2, bias_col, w_gate_up, w_down):
    bsz, seq, d = x.shape
    n = bsz * seq
    rows = MOE_ROWS
    dest, gates, cnt = _route(logits.reshape(n, LANE), bias_col)
    counts = cnt[:N_EXPERTS, 0].astype(I32)
    pad_end = jnp.cumsum((counts + rows - 1) // rows * rows)
    n_rows = -(-(n * TOP_K + N_EXPERTS * (rows - 1)) // rows) * rows
    block_start = jnp.arange(n_rows // rows, dtype=I32) * rows
    block_expert = jnp.minimum(jnp.sum((pad_end[None, :] <= block_start[:, None]).astype(I32), axis=1),
                               N_EXPERTS - 1)
    n_used = pad_end[-1:] // rows
    xs = _dispatch(hp.reshape(n, d // 2), dest, pad_end, n_used, n_rows)
    y = _experts(layer, block_expert, n_used, xs, w_gate_up, w_down)
    return _combine(y, dest, gates, x, g2)


def _permute_w_in(w_in):
    d = w_in.shape[0]
    hq = GDN_HEADS * GDN_DK
    off_a = 4 * hq
    off_cq = off_a + 4 * GDN_HEADS
    off_ckv = off_cq + Q_LORA
    off_kr = off_ckv + KV_LORA
    z = lambda n: jnp.zeros((d, n), w_in.dtype)
    a = lambda k: w_in[:, off_a + k * GDN_HEADS:off_a + (k + 1) * GDN_HEADS]
    pad_ab = LANE - 2 * GDN_HEADS
    cols = [w_in[:, :off_a], w_in[:, off_cq:off_kr + ROPE_DIM], z(LANE - ROPE_DIM),
            a(0), a(2), z(pad_ab), a(1), a(3), z(pad_ab)]
    w = jnp.concatenate(cols, axis=1)
    return jnp.concatenate([w, z(P_WIDTH - w.shape[1])], axis=1).astype(BF16)


def _rope_tables(rows, n_ctx):
    row = jnp.repeat(jnp.arange(rows, dtype=F32), GRID_W)
    col = jnp.tile(jnp.arange(GRID_W, dtype=F32), rows)
    pairs = ROPE_DIM // 4
    inv_freq = ROPE_THETA ** (-jnp.arange(pairs, dtype=F32) / pairs)
    ang_r = row[:, None] * inv_freq
    ang_c = col[:, None] * inv_freq
    ang = jnp.concatenate([ang_r, ang_r, ang_c, ang_c], axis=-1)
    ang = jnp.concatenate([jnp.zeros((n_ctx, ROPE_DIM), F32), ang], axis=0)
    cos, sin = jnp.cos(ang), jnp.sin(ang)
    return jnp.concatenate([cos, cos], axis=1), jnp.concatenate([sin, sin], axis=1)


def kernel(x, c, ctx, c_ctx, w_mod, b_mod, norm_mix_g, norm_ffn_g, w_in, conv_qkv, a_log_fwd, a_log_bwd, dt_bias_fwd, dt_bias_bwd, gdn_out_g, q_a_norm_g, w_uq, kv_a_norm_g, w_ukv, q_norm_g, k_norm_g, w_out, w_pool, pool_scale, w_router, router_bias, w_gate_up, w_down):
    bsz, seq, d = x.shape
    n_ctx = ctx.shape[1]
    depth = w_mod.shape[0]
    assert depth == 2, "the context stream is only read: no layer after the first even layer reads it"
    nct_tiles = n_ctx // SEQ_TILE

    cc = jnp.concatenate([c, c_ctx[None, :], jnp.zeros((8 - bsz - 1, d), F32)], axis=0)
    mod = _modulation(cc, w_mod, b_mod).reshape(depth, 8, 6, d)

    def mods(layer):
        m = mod[layer]
        return [m[:bsz, k][:, None, :] for k in range(6)], [m[bsz:bsz + 1, k] for k in range(6)]

    wr = jnp.pad(w_router, ((0, 0), (0, LANE - N_EXPERTS)))
    wrh = wr.astype(BF16)
    wrl = jnp.concatenate([wrh, (wr - wrh.astype(F32)).astype(BF16)], axis=1)
    bias_col = jnp.broadcast_to(router_bias.astype(F32)[:, None], (N_EXPERTS, LANE))
    row = lambda v: v.astype(F32)[None, :]

    for layer in range(depth):
        j = layer // 2
        (sh1, sc1, g1, sh2, sc2, g2), (csh1, csc1, _, _, _, _) = mods(layer)
        gm = row(norm_mix_g[layer])
        gf = row(norm_ffn_g[layer])
        if layer % 2 == 0:
            p_all = _in_projection(ctx, x, gm, csh1, csc1, sh1, sc1, _permute_w_in(w_in[j]))
            conv_w = conv_qkv[j].astype(F32)
            gate_params = jnp.zeros((2, 8, LANE), F32)
            gate_params = gate_params.at[0, 0, :GDN_HEADS].set(a_log_fwd[j]).at[0, 1, :GDN_HEADS].set(dt_bias_fwd[j])
            gate_params = gate_params.at[1, 0, :GDN_HEADS].set(a_log_bwd[j]).at[1, 1, :GDN_HEADS].set(dt_bias_bwd[j])
            qkv_all, gb_all = _gdn_front(p_all, conv_w, gate_params, nct_tiles)
            o2 = _gdn_scan(qkv_all, gb_all, n_ctx)

            cos, sin = _rope_tables(seq // GRID_W, n_ctx)
            gq = q_norm_g[j].astype(F32)
            gk = k_norm_g[j].astype(F32)
            wq = w_uq[j].reshape(Q_LORA, MLA_HEADS, QK_HEAD)
            wq = jnp.concatenate([wq[:, :, :NOPE_DIM].reshape(Q_LORA, -1),
                                  wq[:, :, NOPE_DIM:].reshape(Q_LORA, -1)], axis=1).astype(BF16)
            q = _mla_queries(p_all, row(q_a_norm_g[j]), wq, gq[None, :NOPE_DIM],
                             jnp.tile(gq[NOPE_DIM:], 2)[None, :], cos, sin, nct_tiles, seq)
            gk_rope = jnp.concatenate([gk[NOPE_DIM:], jnp.zeros((LANE - ROPE_DIM,), F32)])[None, :]
            k_all, v_all = _mla_keys_values(p_all, row(kv_a_norm_g[j]), w_ukv[j].astype(BF16),
                                            gk[None, :NOPE_DIM], gk_rope, cos, sin)
            ymla = _attention(q, k_all, v_all)
            x, hp, logits = _mix_out(o2, p_all, ymla, x, row(gdn_out_g[j]), w_out[j].astype(BF16), g1,
                                     gf, sh2, sc2, wrh, wrl, nct_tiles)
        else:
            x, hp, logits = _pool_mixer(x, gm, sh1, sc1, w_pool[j].astype(BF16), row(pool_scale[j]), g1,
                                        gf, sh2, sc2, wrh, wrl)
        x = _moe(layer, x, hp, logits, g2, bias_col, w_gate_up, w_down)
    return x
```

```python
import functools
import math

import jax
import jax.numpy as jnp
from jax import lax
from jax.experimental import pallas as pl
from jax.experimental.pallas import tpu as pltpu

F32 = jnp.float32
BF16 = jnp.bfloat16
U32 = jnp.uint32
I32 = jnp.int32

EPS = 1e-6
GRID_W = 64
GDN_HEADS = 8
GDN_DK = 128
MLA_HEADS = 8
NOPE_DIM = 128
ROPE_DIM = 64
QK_HEAD = NOPE_DIM + ROPE_DIM
QK_PAD = 256
ROPE_THETA = 10000.0
Q_LORA = 512
KV_LORA = 256
POOL_WINDOWS = (2, 4, 8, 16)
N_EXPERTS = 64
EXPERTS_PER_GROUP = 8
N_GROUPS = N_EXPERTS // EXPERTS_PER_GROUP
TOP_K = 2

LANE = 128
V7X_VMEM_LIMIT = 56 * 1024 * 1024

SEQ_TILE = 256
GDN_CHUNK = 128
MOE_ROWS = 256
DMA_UNROLL = 8
ROUTE_TILE = 512

P_QKV = 0
P_Z = 3072
P_CQ = 4096
P_CKV = 4608
P_KR = 4864
P_AB = 4992
P_WIDTH = 5376
P_NTILE = 1792


def _params(sem):
    return pltpu.CompilerParams(dimension_semantics=sem, vmem_limit_bytes=V7X_VMEM_LIMIT)


def _dot(a, b):
    return jnp.dot(a, b, preferred_element_type=F32)


def _dot_nt(a, b):
    return lax.dot_general(a, b, (((1,), (1,)), ((), ())), preferred_element_type=F32)


def _dot_tn(a, b):
    return lax.dot_general(a, b, (((0,), (0,)), ((), ())), preferred_element_type=F32)


def _silu(x):
    return x * jax.nn.sigmoid(x)


def _rms(x, g):
    return x * lax.rsqrt(jnp.mean(x * x, axis=-1, keepdims=True) + EPS) * g


def _split3(x):
    hi = x.astype(BF16)
    r = x - hi.astype(F32)
    mid = r.astype(BF16)
    lo = (r - mid.astype(F32)).astype(BF16)
    return hi, mid, lo


def _mod_kernel(c_ref, w_ref, b_ref, o_ref):
    s = _silu(c_ref[...]).astype(BF16)
    o_ref[...] = _dot(s, w_ref[...].astype(BF16)) + b_ref[...]


def _modulation(cc, w_mod, b_mod):
    depth, d, n6 = w_mod.shape
    tn = 1024
    return pl.pallas_call(
        _mod_kernel,
        grid=(depth, n6 // tn),
        in_specs=[pl.BlockSpec((8, d), lambda l, j: (0, 0)),
                  pl.BlockSpec((None, d, tn), lambda l, j: (l, 0, j)),
                  pl.BlockSpec((None, 1, tn), lambda l, j: (l, 0, j))],
        out_specs=pl.BlockSpec((None, 8, tn), lambda l, j: (l, 0, j)),
        out_shape=jax.ShapeDtypeStruct((depth, 8, n6), F32),
        compiler_params=_params(("parallel", "parallel")),
    )(cc, w_mod, b_mod.reshape(depth, 1, n6))


def _inproj_kernel(nct, xc_ref, x_ref, g_ref, shc_ref, scc_ref, sh_ref, sc_ref, w_ref, o_ref):
    i = pl.program_id(2)

    def run(xr, shr, scr):
        h = _rms(xr[...], g_ref[...]) * (1.0 + scr[...]) + shr[...]
        o_ref[...] = _dot(h.astype(BF16), w_ref[...]).astype(o_ref.dtype)

    @pl.when(i < nct)
    def _():
        run(xc_ref, shc_ref, scc_ref)

    @pl.when(i >= nct)
    def _():
        run(x_ref, sh_ref, sc_ref)


def _in_projection(ctx, x, g, shc, scc, sh, sc, w):
    bsz, seq, d = x.shape
    tm = SEQ_TILE
    nct = ctx.shape[1] // tm
    nt = nct + seq // tm
    width = w.shape[1]
    tn = P_NTILE
    return pl.pallas_call(
        functools.partial(_inproj_kernel, nct),
        grid=(width // tn, bsz, nt),
        in_specs=[pl.BlockSpec((None, tm, d), lambda j, b, i: (b, jnp.minimum(i, nct - 1), 0)),
                  pl.BlockSpec((None, tm, d), lambda j, b, i: (b, jnp.maximum(i - nct, 0), 0)),
                  pl.BlockSpec((1, d), lambda j, b, i: (0, 0)),
                  pl.BlockSpec((1, d), lambda j, b, i: (0, 0)),
                  pl.BlockSpec((1, d), lambda j, b, i: (0, 0)),
                  pl.BlockSpec((None, 1, d), lambda j, b, i: (b, 0, 0)),
                  pl.BlockSpec((None, 1, d), lambda j, b, i: (b, 0, 0)),
                  pl.BlockSpec((d, tn), lambda j, b, i: (0, j))],
        out_specs=pl.BlockSpec((None, tm, tn), lambda j, b, i: (b, i, j)),
        out_shape=jax.ShapeDtypeStruct((bsz, nt * tm, width), BF16),
        compiler_params=_params(("parallel", "parallel", "parallel")),
    )(ctx, x, g, shc, scc, sh, sc, w)


GDN_HALO = 16


def _gdn_front_kernel(nct, pm_ref, pp_ref, pn_ref, cw_ref, ab0_ref, ab1_ref, gp_ref,
                      qkv_ref, gb_ref, ext):
    i = pl.program_id(1)
    n = pl.num_programs(1)
    tm = pm_ref.shape[0]
    hl = GDN_HALO
    zero_prev = jnp.logical_or(i == 0, i == nct)
    zero_next = jnp.logical_or(i == nct - 1, i == n - 1)
    ext[hl:hl + tm, :] = pm_ref[...].astype(F32)
    ext[0:hl, :] = jnp.where(zero_prev, 0.0, pp_ref[...].astype(F32))
    ext[hl + tm:2 * hl + tm, :] = jnp.where(zero_next, 0.0, pn_ref[...].astype(F32))
    taps = cw_ref.shape[0]
    nqk = 2 * GDN_HEADS
    for cb in range(qkv_ref.shape[1] // LANE):
        cs = slice(cb * LANE, (cb + 1) * LANE)
        acc = None
        for j in range(taps):
            off = hl - taps // 2 + j
            term = cw_ref[j:j + 1, cs] * ext[off:off + tm, cs]
            acc = term if acc is None else acc + term
        y = _silu(acc)
        if cb < nqk:
            inv = lax.rsqrt(jnp.sum(y * y, axis=-1, keepdims=True) + EPS)
            if cb < GDN_HEADS:
                inv = inv * (GDN_DK ** -0.5)
            y = y * inv
        qkv_ref[:, cs] = y.astype(BF16)
    for d, ab_ref in enumerate((ab0_ref, ab1_ref)):
        a = ab_ref[...].astype(F32)
        lane = lax.broadcasted_iota(I32, a.shape, 1)
        neg_decay = -jnp.exp(gp_ref[d, 0:1, :])
        xx = a + gp_ref[d, 1:2, :]
        softplus = jnp.maximum(xx, 0.0) + jnp.log1p(jnp.exp(-jnp.abs(xx)))
        gate = jnp.where(lane < GDN_HEADS, neg_decay * softplus, jax.nn.sigmoid(a))
        gb_ref[d] = jnp.where(lane < 2 * GDN_HEADS, gate, 0.0)


def _gdn_front(p_all, conv_w, gate_params, nct):
    bsz, ttot, _ = p_all.shape
    tm = SEQ_TILE
    nt = ttot // tm
    cq = P_Z
    hl = GDN_HALO
    r = tm // hl
    nh = ttot // hl
    return pl.pallas_call(
        functools.partial(_gdn_front_kernel, nct),
        grid=(bsz, nt),
        in_specs=[pl.BlockSpec((None, tm, cq), lambda b, i: (b, i, 0)),
                  pl.BlockSpec((None, hl, cq), lambda b, i: (b, jnp.maximum(i * r - 1, 0), 0)),
                  pl.BlockSpec((None, hl, cq), lambda b, i: (b, jnp.minimum((i + 1) * r, nh - 1), 0)),
                  pl.BlockSpec(conv_w.shape, lambda b, i: (0, 0)),
                  pl.BlockSpec((None, tm, LANE), lambda b, i: (b, i, P_AB // LANE)),
                  pl.BlockSpec((None, tm, LANE), lambda b, i: (b, i, P_AB // LANE + 1)),
                  pl.BlockSpec(gate_params.shape, lambda b, i: (0, 0, 0))],
        out_specs=[pl.BlockSpec((None, tm, cq), lambda b, i: (b, i, 0)),
                   pl.BlockSpec((2, None, tm, LANE), lambda b, i: (0, b, i, 0))],
        out_shape=[jax.ShapeDtypeStruct((bsz, ttot, cq), BF16),
                   jax.ShapeDtypeStruct((2, bsz, ttot, LANE), F32)],
        scratch_shapes=[pltpu.VMEM((tm + 2 * hl, cq), F32)],
        compiler_params=_params(("parallel", "parallel")),
    )(p_all, p_all, p_all, conv_w, p_all, p_all, gate_params)


def _gdn_scan_kernel(q_ref, k_ref, v_ref, gb_ref, o_ref, s_scr):
    d = pl.program_id(1)
    s = pl.program_id(2)
    c = GDN_CHUNK
    n_sub = q_ref.shape[0] // c
    dk = GDN_DK

    @pl.when(s == 0)
    def _():
        s_scr[...] = jnp.zeros_like(s_scr)

    ri = lax.broadcasted_iota(I32, (c, c), 0)
    ci = lax.broadcasted_iota(I32, (c, c), 1)
    incl = jnp.where(d == 0, ri - ci, ci - ri) >= 0
    strict = jnp.logical_and(incl, ci != ri)
    m_incl = incl.astype(BF16)
    ones = jnp.ones((c, c), BF16)
    eye = (ri == ci).astype(F32)
    pair_masks = []
    for lb in range(int(math.log2(c))):
        same_pair = jnp.right_shift(ri, lb + 1) == jnp.right_shift(ci, lb + 1)
        same_block = jnp.right_shift(ri, lb) == jnp.right_shift(ci, lb)
        pair_masks.append(jnp.logical_and(same_pair, jnp.logical_not(same_block)))
    heads = range(GDN_HEADS)
    col = lambda a, h: a[:, h:h + 1]

    rows = [pl.ds(pl.multiple_of(jnp.where(d == 0, sub, n_sub - 1 - sub) * c, c), c) for sub in range(n_sub)]
    subs = range(n_sub)
    pairs = [(sub, h) for sub in subs for h in heads]
    hcols = lambda ref, sub, h: ref[rows[sub], h * dk:(h + 1) * dk]
    gb = [gb_ref[rows[sub], :] for sub in subs]
    lane = lax.broadcasted_iota(I32, gb[0].shape, 1)
    gsplit = [_split3(jnp.where(lane < GDN_HEADS, gb[sub], 0.0)) for sub in subs]
    gc = [sum(_dot(m_incl, t) for t in gsplit[sub]) for sub in subs]
    gl = [sum(_dot(ones, t) for t in gsplit[sub]) for sub in subs]
    gc_t = [gc[sub].T for sub in subs]
    e_gc = [jnp.exp(gc[sub]) for sub in subs]
    e_rest = [jnp.exp(gl[sub] - gc[sub]) for sub in subs]
    e_all = [jnp.exp(gl[sub]) for sub in subs]
    qb = [hcols(q_ref, sub, h) for sub, h in pairs]
    kb16 = [hcols(k_ref, sub, h) for sub, h in pairs]
    k = [a.astype(F32) for a in kb16]
    beta = [col(gb[sub], GDN_HEADS + h) for sub, h in pairs]
    npair = range(len(pairs))
    kbeta = [k[i] * beta[i] for i in npair]
    kk = [_dot_nt(kbeta[i].astype(BF16), kb16[i]) for i in npair]
    qk = [_dot_nt(qb[i], kb16[i]) for i in npair]
    dec = [jnp.where(incl, jnp.exp(jnp.where(incl, col(gc[sub], h) - gc_t[sub][h:h + 1, :], 0.0)), 0.0)
           for sub, h in pairs]
    low = [jnp.where(strict, kk[i] * dec[i], 0.0) for i in npair]
    r = [-jnp.where(pair_masks[0], low[i], 0.0) for i in npair]
    for mask in pair_masks[1:]:
        dmat = [(r[i] + eye).astype(BF16) for i in npair]
        cd = [_dot(jnp.where(mask, low[i], 0.0).astype(BF16), dmat[i]) for i in npair]
        r = [r[i] - _dot(dmat[i], cd[i].astype(BF16)) for i in npair]
    rhs = [jnp.concatenate([hcols(v_ref, sub, h).astype(F32) * beta[i], kbeta[i] * col(e_gc[sub], h)], axis=1)
           for i, (sub, h) in enumerate(pairs)]
    uw16 = [(rhs[i] + _dot(r[i].astype(BF16), rhs[i].astype(BF16))).astype(BF16) for i in npair]
    kt = [_dot_tn((k[i] * col(e_rest[sub], h)).astype(BF16), uw16[i])
          for i, (sub, h) in enumerate(pairs)]
    qu = [_dot((qk[i] * dec[i]).astype(BF16), uw16[i]) for i in npair]
    qeff = [(qb[i].astype(F32) * col(e_gc[sub], h) - qu[i][:, dk:]).astype(BF16)
            for i, (sub, h) in enumerate(pairs)]
    st = [s_scr[h] for h in heads]
    for sub in subs:
        st16 = [a.astype(BF16) for a in st]
        for h in heads:
            i = sub * GDN_HEADS + h
            o_ref[rows[sub], h * dk:(h + 1) * dk] = _dot(qeff[i], st16[h]) + qu[i][:, :dk]
        st = [st[h] * e_all[sub][0:1, h:h + 1] + kt[sub * GDN_HEADS + h][:, :dk]
              - _dot(kt[sub * GDN_HEADS + h][:, dk:].astype(BF16), st16[h]) for h in heads]
    for h in heads:
        s_scr[h] = st[h]


GDN_CHUNKS_PER_STEP = 2


def _gdn_scan(qkv_all, gb_all, nct_tokens):
    bsz, ttot, _ = qkv_all.shape
    c = GDN_CHUNK * GDN_CHUNKS_PER_STEP
    assert nct_tokens % c == 0 and ttot % c == 0
    nct = nct_tokens // c
    ns = ttot // c
    nlat = ns - nct
    hv = GDN_HEADS * GDN_DK

    def tmap(d, s):
        rev = jnp.where(s < nct, nct - 1 - s, 2 * nct + nlat - 1 - s)
        return jnp.where(d == 0, s, rev)

    def omap(d, s):
        first = jnp.where(d == 0, 0, nlat - 1)
        return jnp.where(s < nct, first, tmap(d, s) - nct)

    return pl.pallas_call(
        _gdn_scan_kernel,
        grid=(bsz, 2, ns),
        in_specs=[pl.BlockSpec((None, c, hv), lambda b, d, s: (b, tmap(d, s), 0)),
                  pl.BlockSpec((None, c, hv), lambda b, d, s: (b, tmap(d, s), 1)),
                  pl.BlockSpec((None, c, hv), lambda b, d, s: (b, tmap(d, s), 2)),
                  pl.BlockSpec((None, None, c, LANE), lambda b, d, s: (d, b, tmap(d, s), 0))],
        out_specs=pl.BlockSpec((None, None, c, hv), lambda b, d, s: (d, b, omap(d, s), 0)),
        out_shape=jax.ShapeDtypeStruct((2, bsz, nlat * c, hv), F32),
        scratch_shapes=[pltpu.VMEM((GDN_HEADS, GDN_DK, GDN_DK), F32)],
        compiler_params=_params(("parallel", "parallel", "arbitrary")),
    )(qkv_all, qkv_all, qkv_all, gb_all)


def _rope_tile(xr, cos, sin):
    lane = lax.broadcasted_iota(I32, xr.shape, 1)
    first_half = (lane % 32) < 16
    rot = jnp.where(first_half, -pltpu.roll(xr, LANE - 16, 1), pltpu.roll(xr, 16, 1))
    return xr * cos + rot * sin


def _mla_q_kernel(c_ref, ga_ref, w_ref, gn_ref, gr_ref, cos_ref, sin_ref, q_ref):
    cn = _rms(c_ref[...].astype(F32), ga_ref[...]).astype(BF16)
    qf = _dot(cn, w_ref[...])
    tm = qf.shape[0]
    lane = lax.broadcasted_iota(I32, (tm, LANE), 1)
    left = lane < ROPE_DIM
    rope_base = MLA_HEADS * NOPE_DIM
    scale = QK_HEAD ** -0.5 * math.log2(math.e)
    cos = cos_ref[...]
    sin = sin_ref[...]
    for hp in range(MLA_HEADS // 2):
        rt = qf[:, rope_base + hp * LANE:rope_base + (hp + 1) * LANE]
        rsq = rt * rt
        ss_left = jnp.sum(jnp.where(left, rsq, 0.0), axis=-1, keepdims=True)
        ss_right = jnp.sum(jnp.where(left, 0.0, rsq), axis=-1, keepdims=True)
        invs = []
        for par, ss_r in ((0, ss_left), (1, ss_right)):
            h = 2 * hp + par
            nope = qf[:, h * NOPE_DIM:(h + 1) * NOPE_DIM]
            ss = jnp.sum(nope * nope, axis=-1, keepdims=True) + ss_r
            inv = lax.rsqrt(ss * (1.0 / QK_HEAD) + EPS)
            invs.append(inv)
            q_ref[h, :, 0:NOPE_DIM] = (nope * inv * gn_ref[...] * scale).astype(BF16)
        inv_lane = jnp.where(left, invs[0], invs[1])
        xr = _rope_tile(rt * inv_lane * gr_ref[...], cos, sin) * scale
        q_ref[2 * hp, :, NOPE_DIM:QK_PAD] = jnp.where(left, xr, 0.0).astype(BF16)
        q_ref[2 * hp + 1, :, NOPE_DIM:QK_PAD] = jnp.where(left, pltpu.roll(xr, ROPE_DIM, 1), 0.0).astype(BF16)


def _mla_queries(p_all, ga, w, gn, gr, cos, sin, nct_tiles, seq):
    bsz = p_all.shape[0]
    tm = SEQ_TILE
    return pl.pallas_call(
        _mla_q_kernel,
        grid=(bsz, seq // tm),
        in_specs=[pl.BlockSpec((None, tm, Q_LORA), lambda b, i: (b, i + nct_tiles, P_CQ // Q_LORA)),
                  pl.BlockSpec(ga.shape, lambda b, i: (0, 0)),
                  pl.BlockSpec(w.shape, lambda b, i: (0, 0)),
                  pl.BlockSpec(gn.shape, lambda b, i: (0, 0)),
                  pl.BlockSpec(gr.shape, lambda b, i: (0, 0)),
                  pl.BlockSpec((tm, LANE), lambda b, i: (i + nct_tiles, 0)),
                  pl.BlockSpec((tm, LANE), lambda b, i: (i + nct_tiles, 0))],
        out_specs=pl.BlockSpec((None, MLA_HEADS, tm, QK_PAD), lambda b, i: (b, 0, i, 0)),
        out_shape=jax.ShapeDtypeStruct((bsz, MLA_HEADS, seq, QK_PAD), BF16),
        compiler_params=_params(("parallel", "parallel")),
    )(p_all, ga, w, gn, gr, cos, sin)


def _mla_kv_kernel(c_ref, kr_ref, ga_ref, w_ref, gn_ref, gr_ref, cos_ref, sin_ref, k_ref, v_ref):
    cn = _rms(c_ref[...].astype(F32), ga_ref[...]).astype(BF16)
    kv = _dot(cn, w_ref[...])
    kr = kr_ref[...].astype(F32)
    ss_r = jnp.sum(kr * kr, axis=-1, keepdims=True)
    kr_rot = _rope_tile(kr * gr_ref[...], cos_ref[...], sin_ref[...])
    width = NOPE_DIM + LANE
    ones_col = (lax.broadcasted_iota(I32, kr.shape, 1) == 0).astype(BF16)
    for h in range(MLA_HEADS):
        nope = kv[:, h * width:h * width + NOPE_DIM]
        ss = jnp.sum(nope * nope, axis=-1, keepdims=True) + ss_r
        inv = lax.rsqrt(ss * (1.0 / QK_HEAD) + EPS)
        k_ref[h, 0:NOPE_DIM, :] = (nope * inv * gn_ref[...]).T.astype(BF16)
        k_ref[h, NOPE_DIM:QK_PAD, :] = (kr_rot * inv).T.astype(BF16)
        v_ref[h, :, 0:LANE] = kv[:, h * width + NOPE_DIM:(h + 1) * width].astype(BF16)
        v_ref[h, :, LANE:2 * LANE] = ones_col


def _mla_keys_values(p_all, ga, w, gn, gr, cos, sin):
    bsz, ttot, _ = p_all.shape
    tm = SEQ_TILE
    return pl.pallas_call(
        _mla_kv_kernel,
        grid=(bsz, ttot // tm),
        in_specs=[pl.BlockSpec((None, tm, KV_LORA), lambda b, i: (b, i, P_CKV // KV_LORA)),
                  pl.BlockSpec((None, tm, LANE), lambda b, i: (b, i, P_KR // LANE)),
                  pl.BlockSpec(ga.shape, lambda b, i: (0, 0)),
                  pl.BlockSpec(w.shape, lambda b, i: (0, 0)),
                  pl.BlockSpec(gn.shape, lambda b, i: (0, 0)),
                  pl.BlockSpec(gr.shape, lambda b, i: (0, 0)),
                  pl.BlockSpec((tm, LANE), lambda b, i: (i, 0)),
                  pl.BlockSpec((tm, LANE), lambda b, i: (i, 0))],
        out_specs=[pl.BlockSpec((None, MLA_HEADS, QK_PAD, tm), lambda b, i: (b, 0, 0, i)),
                   pl.BlockSpec((None, MLA_HEADS, tm, 2 * LANE), lambda b, i: (b, 0, i, 0))],
        out_shape=[jax.ShapeDtypeStruct((bsz, MLA_HEADS, QK_PAD, ttot), BF16),
                   jax.ShapeDtypeStruct((bsz, MLA_HEADS, ttot, 2 * LANE), BF16)],
        compiler_params=_params(("parallel", "parallel")),
    )(p_all, p_all, ga, w, gn, gr, cos, sin)


FLASH_PARTS = 2
FLASH_ROWS = 64


def _flash_kernel(tk, q_ref, k_ref, v_ref, o_ref, s_a, s_b, p_a, p_b, m_scr, a_scr, acc_scr):
    nk = v_ref.shape[0] // tk
    tq = q_ref.shape[0]
    q = q_ref[...]
    m_scr[...] = jnp.full_like(m_scr, -jnp.inf)
    acc_scr[...] = jnp.zeros_like(acc_scr)

    def scores(j):
        return _dot(q, k_ref[:, pl.ds(pl.multiple_of(j * tk, tk), tk)])

    def update(s_ref, p_ref, j):
        vj = v_ref[pl.ds(pl.multiple_of(j * tk, tk), tk), :]
        half = tq // FLASH_PARTS
        for part in range(FLASH_PARTS):
            for r in range(half // FLASH_ROWS):
                r0 = part * half + r * FLASH_ROWS
                rs = slice(r0, r0 + FLASH_ROWS)
                s = s_ref[rs, :]
                m_prev = m_scr[rs, :]
                m_new = jnp.maximum(m_prev, jnp.max(s, axis=-1, keepdims=True))
                m_scr[rs, :] = m_new
                a_scr[rs, :] = jnp.exp2(m_prev - m_new)
                p_ref[rs, :] = jnp.exp2(s - m_new).astype(BF16)
            hs = slice(part * half, (part + 1) * half)
            acc_scr[hs, :] = a_scr[hs, :] * acc_scr[hs, :] + _dot(p_ref[hs, :], vj)

    s_a[...] = scores(0)

    def body(i, carry):
        s_b[...] = scores(2 * i + 1)
        update(s_a, p_a, 2 * i)
        s_a[...] = scores(2 * i + 2)
        update(s_b, p_b, 2 * i + 1)
        return carry

    lax.fori_loop(0, (nk - 1) // 2, body, 0)
    if nk % 2 == 1:
        update(s_a, p_a, nk - 1)
    else:
        s_b[...] = scores(nk - 1)
        update(s_a, p_a, nk - 2)
        update(s_b, p_b, nk - 1)
    o_ref[...] = (acc_scr[:, 0:LANE] / acc_scr[:, LANE:LANE + 1]).astype(o_ref.dtype)


def _key_tile(ttot):
    for cand in (1280, 1024, 768, 512, 256, 128):
        if ttot % cand == 0:
            return cand
    raise ValueError("key length must be a multiple of 128")


def _attention(q, k, v):
    bsz, heads, seq, _ = q.shape
    ttot = v.shape[2]
    tq = 1024
    nq = seq // tq
    tk = _key_tile(ttot)
    return pl.pallas_call(
        functools.partial(_flash_kernel, tk),
        grid=(bsz, heads, nq),
        in_specs=[pl.BlockSpec((None, None, tq, QK_PAD), lambda b, h, i: (b, h, i, 0)),
                  pl.BlockSpec((None, None, QK_PAD, ttot), lambda b, h, i: (b, h, 0, 0),
                               pipeline_mode=pl.Buffered(1)),
                  pl.BlockSpec((None, None, ttot, 2 * LANE), lambda b, h, i: (b, h, 0, 0),
                               pipeline_mode=pl.Buffered(1))],
        out_specs=pl.BlockSpec((None, tq, LANE), lambda b, h, i: (b, i, h)),
        out_shape=jax.ShapeDtypeStruct((bsz, seq, heads * LANE), BF16),
        scratch_shapes=[pltpu.VMEM((tq, tk), F32), pltpu.VMEM((tq, tk), F32),
                        pltpu.VMEM((tq, tk), BF16), pltpu.VMEM((tq, tk), BF16),
                        pltpu.VMEM((tq, 1), F32), pltpu.VMEM((tq, 1), F32),
                        pltpu.VMEM((tq, 2 * LANE), F32)],
        compiler_params=_params(("parallel", "parallel", "arbitrary")),
    )(q, k, v)


def _pack_halves(x):
    w = x.shape[1] // 2
    bits = lax.bitcast_convert_type(x.astype(BF16).astype(F32), U32)
    return jnp.bitwise_or(jnp.right_shift(bits[:, :w], jnp.uint32(16)), bits[:, w:])


def _unpack_halves(wd):
    lo = lax.bitcast_convert_type(jnp.left_shift(wd, jnp.uint32(16)), F32)
    hi = lax.bitcast_convert_type(jnp.bitwise_and(wd, jnp.uint32(0xFFFF0000)), F32)
    return lo, hi


def _ffn_front(xn, gf_ref, sh_ref, sc_ref, wrh_ref, wrl_ref, hp_ref, lg_ref):
    h2 = _rms(xn, gf_ref[...]) * (1.0 + sc_ref[...]) + sh_ref[...]
    hh = h2.astype(BF16)
    hl = (h2 - hh.astype(F32)).astype(BF16)
    hp_ref[...] = _pack_halves(h2)
    both = _dot(hh, wrl_ref[...])
    lg_ref[...] = both[:, 0:LANE] + both[:, LANE:] + _dot(hl, wrh_ref[...])


def _mix_out_kernel(of_ref, ob_ref, z_ref, ym_ref, x_ref, gog_ref, wout_ref, g1_ref,
                    gf_ref, sh_ref, sc_ref, wrh_ref, wrl_ref, xo_ref, hp_ref, lg_ref, mix):
    dv = GDN_DK
    o = of_ref[...] + ob_ref[...]
    for h in range(GDN_HEADS):
        hs = slice(h * dv, (h + 1) * dv)
        y = _rms(o[:, hs], gog_ref[...])
        mix[:, hs] = (y * _silu(z_ref[:, hs].astype(F32))).astype(BF16)
    hv = GDN_HEADS * dv
    mix[:, hv:] = ym_ref[...]
    xn = x_ref[...] + g1_ref[...] * _dot(mix[...], wout_ref[...])
    xo_ref[...] = xn
    _ffn_front(xn, gf_ref, sh_ref, sc_ref, wrh_ref, wrl_ref, hp_ref, lg_ref)


def _mix_out(o2, p_all, ymla, x, gog, wout, g1, gf, sh2, sc2, wrh, wrl, nct_tiles):
    bsz, seq, d = x.shape
    tm = SEQ_TILE
    hv = GDN_HEADS * GDN_DK
    vec = lambda: pl.BlockSpec((None, 1, d), lambda b, i: (b, 0, 0))
    full = lambda a: pl.BlockSpec(a.shape, lambda b, i: (0,) * a.ndim)
    return pl.pallas_call(
        _mix_out_kernel,
        grid=(bsz, seq // tm),
        in_specs=[pl.BlockSpec((None, None, tm, hv), lambda b, i: (0, b, i, 0)),
                  pl.BlockSpec((None, None, tm, hv), lambda b, i: (1, b, i, 0)),
                  pl.BlockSpec((None, tm, hv), lambda b, i: (b, i + nct_tiles, P_Z // hv)),
                  pl.BlockSpec((None, tm, hv), lambda b, i: (b, i, 0)),
                  pl.BlockSpec((None, tm, d), lambda b, i: (b, i, 0)),
                  full(gog), full(wout), vec(), full(gf), vec(), vec(), full(wrh), full(wrl)],
        out_specs=[pl.BlockSpec((None, tm, d), lambda b, i: (b, i, 0)),
                   pl.BlockSpec((None, tm, d // 2), lambda b, i: (b, i, 0)),
                   pl.BlockSpec((None, tm, LANE), lambda b, i: (b, i, 0))],
        out_shape=[jax.ShapeDtypeStruct((bsz, seq, d), F32),
                   jax.ShapeDtypeStruct((bsz, seq, d // 2), U32),
                   jax.ShapeDtypeStruct((bsz, seq, LANE), F32)],
        scratch_shapes=[pltpu.VMEM((tm, wout.shape[0]), BF16)],
        compiler_params=_params(("parallel", "parallel")),
    )(o2, o2, p_all, ymla, x, gog, wout, g1, gf, sh2, sc2, wrh, wrl)


POOL_HALO = 8


def _pool_kernel(seq, xm_ref, xp_ref, xn_ref, gm_ref, sh1_ref, sc1_ref, wp_ref, ps_ref, g1_ref,
                 gf_ref, sh_ref, sc_ref, wrh_ref, wrl_ref, xo_ref, hp_ref, lg_ref, ext):
    i = pl.program_id(1)
    n = pl.num_programs(1)
    tm = xm_ref.shape[0]
    hl = POOL_HALO

    def normed(ref):
        return _rms(ref[...], gm_ref[...]) * (1.0 + sc1_ref[...]) + sh1_ref[...]

    ext[hl:hl + tm, :] = normed(xm_ref)
    ext[0:hl, :] = jnp.where(i == 0, 0.0, normed(xp_ref))
    ext[hl + tm:2 * hl + tm, :] = jnp.where(i == n - 1, 0.0, normed(xn_ref))
    t = i * tm + lax.broadcasted_iota(I32, (tm, 1), 0)
    gw = xm_ref.shape[1] // len(POOL_WINDOWS)
    for gi, win in enumerate(POOL_WINDOWS):
        cs = slice(gi * gw, (gi + 1) * gw)
        half = win // 2
        acc = None
        for off in range(-half, win - half):
            term = ext[hl + off:hl + off + tm, cs]
            acc = term if acc is None else acc + term
        lo = jnp.clip(t - half, 0, seq)
        hi = jnp.clip(t - half + win, 0, seq)
        pooled = acc / (hi - lo).astype(F32) - ext[hl:hl + tm, cs]
        y = _dot(pooled.astype(BF16), wp_ref[gi]) * ps_ref[:, cs]
        xo_ref[:, cs] = xm_ref[:, cs] + g1_ref[:, cs] * y
    _ffn_front(xo_ref[...], gf_ref, sh_ref, sc_ref, wrh_ref, wrl_ref, hp_ref, lg_ref)


def _pool_mixer(x, gm, sh1, sc1, wp, ps, g1, gf, sh2, sc2, wrh, wrl):
    bsz, seq, d = x.shape
    tm = SEQ_TILE
    hl = POOL_HALO
    r = tm // hl
    nh = seq // hl
    vec = lambda: pl.BlockSpec((None, 1, d), lambda b, i: (b, 0, 0))
    full = lambda a: pl.BlockSpec(a.shape, lambda b, i: (0,) * a.ndim)
    return pl.pallas_call(
        functools.partial(_pool_kernel, seq),
        grid=(bsz, seq // tm),
        in_specs=[pl.BlockSpec((None, tm, d), lambda b, i: (b, i, 0)),
                  pl.BlockSpec((None, hl, d), lambda b, i: (b, jnp.maximum(i * r - 1, 0), 0)),
                  pl.BlockSpec((None, hl, d), lambda b, i: (b, jnp.minimum((i + 1) * r, nh - 1), 0)),
                  full(gm), vec(), vec(), full(wp), full(ps), vec(),
                  full(gf), vec(), vec(), full(wrh), full(wrl)],
        out_specs=[pl.BlockSpec((None, tm, d), lambda b, i: (b, i, 0)),
                   pl.BlockSpec((None, tm, d // 2), lambda b, i: (b, i, 0)),
                   pl.BlockSpec((None, tm, LANE), lambda b, i: (b, i, 0))],
        out_shape=[jax.ShapeDtypeStruct((bsz, seq, d), F32),
                   jax.ShapeDtypeStruct((bsz, seq, d // 2), U32),
                   jax.ShapeDtypeStruct((bsz, seq, LANE), F32)],
        scratch_shapes=[pltpu.VMEM((tm + 2 * hl, d), F32)],
        compiler_params=_params(("parallel", "parallel")),
    )(x, x, x, gm, sh1, sc1, wp, ps, g1, gf, sh2, sc2, wrh, wrl)


def _first_max(vals, idx, sentinel):
    m = jnp.max(vals, axis=0, keepdims=True)
    first = jnp.min(jnp.where(vals == m, idx, sentinel), axis=0, keepdims=True)
    return m, first


def _route_kernel(rows, lg_ref, bias_ref, dest_ref, gates_ref, cnt_ref, carry):
    phase = pl.program_id(0)
    step = pl.program_id(1)

    @pl.when(jnp.logical_and(phase == 0, step == 0))
    def _():
        carry[...] = jnp.zeros_like(carry)

    @pl.when(jnp.logical_and(phase == 1, step == 0))
    def _():
        shift = int(math.log2(rows))
        counts_row = carry[...].T.astype(I32)
        padded = jnp.left_shift(jnp.right_shift(counts_row + (rows - 1), shift), shift).astype(F32)
        i_exp = lax.broadcasted_iota(I32, padded.shape, 0)
        j_exp = lax.broadcasted_iota(I32, padded.shape, 1)
        first_row = jnp.sum(jnp.where(j_exp < i_exp, padded, 0.0), axis=1, keepdims=True)
        carry[...] = jnp.broadcast_to(first_row, carry.shape)

    tm = lg_ref.shape[0]
    epg = EXPERTS_PER_GROUP
    scores = jax.nn.sigmoid(lg_ref[...].T[0:N_EXPERTS, :])
    biased = scores + bias_ref[:, 0:1]
    eidx = lax.broadcasted_iota(I32, (epg, tm), 0)
    best_score = best_grp = best_i1 = best_i2 = None
    for g in range(N_GROUPS):
        blk = biased[g * epg:(g + 1) * epg, :]
        m1, i1 = _first_max(blk, eidx, epg)
        m2, i2 = _first_max(jnp.where(eidx == i1, -jnp.inf, blk), eidx, epg)
        gs = m1 + m2
        if g == 0:
            best_score, best_grp, best_i1, best_i2 = gs, jnp.zeros_like(i1), i1, i2
        else:
            better = gs > best_score
            best_score = jnp.where(better, gs, best_score)
            best_grp = jnp.where(better, g, best_grp)
            best_i1 = jnp.where(better, i1, best_i1)
            best_i2 = jnp.where(better, i2, best_i2)
    e1 = best_grp * epg + best_i1
    e2 = best_grp * epg + best_i2
    eall = lax.broadcasted_iota(I32, (LANE, tm), 0)
    oh1 = eall == e1
    oh2 = eall == e2
    oh = jnp.logical_or(oh1, oh2).astype(BF16)
    picked = jnp.sum(oh.astype(F32), axis=1, keepdims=True)

    @pl.when(phase == 0)
    def _():
        carry[...] = carry[...] + picked
        cnt_ref[...] = carry[...]

    @pl.when(phase == 1)
    def _():
        s1 = jnp.sum(jnp.where(oh1[0:N_EXPERTS], scores, 0.0), axis=0, keepdims=True)
        s2 = jnp.sum(jnp.where(oh2[0:N_EXPERTS], scores, 0.0), axis=0, keepdims=True)
        denom = s1 + s2
        earlier = (lax.broadcasted_iota(I32, (tm, tm), 0) < lax.broadcasted_iota(I32, (tm, tm), 1)).astype(BF16)
        row = _dot(oh, earlier) + carry[:, 0:1]
        d1 = jnp.sum(jnp.where(oh1, row, 0.0), axis=0, keepdims=True)
        d2 = jnp.sum(jnp.where(oh2, row, 0.0), axis=0, keepdims=True)
        carry[...] = carry[...] + picked
        dest_ref[...] = jnp.concatenate([d1.astype(I32), d2.astype(I32), jnp.zeros((6, tm), I32)], axis=0)
        gates = jnp.concatenate([s1 / denom, s2 / denom, jnp.zeros((LANE - 2, tm), F32)], axis=0)
        gates_ref[...] = gates.T


def _route(logits, bias_col):
    n = logits.shape[0]
    tm = ROUTE_TILE
    rows = MOE_ROWS
    assert rows & (rows - 1) == 0
    return pl.pallas_call(
        functools.partial(_route_kernel, rows),
        grid=(2, n // tm),
        in_specs=[pl.BlockSpec((tm, LANE), lambda p, t: (t, 0)),
                  pl.BlockSpec(bias_col.shape, lambda p, t: (0, 0))],
        out_specs=[pl.BlockSpec((8, tm), lambda p, t: (0, p * t)),
                   pl.BlockSpec((tm, LANE), lambda p, t: (p * t, 0)),
                   pl.BlockSpec((LANE, LANE), lambda p, t: (0, 0))],
        out_shape=[jax.ShapeDtypeStruct((8, n), I32),
                   jax.ShapeDtypeStruct((n, LANE), F32),
                   jax.ShapeDtypeStruct((LANE, LANE), F32)],
        scratch_shapes=[pltpu.VMEM((LANE, LANE), F32)],
        compiler_params=_params(("arbitrary", "arbitrary")),
    )(logits, bias_col)


def _dispatch_copy(hp_ref, xs_hbm, sem, j, dst):
    return pltpu.make_async_copy(hp_ref.at[pl.ds(j, 1), :], xs_hbm.at[pl.ds(dst, 1), :], sem)


def _zero_block_copy(zbuf, xs_hbm, zsem, start):
    return pltpu.make_async_copy(zbuf, xs_hbm.at[pl.ds(start, zbuf.shape[0]), :], zsem)


def _dispatch_kernel(tm, pad_end_ref, nu_ref, d0_ref, d1_ref, hp_ref, xs_hbm, zbuf, sem, zsem):
    rows = zbuf.shape[0]
    n_blocks = xs_hbm.shape[0] // rows

    @pl.when(pl.program_id(0) == 0)
    def _():
        zbuf[...] = jnp.zeros_like(zbuf)

        def expert_tail(e, carry):
            start = jnp.maximum(pad_end_ref[e] - rows, 0)
            _zero_block_copy(zbuf, xs_hbm, zsem, pl.multiple_of(start, rows)).start()
            return carry

        lax.fori_loop(0, N_EXPERTS, expert_tail, 0)

        def unused_block(b, carry):
            _zero_block_copy(zbuf, xs_hbm, zsem, pl.multiple_of(b * rows, rows)).start()
            return carry

        lax.fori_loop(nu_ref[0], n_blocks, unused_block, 0)

        def drain(b, carry):
            _zero_block_copy(zbuf, xs_hbm, zsem, 0).wait()
            return carry

        lax.fori_loop(0, N_EXPERTS + n_blocks - nu_ref[0], drain, 0)

    def issue(j, carry):
        _dispatch_copy(hp_ref, xs_hbm, sem, j, d0_ref[0, j]).start()
        _dispatch_copy(hp_ref, xs_hbm, sem, j, d1_ref[0, j]).start()
        return carry

    lax.fori_loop(0, tm, issue, 0, unroll=DMA_UNROLL)
    for _ in range(2):
        pltpu.make_async_copy(hp_ref, xs_hbm.at[pl.ds(0, tm), :], sem).wait()


def _dispatch(hp, dest, pad_end, n_used, n_rows):
    n, half = hp.shape
    tm = ROUTE_TILE
    nt = n // tm
    idx = lambda k: pl.BlockSpec((None, None, 1, tm), lambda i, pe, nu: (k, i, 0, 0), memory_space=pltpu.SMEM)
    dest4 = dest.reshape(dest.shape[0], nt, 1, tm)
    grid_spec = pltpu.PrefetchScalarGridSpec(
        num_scalar_prefetch=2,
        grid=(nt,),
        in_specs=[idx(0), idx(1), pl.BlockSpec((tm, half), lambda i, pe, nu: (i, 0))],
        out_specs=pl.BlockSpec(memory_space=pl.ANY),
        scratch_shapes=[pltpu.VMEM((MOE_ROWS, half), U32), pltpu.SemaphoreType.DMA(()),
                        pltpu.SemaphoreType.DMA(())],
    )
    return pl.pallas_call(
        functools.partial(_dispatch_kernel, tm),
        grid_spec=grid_spec,
        out_shape=jax.ShapeDtypeStruct((n_rows, half), U32),
        compiler_params=_params(("arbitrary",)),
    )(pad_end, n_used, dest4, dest4, hp)


def _expert_weight_copies(layer, wgu_hbm, wd_hbm, wgu32, wd32, sem, expert, slot):
    return (pltpu.make_async_copy(wgu_hbm.at[layer, expert], wgu32.at[slot], sem.at[0, slot]),
            pltpu.make_async_copy(wd_hbm.at[layer, expert], wd32.at[slot], sem.at[1, slot]))


def _experts_kernel(layer, be_ref, nu_ref, next_ref, slot_ref, xs_ref, wgu_hbm, wd_hbm, y_ref,
                    wgu32, wd32, wgu16, wd16, sem):
    i = pl.program_id(0)
    expert = be_ref[i]
    copies = functools.partial(_expert_weight_copies, layer, wgu_hbm, wd_hbm, wgu32, wd32, sem)

    @pl.when(i < nu_ref[0])
    def _():
        @pl.when(jnp.logical_or(i == 0, expert != be_ref[jnp.maximum(i - 1, 0)]))
        def _():
            slot = slot_ref[expert]

            @pl.when(i == 0)
            def _():
                for c in copies(expert, slot):
                    c.start()

            for c in copies(expert, slot):
                c.wait()
            wgu16[...] = wgu32[slot].astype(BF16)
            wd16[...] = wd32[slot].astype(BF16)
            following = next_ref[expert]

            @pl.when(following >= 0)
            def _():
                for c in copies(following, 1 - slot):
                    c.start()

        lo, hi = _unpack_halves(xs_ref[...])
        half = lo.shape[1]
        gu = _dot(lo.astype(BF16), wgu16[0:half, :]) + _dot(hi.astype(BF16), wgu16[half:, :])
        f = gu.shape[1] // 2
        hid = _silu(gu[:, :f]) * gu[:, f:]
        y_ref[...] = _pack_halves(_dot(hid.astype(BF16), wd16[...]))

    @pl.when(i >= nu_ref[0])
    def _():
        y_ref[...] = jnp.zeros_like(y_ref)


def _experts(layer, block_expert, n_used, next_expert, weight_slot, xs, w_gate_up, w_down):
    n_rows, half = xs.shape
    rows = MOE_ROWS
    d = 2 * half
    f2 = w_gate_up.shape[3]
    grid_spec = pltpu.PrefetchScalarGridSpec(
        num_scalar_prefetch=4,
        grid=(n_rows // rows,),
        in_specs=[pl.BlockSpec((rows, half), lambda i, *_: (i, 0)),
                  pl.BlockSpec(memory_space=pl.ANY),
                  pl.BlockSpec(memory_space=pl.ANY)],
        out_specs=pl.BlockSpec((rows, half), lambda i, *_: (i, 0)),
        scratch_shapes=[pltpu.VMEM((2, d, f2), F32), pltpu.VMEM((2, f2 // 2, d), F32),
                        pltpu.VMEM((d, f2), BF16), pltpu.VMEM((f2 // 2, d), BF16),
                        pltpu.SemaphoreType.DMA((2, 2))],
    )
    return pl.pallas_call(
        functools.partial(_experts_kernel, layer),
        grid_spec=grid_spec,
        out_shape=jax.ShapeDtypeStruct((n_rows, half), U32),
        compiler_params=_params(("arbitrary",)),
    )(block_expert, n_used, next_expert, weight_slot, xs, w_gate_up, w_down)


def _combine_copy(y_hbm, buf, sem, slot, k, src, j):
    return pltpu.make_async_copy(y_hbm.at[pl.ds(src, 1), :], buf.at[slot, k, pl.ds(j, 1), :], sem.at[slot])


def _combine_kernel(tm, d0_ref, d1_ref, d0n_ref, d1n_ref, y_hbm, gates_ref, x_ref, g2_ref, xo_ref, buf, sem):
    step = pl.program_id(0) * pl.num_programs(1) + pl.program_id(1)
    total = pl.num_programs(0) * pl.num_programs(1)
    slot = step % 2

    def start_gather(r0_ref, r1_ref, s):
        def issue(j, carry):
            _combine_copy(y_hbm, buf, sem, s, 0, r0_ref[0, j], j).start()
            _combine_copy(y_hbm, buf, sem, s, 1, r1_ref[0, j], j).start()
            return carry
        lax.fori_loop(0, tm, issue, 0, unroll=DMA_UNROLL)

    @pl.when(step == 0)
    def _():
        start_gather(d0_ref, d1_ref, 0)

    @pl.when(step + 1 < total)
    def _():
        start_gather(d0n_ref, d1n_ref, 1 - slot)

    for k in range(2):
        pltpu.make_async_copy(y_hbm.at[pl.ds(0, tm), :], buf.at[slot, k], sem.at[slot]).wait()
    lo0, hi0 = _unpack_halves(buf[slot, 0])
    lo1, hi1 = _unpack_halves(buf[slot, 1])
    half = lo0.shape[1]
    ga = gates_ref[:, 0:1]
    gb = gates_ref[:, 1:2]
    xo_ref[:, 0:half] = x_ref[:, 0:half] + g2_ref[:, 0:half] * (ga * lo0 + gb * lo1)
    xo_ref[:, half:] = x_ref[:, half:] + g2_ref[:, half:] * (ga * hi0 + gb * hi1)


def _combine(y, dest, gates, x, g2):
    bsz, seq, d = x.shape
    tm = SEQ_TILE
    nt = seq // tm
    half = d // 2
    last = bsz * nt - 1
    idx = lambda k, ahead: pl.BlockSpec(
        (None, None, 1, tm), lambda b, i: (k, jnp.minimum(b * nt + i + ahead, last), 0, 0),
        memory_space=pltpu.SMEM)
    dest4 = dest.reshape(dest.shape[0], bsz * nt, 1, tm)
    return pl.pallas_call(
        functools.partial(_combine_kernel, tm),
        grid=(bsz, nt),
        in_specs=[idx(0, 0), idx(1, 0), idx(0, 1), idx(1, 1),
                  pl.BlockSpec(memory_space=pl.ANY),
                  pl.BlockSpec((tm, LANE), lambda b, i: (b * nt + i, 0)),
                  pl.BlockSpec((None, tm, d), lambda b, i: (b, i, 0)),
                  pl.BlockSpec((None, 1, d), lambda b, i: (b, 0, 0))],
        out_specs=pl.BlockSpec((None, tm, d), lambda b, i: (b, i, 0)),
        out_shape=jax.ShapeDtypeStruct((bsz, seq, d), F32),
        scratch_shapes=[pltpu.VMEM((2, 2, tm, half), U32), pltpu.SemaphoreType.DMA((2,))],
        compiler_params=_params(("arbitrary", "arbitrary")),
    )(dest4, dest4, dest4, dest4, y, gates, x, g2)


def _moe(layer, x, hp, logits, g2, bias_col, w_gate_up, w_down):
    bsz, seq, d = x.shape
    n = bsz * seq
    rows = MOE_ROWS
    dest, gates, cnt = _route(logits.reshape(n, LANE), bias_col)
    counts = cnt[:N_EXPERTS, 0].astype(I32)
    pad_end = jnp.cumsum((counts + rows - 1) // rows * rows)
    n_rows = -(-(n * TOP_K + N_EXPERTS * (rows - 1)) // rows) * rows
    block_start = jnp.arange(n_rows // rows, dtype=I32) * rows
    block_expert = jnp.minimum(jnp.sum((pad_end[None, :] <= block_start[:, None]).astype(I32), axis=1),
                               N_EXPERTS - 1)
    n_used = pad_end[-1:] // rows
    owns = counts > 0
    eid = jnp.arange(N_EXPERTS, dtype=I32)
    later = jnp.where(owns[None, :] & (eid[None, :] > eid[:, None]), eid[None, :], N_EXPERTS)
    next_expert = jnp.min(later, axis=1)
    next_expert = jnp.where(next_expert < N_EXPERTS, next_expert, -1).astype(I32)
    weight_slot = ((jnp.cumsum(owns.astype(I32)) - 1) % 2).astype(I32)
    xs = _dispatch(hp.reshape(n, d // 2), dest, pad_end, n_used, n_rows)
    y = _experts(layer, block_expert, n_used, next_expert, weight_slot, xs, w_gate_up, w_down)
    return _combine(y, dest, gates, x, g2)


def _permute_w_in(w_in):
    d = w_in.shape[0]
    hq = GDN_HEADS * GDN_DK
    off_a = 4 * hq
    off_cq = off_a + 4 * GDN_HEADS
    off_ckv = off_cq + Q_LORA
    off_kr = off_ckv + KV_LORA
    z = lambda n: jnp.zeros((d, n), w_in.dtype)
    a = lambda k: w_in[:, off_a + k * GDN_HEADS:off_a + (k + 1) * GDN_HEADS]
    pad_ab = LANE - 2 * GDN_HEADS
    cols = [w_in[:, :off_a], w_in[:, off_cq:off_kr + ROPE_DIM], z(LANE - ROPE_DIM),
            a(0), a(2), z(pad_ab), a(1), a(3), z(pad_ab)]
    w = jnp.concatenate(cols, axis=1)
    return jnp.concatenate([w, z(P_WIDTH - w.shape[1])], axis=1).astype(BF16)


def _rope_tables(rows, n_ctx):
    row = jnp.repeat(jnp.arange(rows, dtype=F32), GRID_W)
    col = jnp.tile(jnp.arange(GRID_W, dtype=F32), rows)
    pairs = ROPE_DIM // 4
    inv_freq = ROPE_THETA ** (-jnp.arange(pairs, dtype=F32) / pairs)
    ang_r = row[:, None] * inv_freq
    ang_c = col[:, None] * inv_freq
    ang = jnp.concatenate([ang_r, ang_r, ang_c, ang_c], axis=-1)
    ang = jnp.concatenate([jnp.zeros((n_ctx, ROPE_DIM), F32), ang], axis=0)
    cos, sin = jnp.cos(ang), jnp.sin(ang)
    return jnp.concatenate([cos, cos], axis=1), jnp.concatenate([sin, sin], axis=1)


def kernel(x, c, ctx, c_ctx, w_mod, b_mod, norm_mix_g, norm_ffn_g, w_in, conv_qkv, a_log_fwd, a_log_bwd, dt_bias_fwd, dt_bias_bwd, gdn_out_g, q_a_norm_g, w_uq, kv_a_norm_g, w_ukv, q_norm_g, k_norm_g, w_out, w_pool, pool_scale, w_router, router_bias, w_gate_up, w_down):
    bsz, seq, d = x.shape
    n_ctx = ctx.shape[1]
    depth = w_mod.shape[0]
    assert depth == 2, "the context stream is only read: no layer after the first even layer reads it"
    nct_tiles = n_ctx // SEQ_TILE

    cc = jnp.concatenate([c, c_ctx[None, :], jnp.zeros((8 - bsz - 1, d), F32)], axis=0)
    mod = _modulation(cc, w_mod, b_mod).reshape(depth, 8, 6, d)

    def mods(layer):
        m = mod[layer]
        return [m[:bsz, k][:, None, :] for k in range(6)], [m[bsz:bsz + 1, k] for k in range(6)]

    wr = jnp.pad(w_router, ((0, 0), (0, LANE - N_EXPERTS)))
    wrh = wr.astype(BF16)
    wrl = jnp.concatenate([wrh, (wr - wrh.astype(F32)).astype(BF16)], axis=1)
    bias_col = jnp.broadcast_to(router_bias.astype(F32)[:, None], (N_EXPERTS, LANE))
    row = lambda v: v.astype(F32)[None, :]

    for layer in range(depth):
        j = layer // 2
        (sh1, sc1, g1, sh2, sc2, g2), (csh1, csc1, _, _, _, _) = mods(layer)
        gm = row(norm_mix_g[layer])
        gf = row(norm_ffn_g[layer])
        if layer % 2 == 0:
            p_all = _in_projection(ctx, x, gm, csh1, csc1, sh1, sc1, _permute_w_in(w_in[j]))
            conv_w = conv_qkv[j].astype(F32)
            gate_params = jnp.zeros((2, 8, LANE), F32)
            gate_params = gate_params.at[0, 0, :GDN_HEADS].set(a_log_fwd[j]).at[0, 1, :GDN_HEADS].set(dt_bias_fwd[j])
            gate_params = gate_params.at[1, 0, :GDN_HEADS].set(a_log_bwd[j]).at[1, 1, :GDN_HEADS].set(dt_bias_bwd[j])
            qkv_all, gb_all = _gdn_front(p_all, conv_w, gate_params, nct_tiles)
            o2 = _gdn_scan(qkv_all, gb_all, n_ctx)

            cos, sin = _rope_tables(seq // GRID_W, n_ctx)
            gq = q_norm_g[j].astype(F32)
            gk = k_norm_g[j].astype(F32)
            wq = w_uq[j].reshape(Q_LORA, MLA_HEADS, QK_HEAD)
            wq = jnp.concatenate([wq[:, :, :NOPE_DIM].reshape(Q_LORA, -1),
                                  wq[:, :, NOPE_DIM:].reshape(Q_LORA, -1)], axis=1).astype(BF16)
            q = _mla_queries(p_all, row(q_a_norm_g[j]), wq, gq[None, :NOPE_DIM],
                             jnp.tile(gq[NOPE_DIM:], 2)[None, :], cos, sin, nct_tiles, seq)
            gk_rope = jnp.concatenate([gk[NOPE_DIM:], jnp.zeros((LANE - ROPE_DIM,), F32)])[None, :]
            k_all, v_all = _mla_keys_values(p_all, row(kv_a_norm_g[j]), w_ukv[j].astype(BF16),
                                            gk[None, :NOPE_DIM], gk_rope, cos, sin)
            ymla = _attention(q, k_all, v_all)
            x, hp, logits = _mix_out(o2, p_all, ymla, x, row(gdn_out_g[j]), w_out[j].astype(BF16), g1,
                                     gf, sh2, sc2, wrh, wrl, nct_tiles)
        else:
            x, hp, logits = _pool_mixer(x, gm, sh1, sc1, w_pool[j].astype(BF16), row(pool_scale[j]), g1,
                                        gf, sh2, sc2, wrh, wrl)
        x = _moe(layer, x, hp, logits, g2, bias_col, w_gate_up, w_down)
    return x
```

```python
import functools
import math

import jax
import jax.numpy as jnp
from jax import lax
from jax.experimental import pallas as pl
from jax.experimental.pallas import tpu as pltpu

F32 = jnp.float32
BF16 = jnp.bfloat16
U32 = jnp.uint32
I32 = jnp.int32

EPS = 1e-6
GRID_W = 64
GDN_HEADS = 8
GDN_DK = 128
MLA_HEADS = 8
NOPE_DIM = 128
ROPE_DIM = 64
QK_HEAD = NOPE_DIM + ROPE_DIM
QK_PAD = 256
ROPE_THETA = 10000.0
Q_LORA = 512
KV_LORA = 256
POOL_WINDOWS = (2, 4, 8, 16)
N_EXPERTS = 64
EXPERTS_PER_GROUP = 8
N_GROUPS = N_EXPERTS // EXPERTS_PER_GROUP
TOP_K = 2

LANE = 128
V7X_VMEM_LIMIT = 56 * 1024 * 1024

SEQ_TILE = 256
GDN_CHUNK = 128
MOE_ROWS = 256
DMA_UNROLL = 8
ROUTE_TILE = 512

P_QKV = 0
P_Z = 3072
P_CQ = 4096
P_CKV = 4608
P_KR = 4864
P_AB = 4992
P_WIDTH = 5376
P_NTILE = 1792


def _params(sem):
    return pltpu.CompilerParams(dimension_semantics=sem, vmem_limit_bytes=V7X_VMEM_LIMIT)


def _dot(a, b):
    return jnp.dot(a, b, preferred_element_type=F32)


def _dot_nt(a, b):
    return lax.dot_general(a, b, (((1,), (1,)), ((), ())), preferred_element_type=F32)


def _dot_tn(a, b):
    return lax.dot_general(a, b, (((0,), (0,)), ((), ())), preferred_element_type=F32)


def _silu(x):
    return x * jax.nn.sigmoid(x)


def _rms(x, g):
    return x * lax.rsqrt(jnp.mean(x * x, axis=-1, keepdims=True) + EPS) * g


def _split3(x):
    hi = x.astype(BF16)
    r = x - hi.astype(F32)
    mid = r.astype(BF16)
    lo = (r - mid.astype(F32)).astype(BF16)
    return hi, mid, lo


def _mod_kernel(c_ref, w_ref, b_ref, o_ref):
    s = _silu(c_ref[...]).astype(BF16)
    o_ref[...] = _dot(s, w_ref[...].astype(BF16)) + b_ref[...]


def _modulation(cc, w_mod, b_mod):
    depth, d, n6 = w_mod.shape
    tn = 1024
    return pl.pallas_call(
        _mod_kernel,
        grid=(depth, n6 // tn),
        in_specs=[pl.BlockSpec((8, d), lambda l, j: (0, 0)),
                  pl.BlockSpec((None, d, tn), lambda l, j: (l, 0, j)),
                  pl.BlockSpec((None, 1, tn), lambda l, j: (l, 0, j))],
        out_specs=pl.BlockSpec((None, 8, tn), lambda l, j: (l, 0, j)),
        out_shape=jax.ShapeDtypeStruct((depth, 8, n6), F32),
        compiler_params=_params(("parallel", "parallel")),
    )(cc, w_mod, b_mod.reshape(depth, 1, n6))


def _inproj_kernel(nct, xc_ref, x_ref, g_ref, shc_ref, scc_ref, sh_ref, sc_ref, w_ref, o_ref):
    i = pl.program_id(2)

    def run(xr, shr, scr):
        h = _rms(xr[...], g_ref[...]) * (1.0 + scr[...]) + shr[...]
        o_ref[...] = _dot(h.astype(BF16), w_ref[...]).astype(o_ref.dtype)

    @pl.when(i < nct)
    def _():
        run(xc_ref, shc_ref, scc_ref)

    @pl.when(i >= nct)
    def _():
        run(x_ref, sh_ref, sc_ref)


def _in_projection(ctx, x, g, shc, scc, sh, sc, w):
    bsz, seq, d = x.shape
    tm = SEQ_TILE
    nct = ctx.shape[1] // tm
    nt = nct + seq // tm
    width = w.shape[1]
    tn = P_NTILE
    return pl.pallas_call(
        functools.partial(_inproj_kernel, nct),
        grid=(width // tn, bsz, nt),
        in_specs=[pl.BlockSpec((None, tm, d), lambda j, b, i: (b, jnp.minimum(i, nct - 1), 0)),
                  pl.BlockSpec((None, tm, d), lambda j, b, i: (b, jnp.maximum(i - nct, 0), 0)),
                  pl.BlockSpec((1, d), lambda j, b, i: (0, 0)),
                  pl.BlockSpec((1, d), lambda j, b, i: (0, 0)),
                  pl.BlockSpec((1, d), lambda j, b, i: (0, 0)),
                  pl.BlockSpec((None, 1, d), lambda j, b, i: (b, 0, 0)),
                  pl.BlockSpec((None, 1, d), lambda j, b, i: (b, 0, 0)),
                  pl.BlockSpec((d, tn), lambda j, b, i: (0, j))],
        out_specs=pl.BlockSpec((None, tm, tn), lambda j, b, i: (b, i, j)),
        out_shape=jax.ShapeDtypeStruct((bsz, nt * tm, width), BF16),
        compiler_params=_params(("parallel", "parallel", "parallel")),
    )(ctx, x, g, shc, scc, sh, sc, w)


GDN_HALO = 16


def _gdn_front_kernel(nct, pm_ref, pp_ref, pn_ref, cw_ref, ab0_ref, ab1_ref, gp_ref,
                      qkv_ref, gb_ref, ext):
    i = pl.program_id(1)
    n = pl.num_programs(1)
    tm = pm_ref.shape[0]
    hl = GDN_HALO
    zero_prev = jnp.logical_or(i == 0, i == nct)
    zero_next = jnp.logical_or(i == nct - 1, i == n - 1)
    ext[hl:hl + tm, :] = pm_ref[...].astype(F32)
    ext[0:hl, :] = jnp.where(zero_prev, 0.0, pp_ref[...].astype(F32))
    ext[hl + tm:2 * hl + tm, :] = jnp.where(zero_next, 0.0, pn_ref[...].astype(F32))
    taps = cw_ref.shape[0]
    nqk = 2 * GDN_HEADS
    for cb in range(qkv_ref.shape[1] // LANE):
        cs = slice(cb * LANE, (cb + 1) * LANE)
        acc = None
        for j in range(taps):
            off = hl - taps // 2 + j
            term = cw_ref[j:j + 1, cs] * ext[off:off + tm, cs]
            acc = term if acc is None else acc + term
        y = _silu(acc)
        if cb < nqk:
            inv = lax.rsqrt(jnp.sum(y * y, axis=-1, keepdims=True) + EPS)
            if cb < GDN_HEADS:
                inv = inv * (GDN_DK ** -0.5)
            y = y * inv
        qkv_ref[:, cs] = y.astype(BF16)
    for d, ab_ref in enumerate((ab0_ref, ab1_ref)):
        a = ab_ref[...].astype(F32)
        lane = lax.broadcasted_iota(I32, a.shape, 1)
        neg_decay = -jnp.exp(gp_ref[d, 0:1, :])
        xx = a + gp_ref[d, 1:2, :]
        softplus = jnp.maximum(xx, 0.0) + jnp.log1p(jnp.exp(-jnp.abs(xx)))
        gate = jnp.where(lane < GDN_HEADS, neg_decay * softplus, jax.nn.sigmoid(a))
        gb_ref[d] = jnp.where(lane < 2 * GDN_HEADS, gate, 0.0)


def _gdn_front(p_all, conv_w, gate_params, nct):
    bsz, ttot, _ = p_all.shape
    tm = SEQ_TILE
    nt = ttot // tm
    cq = P_Z
    hl = GDN_HALO
    r = tm // hl
    nh = ttot // hl
    return pl.pallas_call(
        functools.partial(_gdn_front_kernel, nct),
        grid=(bsz, nt),
        in_specs=[pl.BlockSpec((None, tm, cq), lambda b, i: (b, i, 0)),
                  pl.BlockSpec((None, hl, cq), lambda b, i: (b, jnp.maximum(i * r - 1, 0), 0)),
                  pl.BlockSpec((None, hl, cq), lambda b, i: (b, jnp.minimum((i + 1) * r, nh - 1), 0)),
                  pl.BlockSpec(conv_w.shape, lambda b, i: (0, 0)),
                  pl.BlockSpec((None, tm, LANE), lambda b, i: (b, i, P_AB // LANE)),
                  pl.BlockSpec((None, tm, LANE), lambda b, i: (b, i, P_AB // LANE + 1)),
                  pl.BlockSpec(gate_params.shape, lambda b, i: (0, 0, 0))],
        out_specs=[pl.BlockSpec((None, tm, cq), lambda b, i: (b, i, 0)),
                   pl.BlockSpec((2, None, tm, LANE), lambda b, i: (0, b, i, 0))],
        out_shape=[jax.ShapeDtypeStruct((bsz, ttot, cq), BF16),
                   jax.ShapeDtypeStruct((2, bsz, ttot, LANE), F32)],
        scratch_shapes=[pltpu.VMEM((tm + 2 * hl, cq), F32)],
        compiler_params=_params(("parallel", "parallel")),
    )(p_all, p_all, p_all, conv_w, p_all, p_all, gate_params)


def _gdn_scan_kernel(q_ref, k_ref, v_ref, gb_ref, o_ref, s_scr):
    d = pl.program_id(1)
    s = pl.program_id(2)
    c = GDN_CHUNK
    n_sub = q_ref.shape[0] // c
    dk = GDN_DK

    @pl.when(s == 0)
    def _():
        s_scr[...] = jnp.zeros_like(s_scr)

    ri = lax.broadcasted_iota(I32, (c, c), 0)
    ci = lax.broadcasted_iota(I32, (c, c), 1)
    incl = jnp.where(d == 0, ri - ci, ci - ri) >= 0
    strict = jnp.logical_and(incl, ci != ri)
    m_incl = incl.astype(BF16)
    ones = jnp.ones((c, c), BF16)
    eye = (ri == ci).astype(F32)
    pair_masks = []
    for lb in range(int(math.log2(c))):
        same_pair = jnp.right_shift(ri, lb + 1) == jnp.right_shift(ci, lb + 1)
        same_block = jnp.right_shift(ri, lb) == jnp.right_shift(ci, lb)
        pair_masks.append(jnp.logical_and(same_pair, jnp.logical_not(same_block)))
    heads = range(GDN_HEADS)
    col = lambda a, h: a[:, h:h + 1]

    rows = [pl.ds(pl.multiple_of(jnp.where(d == 0, sub, n_sub - 1 - sub) * c, c), c) for sub in range(n_sub)]
    subs = range(n_sub)
    pairs = [(sub, h) for sub in subs for h in heads]
    hcols = lambda ref, sub, h: ref[rows[sub], h * dk:(h + 1) * dk]
    gb = [gb_ref[rows[sub], :] for sub in subs]
    lane = lax.broadcasted_iota(I32, gb[0].shape, 1)
    gsplit = [_split3(jnp.where(lane < GDN_HEADS, gb[sub], 0.0)) for sub in subs]
    gc = [sum(_dot(m_incl, t) for t in gsplit[sub]) for sub in subs]
    gl = [sum(_dot(ones, t) for t in gsplit[sub]) for sub in subs]
    gc_t = [gc[sub].T for sub in subs]
    e_gc = [jnp.exp(gc[sub]) for sub in subs]
    e_rest = [jnp.exp(gl[sub] - gc[sub]) for sub in subs]
    e_all = [jnp.exp(gl[sub]) for sub in subs]
    qb = [hcols(q_ref, sub, h) for sub, h in pairs]
    kb16 = [hcols(k_ref, sub, h) for sub, h in pairs]
    k = [a.astype(F32) for a in kb16]
    beta = [col(gb[sub], GDN_HEADS + h) for sub, h in pairs]
    npair = range(len(pairs))
    kbeta = [k[i] * beta[i] for i in npair]
    kk = [_dot_nt(kbeta[i].astype(BF16), kb16[i]) for i in npair]
    qk = [_dot_nt(qb[i], kb16[i]) for i in npair]
    dec = [jnp.where(incl, jnp.exp(jnp.where(incl, col(gc[sub], h) - gc_t[sub][h:h + 1, :], 0.0)), 0.0)
           for sub, h in pairs]
    low = [jnp.where(strict, kk[i] * dec[i], 0.0) for i in npair]
    r = [-jnp.where(pair_masks[0], low[i], 0.0) for i in npair]
    for mask in pair_masks[1:]:
        dmat = [(r[i] + eye).astype(BF16) for i in npair]
        cd = [_dot(jnp.where(mask, low[i], 0.0).astype(BF16), dmat[i]) for i in npair]
        r = [r[i] - _dot(dmat[i], cd[i].astype(BF16)) for i in npair]
    rhs = [jnp.concatenate([hcols(v_ref, sub, h).astype(F32) * beta[i], kbeta[i] * col(e_gc[sub], h)], axis=1)
           for i, (sub, h) in enumerate(pairs)]
    uw16 = [(rhs[i] + _dot(r[i].astype(BF16), rhs[i].astype(BF16))).astype(BF16) for i in npair]
    kt = [_dot_tn((k[i] * col(e_rest[sub], h)).astype(BF16), uw16[i])
          for i, (sub, h) in enumerate(pairs)]
    qu = [_dot((qk[i] * dec[i]).astype(BF16), uw16[i]) for i in npair]
    qeff = [(qb[i].astype(F32) * col(e_gc[sub], h) - qu[i][:, dk:]).astype(BF16)
            for i, (sub, h) in enumerate(pairs)]
    st = [s_scr[h] for h in heads]
    for sub in subs:
        st16 = [a.astype(BF16) for a in st]
        for h in heads:
            i = sub * GDN_HEADS + h
            o_ref[rows[sub], h * dk:(h + 1) * dk] = _dot(qeff[i], st16[h]) + qu[i][:, :dk]
        st = [st[h] * e_all[sub][0:1, h:h + 1] + kt[sub * GDN_HEADS + h][:, :dk]
              - _dot(kt[sub * GDN_HEADS + h][:, dk:].astype(BF16), st16[h]) for h in heads]
    for h in heads:
        s_scr[h] = st[h]


GDN_CHUNKS_PER_STEP = 2


def _gdn_scan(qkv_all, gb_all, nct_tokens):
    bsz, ttot, _ = qkv_all.shape
    c = GDN_CHUNK * GDN_CHUNKS_PER_STEP
    assert nct_tokens % c == 0 and ttot % c == 0
    nct = nct_tokens // c
    ns = ttot // c
    nlat = ns - nct
    hv = GDN_HEADS * GDN_DK

    def tmap(d, s):
        rev = jnp.where(s < nct, nct - 1 - s, 2 * nct + nlat - 1 - s)
        return jnp.where(d == 0, s, rev)

    def omap(d, s):
        first = jnp.where(d == 0, 0, nlat - 1)
        return jnp.where(s < nct, first, tmap(d, s) - nct)

    return pl.pallas_call(
        _gdn_scan_kernel,
        grid=(bsz, 2, ns),
        in_specs=[pl.BlockSpec((None, c, hv), lambda b, d, s: (b, tmap(d, s), 0)),
                  pl.BlockSpec((None, c, hv), lambda b, d, s: (b, tmap(d, s), 1)),
                  pl.BlockSpec((None, c, hv), lambda b, d, s: (b, tmap(d, s), 2)),
                  pl.BlockSpec((None, None, c, LANE), lambda b, d, s: (d, b, tmap(d, s), 0))],
        out_specs=pl.BlockSpec((None, None, c, hv), lambda b, d, s: (d, b, omap(d, s), 0)),
        out_shape=jax.ShapeDtypeStruct((2, bsz, nlat * c, hv), F32),
        scratch_shapes=[pltpu.VMEM((GDN_HEADS, GDN_DK, GDN_DK), F32)],
        compiler_params=_params(("parallel", "parallel", "arbitrary")),
    )(qkv_all, qkv_all, qkv_all, gb_all)


def _rope_tile(xr, cos, sin):
    lane = lax.broadcasted_iota(I32, xr.shape, 1)
    first_half = (lane % 32) < 16
    rot = jnp.where(first_half, -pltpu.roll(xr, LANE - 16, 1), pltpu.roll(xr, 16, 1))
    return xr * cos + rot * sin


def _mla_q_kernel(c_ref, ga_ref, w_ref, gn_ref, gr_ref, cos_ref, sin_ref, q_ref):
    cn = _rms(c_ref[...].astype(F32), ga_ref[...]).astype(BF16)
    qf = _dot(cn, w_ref[...])
    tm = qf.shape[0]
    lane = lax.broadcasted_iota(I32, (tm, LANE), 1)
    left = lane < ROPE_DIM
    rope_base = MLA_HEADS * NOPE_DIM
    scale = QK_HEAD ** -0.5 * math.log2(math.e)
    cos = cos_ref[...]
    sin = sin_ref[...]
    for hp in range(MLA_HEADS // 2):
        rt = qf[:, rope_base + hp * LANE:rope_base + (hp + 1) * LANE]
        rsq = rt * rt
        ss_left = jnp.sum(jnp.where(left, rsq, 0.0), axis=-1, keepdims=True)
        ss_right = jnp.sum(jnp.where(left, 0.0, rsq), axis=-1, keepdims=True)
        invs = []
        for par, ss_r in ((0, ss_left), (1, ss_right)):
            h = 2 * hp + par
            nope = qf[:, h * NOPE_DIM:(h + 1) * NOPE_DIM]
            ss = jnp.sum(nope * nope, axis=-1, keepdims=True) + ss_r
            inv = lax.rsqrt(ss * (1.0 / QK_HEAD) + EPS)
            invs.append(inv)
            q_ref[h, :, 0:NOPE_DIM] = (nope * inv * gn_ref[...] * scale).astype(BF16)
        inv_lane = jnp.where(left, invs[0], invs[1])
        xr = _rope_tile(rt * inv_lane * gr_ref[...], cos, sin) * scale
        q_ref[2 * hp, :, NOPE_DIM:QK_PAD] = jnp.where(left, xr, 0.0).astype(BF16)
        q_ref[2 * hp + 1, :, NOPE_DIM:QK_PAD] = jnp.where(left, pltpu.roll(xr, ROPE_DIM, 1), 0.0).astype(BF16)


def _mla_queries(p_all, ga, w, gn, gr, cos, sin, nct_tiles, seq):
    bsz = p_all.shape[0]
    tm = SEQ_TILE
    return pl.pallas_call(
        _mla_q_kernel,
        grid=(bsz, seq // tm),
        in_specs=[pl.BlockSpec((None, tm, Q_LORA), lambda b, i: (b, i + nct_tiles, P_CQ // Q_LORA)),
                  pl.BlockSpec(ga.shape, lambda b, i: (0, 0)),
                  pl.BlockSpec(w.shape, lambda b, i: (0, 0)),
                  pl.BlockSpec(gn.shape, lambda b, i: (0, 0)),
                  pl.BlockSpec(gr.shape, lambda b, i: (0, 0)),
                  pl.BlockSpec((tm, LANE), lambda b, i: (i + nct_tiles, 0)),
                  pl.BlockSpec((tm, LANE), lambda b, i: (i + nct_tiles, 0))],
        out_specs=pl.BlockSpec((None, MLA_HEADS, tm, QK_PAD), lambda b, i: (b, 0, i, 0)),
        out_shape=jax.ShapeDtypeStruct((bsz, MLA_HEADS, seq, QK_PAD), BF16),
        compiler_params=_params(("parallel", "parallel")),
    )(p_all, ga, w, gn, gr, cos, sin)


def _mla_kv_kernel(c_ref, kr_ref, ga_ref, w_ref, gn_ref, gr_ref, cos_ref, sin_ref, k_ref, v_ref):
    cn = _rms(c_ref[...].astype(F32), ga_ref[...]).astype(BF16)
    kv = _dot(cn, w_ref[...])
    kr = kr_ref[...].astype(F32)
    ss_r = jnp.sum(kr * kr, axis=-1, keepdims=True)
    kr_rot = _rope_tile(kr * gr_ref[...], cos_ref[...], sin_ref[...])
    width = NOPE_DIM + LANE
    ones_col = (lax.broadcasted_iota(I32, kr.shape, 1) == 0).astype(BF16)
    for h in range(MLA_HEADS):
        nope = kv[:, h * width:h * width + NOPE_DIM]
        ss = jnp.sum(nope * nope, axis=-1, keepdims=True) + ss_r
        inv = lax.rsqrt(ss * (1.0 / QK_HEAD) + EPS)
        k_ref[h, 0:NOPE_DIM, :] = (nope * inv * gn_ref[...]).T.astype(BF16)
        k_ref[h, NOPE_DIM:QK_PAD, :] = (kr_rot * inv).T.astype(BF16)
        v_ref[h, :, 0:LANE] = kv[:, h * width + NOPE_DIM:(h + 1) * width].astype(BF16)
        v_ref[h, :, LANE:2 * LANE] = ones_col


def _mla_keys_values(p_all, ga, w, gn, gr, cos, sin):
    bsz, ttot, _ = p_all.shape
    tm = SEQ_TILE
    return pl.pallas_call(
        _mla_kv_kernel,
        grid=(bsz, ttot // tm),
        in_specs=[pl.BlockSpec((None, tm, KV_LORA), lambda b, i: (b, i, P_CKV // KV_LORA)),
                  pl.BlockSpec((None, tm, LANE), lambda b, i: (b, i, P_KR // LANE)),
                  pl.BlockSpec(ga.shape, lambda b, i: (0, 0)),
                  pl.BlockSpec(w.shape, lambda b, i: (0, 0)),
                  pl.BlockSpec(gn.shape, lambda b, i: (0, 0)),
                  pl.BlockSpec(gr.shape, lambda b, i: (0, 0)),
                  pl.BlockSpec((tm, LANE), lambda b, i: (i, 0)),
                  pl.BlockSpec((tm, LANE), lambda b, i: (i, 0))],
        out_specs=[pl.BlockSpec((None, MLA_HEADS, QK_PAD, tm), lambda b, i: (b, 0, 0, i)),
                   pl.BlockSpec((None, MLA_HEADS, tm, 2 * LANE), lambda b, i: (b, 0, i, 0))],
        out_shape=[jax.ShapeDtypeStruct((bsz, MLA_HEADS, QK_PAD, ttot), BF16),
                   jax.ShapeDtypeStruct((bsz, MLA_HEADS, ttot, 2 * LANE), BF16)],
        compiler_params=_params(("parallel", "parallel")),
    )(p_all, p_all, ga, w, gn, gr, cos, sin)


FLASH_PARTS = 2
FLASH_ROWS = 64


def _flash_kernel(tk, q_ref, k_ref, v_ref, o_ref, s_a, s_b, p_a, p_b, m_scr, a_scr, acc_scr):
    nk = v_ref.shape[0] // tk
    tq = q_ref.shape[0]
    q = q_ref[...]
    m_scr[...] = jnp.full_like(m_scr, -jnp.inf)
    acc_scr[...] = jnp.zeros_like(acc_scr)

    def scores(j):
        return _dot(q, k_ref[:, pl.ds(pl.multiple_of(j * tk, tk), tk)])

    def update(s_ref, p_ref, j):
        vj = v_ref[pl.ds(pl.multiple_of(j * tk, tk), tk), :]
        half = tq // FLASH_PARTS
        for part in range(FLASH_PARTS):
            for r in range(half // FLASH_ROWS):
                r0 = part * half + r * FLASH_ROWS
                rs = slice(r0, r0 + FLASH_ROWS)
                s = s_ref[rs, :]
                m_prev = m_scr[rs, :]
                m_new = jnp.maximum(m_prev, jnp.max(s, axis=-1, keepdims=True))
                m_scr[rs, :] = m_new
                a_scr[rs, :] = jnp.exp2(m_prev - m_new)
                p_ref[rs, :] = jnp.exp2(s - m_new).astype(BF16)
            hs = slice(part * half, (part + 1) * half)
            acc_scr[hs, :] = a_scr[hs, :] * acc_scr[hs, :] + _dot(p_ref[hs, :], vj)

    s_a[...] = scores(0)

    def body(i, carry):
        s_b[...] = scores(2 * i + 1)
        update(s_a, p_a, 2 * i)
        s_a[...] = scores(2 * i + 2)
        update(s_b, p_b, 2 * i + 1)
        return carry

    lax.fori_loop(0, (nk - 1) // 2, body, 0)
    if nk % 2 == 1:
        update(s_a, p_a, nk - 1)
    else:
        s_b[...] = scores(nk - 1)
        update(s_a, p_a, nk - 2)
        update(s_b, p_b, nk - 1)
    o_ref[...] = (acc_scr[:, 0:LANE] / acc_scr[:, LANE:LANE + 1]).astype(o_ref.dtype)


def _key_tile(ttot):
    for cand in (1280, 1024, 768, 512, 256, 128):
        if ttot % cand == 0:
            return cand
    raise ValueError("key length must be a multiple of 128")


def _attention(q, k, v):
    bsz, heads, seq, _ = q.shape
    ttot = v.shape[2]
    tq = 1024
    nq = seq // tq
    tk = _key_tile(ttot)
    return pl.pallas_call(
        functools.partial(_flash_kernel, tk),
        grid=(bsz, heads, nq),
        in_specs=[pl.BlockSpec((None, None, tq, QK_PAD), lambda b, h, i: (b, h, i, 0)),
                  pl.BlockSpec((None, None, QK_PAD, ttot), lambda b, h, i: (b, h, 0, 0),
                               pipeline_mode=pl.Buffered(1)),
                  pl.BlockSpec((None, None, ttot, 2 * LANE), lambda b, h, i: (b, h, 0, 0),
                               pipeline_mode=pl.Buffered(1))],
        out_specs=pl.BlockSpec((None, tq, LANE), lambda b, h, i: (b, i, h)),
        out_shape=jax.ShapeDtypeStruct((bsz, seq, heads * LANE), BF16),
        scratch_shapes=[pltpu.VMEM((tq, tk), F32), pltpu.VMEM((tq, tk), F32),
                        pltpu.VMEM((tq, tk), BF16), pltpu.VMEM((tq, tk), BF16),
                        pltpu.VMEM((tq, 1), F32), pltpu.VMEM((tq, 1), F32),
                        pltpu.VMEM((tq, 2 * LANE), F32)],
        compiler_params=_params(("parallel", "parallel", "arbitrary")),
    )(q, k, v)


def _pack_halves(x):
    w = x.shape[1] // 2
    bits = lax.bitcast_convert_type(x.astype(BF16).astype(F32), U32)
    return jnp.bitwise_or(jnp.right_shift(bits[:, :w], jnp.uint32(16)), bits[:, w:])


def _unpack_halves(wd):
    lo = lax.bitcast_convert_type(jnp.left_shift(wd, jnp.uint32(16)), F32)
    hi = lax.bitcast_convert_type(jnp.bitwise_and(wd, jnp.uint32(0xFFFF0000)), F32)
    return lo, hi


def _ffn_front(xn, gf_ref, sh_ref, sc_ref, wrh_ref, wrl_ref, hp_ref, lg_ref):
    h2 = _rms(xn, gf_ref[...]) * (1.0 + sc_ref[...]) + sh_ref[...]
    hh = h2.astype(BF16)
    hl = (h2 - hh.astype(F32)).astype(BF16)
    hp_ref[...] = _pack_halves(h2)
    both = _dot(hh, wrl_ref[...])
    lg_ref[...] = both[:, 0:LANE] + both[:, LANE:] + _dot(hl, wrh_ref[...])


def _mix_out_kernel(of_ref, ob_ref, z_ref, ym_ref, x_ref, gog_ref, wout_ref, g1_ref,
                    gf_ref, sh_ref, sc_ref, wrh_ref, wrl_ref, xo_ref, hp_ref, lg_ref, mix):
    dv = GDN_DK
    o = of_ref[...] + ob_ref[...]
    for h in range(GDN_HEADS):
        hs = slice(h * dv, (h + 1) * dv)
        y = _rms(o[:, hs], gog_ref[...])
        mix[:, hs] = (y * _silu(z_ref[:, hs].astype(F32))).astype(BF16)
    hv = GDN_HEADS * dv
    mix[:, hv:] = ym_ref[...]
    xn = x_ref[...] + g1_ref[...] * _dot(mix[...], wout_ref[...])
    xo_ref[...] = xn
    _ffn_front(xn, gf_ref, sh_ref, sc_ref, wrh_ref, wrl_ref, hp_ref, lg_ref)


def _mix_out(o2, p_all, ymla, x, gog, wout, g1, gf, sh2, sc2, wrh, wrl, nct_tiles):
    bsz, seq, d = x.shape
    tm = SEQ_TILE
    hv = GDN_HEADS * GDN_DK
    vec = lambda: pl.BlockSpec((None, 1, d), lambda b, i: (b, 0, 0))
    full = lambda a: pl.BlockSpec(a.shape, lambda b, i: (0,) * a.ndim)
    return pl.pallas_call(
        _mix_out_kernel,
        grid=(bsz, seq // tm),
        in_specs=[pl.BlockSpec((None, None, tm, hv), lambda b, i: (0, b, i, 0)),
                  pl.BlockSpec((None, None, tm, hv), lambda b, i: (1, b, i, 0)),
                  pl.BlockSpec((None, tm, hv), lambda b, i: (b, i + nct_tiles, P_Z // hv)),
                  pl.BlockSpec((None, tm, hv), lambda b, i: (b, i, 0)),
                  pl.BlockSpec((None, tm, d), lambda b, i: (b, i, 0)),
                  full(gog), full(wout), vec(), full(gf), vec(), vec(), full(wrh), full(wrl)],
        out_specs=[pl.BlockSpec((None, tm, d), lambda b, i: (b, i, 0)),
                   pl.BlockSpec((None, tm, d // 2), lambda b, i: (b, i, 0)),
                   pl.BlockSpec((None, tm, LANE), lambda b, i: (b, i, 0))],
        out_shape=[jax.ShapeDtypeStruct((bsz, seq, d), F32),
                   jax.ShapeDtypeStruct((bsz, seq, d // 2), U32),
                   jax.ShapeDtypeStruct((bsz, seq, LANE), F32)],
        scratch_shapes=[pltpu.VMEM((tm, wout.shape[0]), BF16)],
        compiler_params=_params(("parallel", "parallel")),
    )(o2, o2, p_all, ymla, x, gog, wout, g1, gf, sh2, sc2, wrh, wrl)


POOL_HALO = 8


def _pool_kernel(seq, xm_ref, xp_ref, xn_ref, gm_ref, sh1_ref, sc1_ref, band_ref, wp_ref, ps_ref, g1_ref,
                 gf_ref, sh_ref, sc_ref, wrh_ref, wrl_ref, xo_ref, hp_ref, lg_ref, ext):
    i = pl.program_id(1)
    n = pl.num_programs(1)
    tm = xm_ref.shape[0]
    hl = POOL_HALO

    def normed(ref):
        return _rms(ref[...], gm_ref[...]) * (1.0 + sc1_ref[...]) + sh1_ref[...]

    ext[hl:hl + tm, :] = normed(xm_ref)
    ext[0:hl, :] = jnp.where(i == 0, 0.0, normed(xp_ref))
    ext[hl + tm:2 * hl + tm, :] = jnp.where(i == n - 1, 0.0, normed(xn_ref))
    ext[2 * hl + tm:, :] = jnp.zeros((ext.shape[0] - 2 * hl - tm, ext.shape[1]), F32)
    t = i * tm + lax.broadcasted_iota(I32, (tm, 1), 0)
    gw = xm_ref.shape[1] // len(POOL_WINDOWS)
    for gi, win in enumerate(POOL_WINDOWS):
        cs = slice(gi * gw, (gi + 1) * gw)
        half = win // 2
        acc = _dot(band_ref[gi], ext[:, cs].astype(BF16))
        lo = jnp.clip(t - half, 0, seq)
        hi = jnp.clip(t - half + win, 0, seq)
        pooled = acc / (hi - lo).astype(F32) - ext[hl:hl + tm, cs]
        y = _dot(pooled.astype(BF16), wp_ref[gi]) * ps_ref[:, cs]
        xo_ref[:, cs] = xm_ref[:, cs] + g1_ref[:, cs] * y
    _ffn_front(xo_ref[...], gf_ref, sh_ref, sc_ref, wrh_ref, wrl_ref, hp_ref, lg_ref)


def _pool_mixer(x, gm, sh1, sc1, wp, ps, g1, gf, sh2, sc2, wrh, wrl):
    bsz, seq, d = x.shape
    tm = SEQ_TILE
    hl = POOL_HALO
    r = tm // hl
    nh = seq // hl
    vec = lambda: pl.BlockSpec((None, 1, d), lambda b, i: (b, 0, 0))
    full = lambda a: pl.BlockSpec(a.shape, lambda b, i: (0,) * a.ndim)
    ext_rows = -(-(tm + 2 * hl) // LANE) * LANE
    offset = jnp.arange(ext_rows, dtype=I32)[None, :] - hl - jnp.arange(tm, dtype=I32)[:, None]
    band = jnp.stack([((offset >= -(w // 2)) & (offset < w - w // 2)).astype(BF16) for w in POOL_WINDOWS])
    return pl.pallas_call(
        functools.partial(_pool_kernel, seq),
        grid=(bsz, seq // tm),
        in_specs=[pl.BlockSpec((None, tm, d), lambda b, i: (b, i, 0)),
                  pl.BlockSpec((None, hl, d), lambda b, i: (b, jnp.maximum(i * r - 1, 0), 0)),
                  pl.BlockSpec((None, hl, d), lambda b, i: (b, jnp.minimum((i + 1) * r, nh - 1), 0)),
                  full(gm), vec(), vec(), full(band), full(wp), full(ps), vec(),
                  full(gf), vec(), vec(), full(wrh), full(wrl)],
        out_specs=[pl.BlockSpec((None, tm, d), lambda b, i: (b, i, 0)),
                   pl.BlockSpec((None, tm, d // 2), lambda b, i: (b, i, 0)),
                   pl.BlockSpec((None, tm, LANE), lambda b, i: (b, i, 0))],
        out_shape=[jax.ShapeDtypeStruct((bsz, seq, d), F32),
                   jax.ShapeDtypeStruct((bsz, seq, d // 2), U32),
                   jax.ShapeDtypeStruct((bsz, seq, LANE), F32)],
        scratch_shapes=[pltpu.VMEM((ext_rows, d), F32)],
        compiler_params=_params(("parallel", "parallel")),
    )(x, x, x, gm, sh1, sc1, band, wp, ps, g1, gf, sh2, sc2, wrh, wrl)


def _first_max(vals, idx, sentinel):
    m = jnp.max(vals, axis=0, keepdims=True)
    first = jnp.min(jnp.where(vals == m, idx, sentinel), axis=0, keepdims=True)
    return m, first


def _route_kernel(rows, lg_ref, bias_ref, dest_ref, gates_ref, cnt_ref, carry):
    phase = pl.program_id(0)
    step = pl.program_id(1)

    @pl.when(jnp.logical_and(phase == 0, step == 0))
    def _():
        carry[...] = jnp.zeros_like(carry)

    @pl.when(jnp.logical_and(phase == 1, step == 0))
    def _():
        shift = int(math.log2(rows))
        counts_row = carry[...].T.astype(I32)
        padded = jnp.left_shift(jnp.right_shift(counts_row + (rows - 1), shift), shift).astype(F32)
        i_exp = lax.broadcasted_iota(I32, padded.shape, 0)
        j_exp = lax.broadcasted_iota(I32, padded.shape, 1)
        first_row = jnp.sum(jnp.where(j_exp < i_exp, padded, 0.0), axis=1, keepdims=True)
        carry[...] = jnp.broadcast_to(first_row, carry.shape)

    tm = lg_ref.shape[0]
    epg = EXPERTS_PER_GROUP
    scores = jax.nn.sigmoid(lg_ref[...].T[0:N_EXPERTS, :])
    biased = scores + bias_ref[:, 0:1]
    eidx = lax.broadcasted_iota(I32, (epg, tm), 0)
    best_score = best_grp = best_i1 = best_i2 = None
    for g in range(N_GROUPS):
        blk = biased[g * epg:(g + 1) * epg, :]
        m1, i1 = _first_max(blk, eidx, epg)
        m2, i2 = _first_max(jnp.where(eidx == i1, -jnp.inf, blk), eidx, epg)
        gs = m1 + m2
        if g == 0:
            best_score, best_grp, best_i1, best_i2 = gs, jnp.zeros_like(i1), i1, i2
        else:
            better = gs > best_score
            best_score = jnp.where(better, gs, best_score)
            best_grp = jnp.where(better, g, best_grp)
            best_i1 = jnp.where(better, i1, best_i1)
            best_i2 = jnp.where(better, i2, best_i2)
    e1 = best_grp * epg + best_i1
    e2 = best_grp * epg + best_i2
    eall = lax.broadcasted_iota(I32, (LANE, tm), 0)
    oh1 = eall == e1
    oh2 = eall == e2
    oh = jnp.logical_or(oh1, oh2).astype(BF16)
    picked = jnp.sum(oh.astype(F32), axis=1, keepdims=True)

    @pl.when(phase == 0)
    def _():
        carry[...] = carry[...] + picked
        cnt_ref[...] = carry[...]

    @pl.when(phase == 1)
    def _():
        s1 = jnp.sum(jnp.where(oh1[0:N_EXPERTS], scores, 0.0), axis=0, keepdims=True)
        s2 = jnp.sum(jnp.where(oh2[0:N_EXPERTS], scores, 0.0), axis=0, keepdims=True)
        denom = s1 + s2
        earlier = (lax.broadcasted_iota(I32, (tm, tm), 0) < lax.broadcasted_iota(I32, (tm, tm), 1)).astype(BF16)
        row = _dot(oh, earlier) + carry[:, 0:1]
        d1 = jnp.sum(jnp.where(oh1, row, 0.0), axis=0, keepdims=True)
        d2 = jnp.sum(jnp.where(oh2, row, 0.0), axis=0, keepdims=True)
        carry[...] = carry[...] + picked
        dest_ref[...] = jnp.concatenate([d1.astype(I32), d2.astype(I32), jnp.zeros((6, tm), I32)], axis=0)
        gates = jnp.concatenate([s1 / denom, s2 / denom, jnp.zeros((LANE - 2, tm), F32)], axis=0)
        gates_ref[...] = gates.T


def _route(logits, bias_col):
    n = logits.shape[0]
    tm = ROUTE_TILE
    rows = MOE_ROWS
    assert rows & (rows - 1) == 0
    return pl.pallas_call(
        functools.partial(_route_kernel, rows),
        grid=(2, n // tm),
        in_specs=[pl.BlockSpec((tm, LANE), lambda p, t: (t, 0)),
                  pl.BlockSpec(bias_col.shape, lambda p, t: (0, 0))],
        out_specs=[pl.BlockSpec((8, tm), lambda p, t: (0, p * t)),
                   pl.BlockSpec((tm, LANE), lambda p, t: (p * t, 0)),
                   pl.BlockSpec((LANE, LANE), lambda p, t: (0, 0))],
        out_shape=[jax.ShapeDtypeStruct((8, n), I32),
                   jax.ShapeDtypeStruct((n, LANE), F32),
                   jax.ShapeDtypeStruct((LANE, LANE), F32)],
        scratch_shapes=[pltpu.VMEM((LANE, LANE), F32)],
        compiler_params=_params(("arbitrary", "arbitrary")),
    )(logits, bias_col)


def _dispatch_copy(hp_ref, xs_hbm, sem, j, dst):
    return pltpu.make_async_copy(hp_ref.at[pl.ds(j, 1), :], xs_hbm.at[pl.ds(dst, 1), :], sem)


def _zero_block_copy(zbuf, xs_hbm, zsem, start):
    return pltpu.make_async_copy(zbuf, xs_hbm.at[pl.ds(start, zbuf.shape[0]), :], zsem)


def _dispatch_kernel(tm, pad_end_ref, nu_ref, d0_ref, d1_ref, hp_ref, xs_hbm, zbuf, sem, zsem):
    rows = zbuf.shape[0]
    n_blocks = xs_hbm.shape[0] // rows

    @pl.when(pl.program_id(0) == 0)
    def _():
        zbuf[...] = jnp.zeros_like(zbuf)

        def expert_tail(e, carry):
            start = jnp.maximum(pad_end_ref[e] - rows, 0)
            _zero_block_copy(zbuf, xs_hbm, zsem, pl.multiple_of(start, rows)).start()
            return carry

        lax.fori_loop(0, N_EXPERTS, expert_tail, 0)

        def unused_block(b, carry):
            _zero_block_copy(zbuf, xs_hbm, zsem, pl.multiple_of(b * rows, rows)).start()
            return carry

        lax.fori_loop(nu_ref[0], n_blocks, unused_block, 0)

        def drain(b, carry):
            _zero_block_copy(zbuf, xs_hbm, zsem, 0).wait()
            return carry

        lax.fori_loop(0, N_EXPERTS + n_blocks - nu_ref[0], drain, 0)

    def issue(j, carry):
        _dispatch_copy(hp_ref, xs_hbm, sem, j, d0_ref[0, j]).start()
        _dispatch_copy(hp_ref, xs_hbm, sem, j, d1_ref[0, j]).start()
        return carry

    lax.fori_loop(0, tm, issue, 0, unroll=DMA_UNROLL)
    for _ in range(2):
        pltpu.make_async_copy(hp_ref, xs_hbm.at[pl.ds(0, tm), :], sem).wait()


def _dispatch(hp, dest, pad_end, n_used, n_rows):
    n, half = hp.shape
    tm = ROUTE_TILE
    nt = n // tm
    idx = lambda k: pl.BlockSpec((None, None, 1, tm), lambda i, pe, nu: (k, i, 0, 0), memory_space=pltpu.SMEM)
    dest4 = dest.reshape(dest.shape[0], nt, 1, tm)
    grid_spec = pltpu.PrefetchScalarGridSpec(
        num_scalar_prefetch=2,
        grid=(nt,),
        in_specs=[idx(0), idx(1), pl.BlockSpec((tm, half), lambda i, pe, nu: (i, 0))],
        out_specs=pl.BlockSpec(memory_space=pl.ANY),
        scratch_shapes=[pltpu.VMEM((MOE_ROWS, half), U32), pltpu.SemaphoreType.DMA(()),
                        pltpu.SemaphoreType.DMA(())],
    )
    return pl.pallas_call(
        functools.partial(_dispatch_kernel, tm),
        grid_spec=grid_spec,
        out_shape=jax.ShapeDtypeStruct((n_rows, half), U32),
        compiler_params=_params(("arbitrary",)),
    )(pad_end, n_used, dest4, dest4, hp)


def _expert_weight_copies(layer, wgu_hbm, wd_hbm, wgu32, wd32, sem, expert, slot):
    return (pltpu.make_async_copy(wgu_hbm.at[layer, expert], wgu32.at[slot], sem.at[0, slot]),
            pltpu.make_async_copy(wd_hbm.at[layer, expert], wd32.at[slot], sem.at[1, slot]))


def _experts_kernel(layer, be_ref, nu_ref, next_ref, slot_ref, xs_ref, wgu_hbm, wd_hbm, y_ref,
                    wgu32, wd32, wgu16, wd16, sem):
    i = pl.program_id(0)
    expert = be_ref[i]
    copies = functools.partial(_expert_weight_copies, layer, wgu_hbm, wd_hbm, wgu32, wd32, sem)

    @pl.when(i < nu_ref[0])
    def _():
        @pl.when(jnp.logical_or(i == 0, expert != be_ref[jnp.maximum(i - 1, 0)]))
        def _():
            slot = slot_ref[expert]

            @pl.when(i == 0)
            def _():
                for c in copies(expert, slot):
                    c.start()

            for c in copies(expert, slot):
                c.wait()
            wgu16[...] = wgu32[slot].astype(BF16)
            wd16[...] = wd32[slot].astype(BF16)
            following = next_ref[expert]

            @pl.when(following >= 0)
            def _():
                for c in copies(following, 1 - slot):
                    c.start()

        lo, hi = _unpack_halves(xs_ref[...])
        half = lo.shape[1]
        gu = _dot(lo.astype(BF16), wgu16[0:half, :]) + _dot(hi.astype(BF16), wgu16[half:, :])
        f = gu.shape[1] // 2
        hid = _silu(gu[:, :f]) * gu[:, f:]
        y_ref[...] = _pack_halves(_dot(hid.astype(BF16), wd16[...]))

    @pl.when(i >= nu_ref[0])
    def _():
        y_ref[...] = jnp.zeros_like(y_ref)


def _experts(layer, block_expert, n_used, next_expert, weight_slot, xs, w_gate_up, w_down):
    n_rows, half = xs.shape
    rows = MOE_ROWS
    d = 2 * half
    f2 = w_gate_up.shape[3]
    grid_spec = pltpu.PrefetchScalarGridSpec(
        num_scalar_prefetch=4,
        grid=(n_rows // rows,),
        in_specs=[pl.BlockSpec((rows, half), lambda i, *_: (i, 0)),
                  pl.BlockSpec(memory_space=pl.ANY),
                  pl.BlockSpec(memory_space=pl.ANY)],
        out_specs=pl.BlockSpec((rows, half), lambda i, *_: (i, 0)),
        scratch_shapes=[pltpu.VMEM((2, d, f2), F32), pltpu.VMEM((2, f2 // 2, d), F32),
                        pltpu.VMEM((d, f2), BF16), pltpu.VMEM((f2 // 2, d), BF16),
                        pltpu.SemaphoreType.DMA((2, 2))],
    )
    return pl.pallas_call(
        functools.partial(_experts_kernel, layer),
        grid_spec=grid_spec,
        out_shape=jax.ShapeDtypeStruct((n_rows, half), U32),
        compiler_params=_params(("arbitrary",)),
    )(block_expert, n_used, next_expert, weight_slot, xs, w_gate_up, w_down)


def _combine_copy(y_hbm, buf, sem, slot, k, src, j):
    return pltpu.make_async_copy(y_hbm.at[pl.ds(src, 1), :], buf.at[slot, k, pl.ds(j, 1), :], sem.at[slot])


def _combine_kernel(tm, d0_ref, d1_ref, d0n_ref, d1n_ref, y_hbm, gates_ref, x_ref, g2_ref, xo_ref, buf, sem):
    step = pl.program_id(0) * pl.num_programs(1) + pl.program_id(1)
    total = pl.num_programs(0) * pl.num_programs(1)
    slot = step % 2

    def start_gather(r0_ref, r1_ref, s):
        def issue(j, carry):
            _combine_copy(y_hbm, buf, sem, s, 0, r0_ref[0, j], j).start()
            _combine_copy(y_hbm, buf, sem, s, 1, r1_ref[0, j], j).start()
            return carry
        lax.fori_loop(0, tm, issue, 0, unroll=DMA_UNROLL)

    @pl.when(step == 0)
    def _():
        start_gather(d0_ref, d1_ref, 0)

    @pl.when(step + 1 < total)
    def _():
        start_gather(d0n_ref, d1n_ref, 1 - slot)

    for k in range(2):
        pltpu.make_async_copy(y_hbm.at[pl.ds(0, tm), :], buf.at[slot, k], sem.at[slot]).wait()
    lo0, hi0 = _unpack_halves(buf[slot, 0])
    lo1, hi1 = _unpack_halves(buf[slot, 1])
    half = lo0.shape[1]
    ga = gates_ref[:, 0:1]
    gb = gates_ref[:, 1:2]
    xo_ref[:, 0:half] = x_ref[:, 0:half] + g2_ref[:, 0:half] * (ga * lo0 + gb * lo1)
    xo_ref[:, half:] = x_ref[:, half:] + g2_ref[:, half:] * (ga * hi0 + gb * hi1)


def _combine(y, dest, gates, x, g2):
    bsz, seq, d = x.shape
    tm = SEQ_TILE
    nt = seq // tm
    half = d // 2
    last = bsz * nt - 1
    idx = lambda k, ahead: pl.BlockSpec(
        (None, None, 1, tm), lambda b, i: (k, jnp.minimum(b * nt + i + ahead, last), 0, 0),
        memory_space=pltpu.SMEM)
    dest4 = dest.reshape(dest.shape[0], bsz * nt, 1, tm)
    return pl.pallas_call(
        functools.partial(_combine_kernel, tm),
        grid=(bsz, nt),
        in_specs=[idx(0, 0), idx(1, 0), idx(0, 1), idx(1, 1),
                  pl.BlockSpec(memory_space=pl.ANY),
                  pl.BlockSpec((tm, LANE), lambda b, i: (b * nt + i, 0)),
                  pl.BlockSpec((None, tm, d), lambda b, i: (b, i, 0)),
                  pl.BlockSpec((None, 1, d), lambda b, i: (b, 0, 0))],
        out_specs=pl.BlockSpec((None, tm, d), lambda b, i: (b, i, 0)),
        out_shape=jax.ShapeDtypeStruct((bsz, seq, d), F32),
        scratch_shapes=[pltpu.VMEM((2, 2, tm, half), U32), pltpu.SemaphoreType.DMA((2,))],
        compiler_params=_params(("arbitrary", "arbitrary")),
    )(dest4, dest4, dest4, dest4, y, gates, x, g2)


def _moe(layer, x, hp, logits, g2, bias_col, w_gate_up, w_down):
    bsz, seq, d = x.shape
    n = bsz * seq
    rows = MOE_ROWS
    dest, gates, cnt = _route(logits.reshape(n, LANE), bias_col)
    counts = cnt[:N_EXPERTS, 0].astype(I32)
    pad_end = jnp.cumsum((counts + rows - 1) // rows * rows)
    n_rows = -(-(n * TOP_K + N_EXPERTS * (rows - 1)) // rows) * rows
    block_start = jnp.arange(n_rows // rows, dtype=I32) * rows
    block_expert = jnp.minimum(jnp.sum((pad_end[None, :] <= block_start[:, None]).astype(I32), axis=1),
                               N_EXPERTS - 1)
    n_used = pad_end[-1:] // rows
    owns = counts > 0
    eid = jnp.arange(N_EXPERTS, dtype=I32)
    later = jnp.where(owns[None, :] & (eid[None, :] > eid[:, None]), eid[None, :], N_EXPERTS)
    next_expert = jnp.min(later, axis=1)
    next_expert = jnp.where(next_expert < N_EXPERTS, next_expert, -1).astype(I32)
    weight_slot = ((jnp.cumsum(owns.astype(I32)) - 1) % 2).astype(I32)
    xs = _dispatch(hp.reshape(n, d // 2), dest, pad_end, n_used, n_rows)
    y = _experts(layer, block_expert, n_used, next_expert, weight_slot, xs, w_gate_up, w_down)
    return _combine(y, dest, gates, x, g2)


def _permute_w_in(w_in):
    d = w_in.shape[0]
    hq = GDN_HEADS * GDN_DK
    off_a = 4 * hq
    off_cq = off_a + 4 * GDN_HEADS
    off_ckv = off_cq + Q_LORA
    off_kr = off_ckv + KV_LORA
    z = lambda n: jnp.zeros((d, n), w_in.dtype)
    a = lambda k: w_in[:, off_a + k * GDN_HEADS:off_a + (k + 1) * GDN_HEADS]
    pad_ab = LANE - 2 * GDN_HEADS
    cols = [w_in[:, :off_a], w_in[:, off_cq:off_kr + ROPE_DIM], z(LANE - ROPE_DIM),
            a(0), a(2), z(pad_ab), a(1), a(3), z(pad_ab)]
    w = jnp.concatenate(cols, axis=1)
    return jnp.concatenate([w, z(P_WIDTH - w.shape[1])], axis=1).astype(BF16)


def _rope_tables(rows, n_ctx):
    row = jnp.repeat(jnp.arange(rows, dtype=F32), GRID_W)
    col = jnp.tile(jnp.arange(GRID_W, dtype=F32), rows)
    pairs = ROPE_DIM // 4
    inv_freq = ROPE_THETA ** (-jnp.arange(pairs, dtype=F32) / pairs)
    ang_r = row[:, None] * inv_freq
    ang_c = col[:, None] * inv_freq
    ang = jnp.concatenate([ang_r, ang_r, ang_c, ang_c], axis=-1)
    ang = jnp.concatenate([jnp.zeros((n_ctx, ROPE_DIM), F32), ang], axis=0)
    cos, sin = jnp.cos(ang), jnp.sin(ang)
    return jnp.concatenate([cos, cos], axis=1), jnp.concatenate([sin, sin], axis=1)


def kernel(x, c, ctx, c_ctx, w_mod, b_mod, norm_mix_g, norm_ffn_g, w_in, conv_qkv, a_log_fwd, a_log_bwd, dt_bias_fwd, dt_bias_bwd, gdn_out_g, q_a_norm_g, w_uq, kv_a_norm_g, w_ukv, q_norm_g, k_norm_g, w_out, w_pool, pool_scale, w_router, router_bias, w_gate_up, w_down):
    bsz, seq, d = x.shape
    n_ctx = ctx.shape[1]
    depth = w_mod.shape[0]
    assert depth == 2, "the context stream is only read: no layer after the first even layer reads it"
    nct_tiles = n_ctx // SEQ_TILE

    cc = jnp.concatenate([c, c_ctx[None, :], jnp.zeros((8 - bsz - 1, d), F32)], axis=0)
    mod = _modulation(cc, w_mod, b_mod).reshape(depth, 8, 6, d)

    def mods(layer):
        m = mod[layer]
        return [m[:bsz, k][:, None, :] for k in range(6)], [m[bsz:bsz + 1, k] for k in range(6)]

    wr = jnp.pad(w_router, ((0, 0), (0, LANE - N_EXPERTS)))
    wrh = wr.astype(BF16)
    wrl = jnp.concatenate([wrh, (wr - wrh.astype(F32)).astype(BF16)], axis=1)
    bias_col = jnp.broadcast_to(router_bias.astype(F32)[:, None], (N_EXPERTS, LANE))
    row = lambda v: v.astype(F32)[None, :]

    for layer in range(depth):
        j = layer // 2
        (sh1, sc1, g1, sh2, sc2, g2), (csh1, csc1, _, _, _, _) = mods(layer)
        gm = row(norm_mix_g[layer])
        gf = row(norm_ffn_g[layer])
        if layer % 2 == 0:
            p_all = _in_projection(ctx, x, gm, csh1, csc1, sh1, sc1, _permute_w_in(w_in[j]))
            conv_w = conv_qkv[j].astype(F32)
            gate_params = jnp.zeros((2, 8, LANE), F32)
            gate_params = gate_params.at[0, 0, :GDN_HEADS].set(a_log_fwd[j]).at[0, 1, :GDN_HEADS].set(dt_bias_fwd[j])
            gate_params = gate_params.at[1, 0, :GDN_HEADS].set(a_log_bwd[j]).at[1, 1, :GDN_HEADS].set(dt_bias_bwd[j])
            qkv_all, gb_all = _gdn_front(p_all, conv_w, gate_params, nct_tiles)
            o2 = _gdn_scan(qkv_all, gb_all, n_ctx)

            cos, sin = _rope_tables(seq // GRID_W, n_ctx)
            gq = q_norm_g[j].astype(F32)
            gk = k_norm_g[j].astype(F32)
            wq = w_uq[j].reshape(Q_LORA, MLA_HEADS, QK_HEAD)
            wq = jnp.concatenate([wq[:, :, :NOPE_DIM].reshape(Q_LORA, -1),
                                  wq[:, :, NOPE_DIM:].reshape(Q_LORA, -1)], axis=1).astype(BF16)
            q = _mla_queries(p_all, row(q_a_norm_g[j]), wq, gq[None, :NOPE_DIM],
                             jnp.tile(gq[NOPE_DIM:], 2)[None, :], cos, sin, nct_tiles, seq)
            gk_rope = jnp.concatenate([gk[NOPE_DIM:], jnp.zeros((LANE - ROPE_DIM,), F32)])[None, :]
            k_all, v_all = _mla_keys_values(p_all, row(kv_a_norm_g[j]), w_ukv[j].astype(BF16),
                                            gk[None, :NOPE_DIM], gk_rope, cos, sin)
            ymla = _attention(q, k_all, v_all)
            x, hp, logits = _mix_out(o2, p_all, ymla, x, row(gdn_out_g[j]), w_out[j].astype(BF16), g1,
                                     gf, sh2, sc2, wrh, wrl, nct_tiles)
        else:
            x, hp, logits = _pool_mixer(x, gm, sh1, sc1, w_pool[j].astype(BF16), row(pool_scale[j]), g1,
                                        gf, sh2, sc2, wrh, wrl)
        x = _moe(layer, x, hp, logits, g2, bias_col, w_gate_up, w_down)
    return x
```

```python
import functools
import math

import jax
import jax.numpy as jnp
from jax import lax
from jax.experimental import pallas as pl
from jax.experimental.pallas import tpu as pltpu

F32 = jnp.float32
BF16 = jnp.bfloat16
U32 = jnp.uint32
I32 = jnp.int32

EPS = 1e-6
GRID_W = 64
GDN_HEADS = 8
GDN_DK = 128
MLA_HEADS = 8
NOPE_DIM = 128
ROPE_DIM = 64
QK_HEAD = NOPE_DIM + ROPE_DIM
QK_PAD = 256
ROPE_THETA = 10000.0
Q_LORA = 512
KV_LORA = 256
POOL_WINDOWS = (2, 4, 8, 16)
N_EXPERTS = 64
EXPERTS_PER_GROUP = 8
N_GROUPS = N_EXPERTS // EXPERTS_PER_GROUP
TOP_K = 2

LANE = 128
V7X_VMEM_LIMIT = 56 * 1024 * 1024

SEQ_TILE = 256
GDN_CHUNK = 128
MOE_ROWS = 256
DMA_UNROLL = 8
ROUTE_TILE = 512

P_QKV = 0
P_Z = 3072
P_CQ = 4096
P_CKV = 4608
P_KR = 4864
P_AB = 4992
P_WIDTH = 5376
P_NTILE = 5376


def _params(sem):
    return pltpu.CompilerParams(dimension_semantics=sem, vmem_limit_bytes=V7X_VMEM_LIMIT)


def _dot(a, b):
    return jnp.dot(a, b, preferred_element_type=F32)


def _dot_nt(a, b):
    return lax.dot_general(a, b, (((1,), (1,)), ((), ())), preferred_element_type=F32)


def _dot_tn(a, b):
    return lax.dot_general(a, b, (((0,), (0,)), ((), ())), preferred_element_type=F32)


def _silu(x):
    return x * jax.nn.sigmoid(x)


def _rms(x, g):
    return x * lax.rsqrt(jnp.mean(x * x, axis=-1, keepdims=True) + EPS) * g


def _split3(x):
    hi = x.astype(BF16)
    r = x - hi.astype(F32)
    mid = r.astype(BF16)
    lo = (r - mid.astype(F32)).astype(BF16)
    return hi, mid, lo


def _mod_kernel(c_ref, w_ref, b_ref, o_ref):
    s = _silu(c_ref[...]).astype(BF16)
    o_ref[...] = _dot(s, w_ref[...].astype(BF16)) + b_ref[...]


def _modulation(cc, w_mod, b_mod):
    depth, d, n6 = w_mod.shape
    tn = 1024
    return pl.pallas_call(
        _mod_kernel,
        grid=(depth, n6 // tn),
        in_specs=[pl.BlockSpec((8, d), lambda l, j: (0, 0)),
                  pl.BlockSpec((None, d, tn), lambda l, j: (l, 0, j)),
                  pl.BlockSpec((None, 1, tn), lambda l, j: (l, 0, j))],
        out_specs=pl.BlockSpec((None, 8, tn), lambda l, j: (l, 0, j)),
        out_shape=jax.ShapeDtypeStruct((depth, 8, n6), F32),
        compiler_params=_params(("parallel", "parallel")),
    )(cc, w_mod, b_mod.reshape(depth, 1, n6))


def _inproj_kernel(nct, xc_ref, x_ref, g_ref, shc_ref, scc_ref, sh_ref, sc_ref, w_ref, o_ref):
    i = pl.program_id(2)

    def run(xr, shr, scr):
        h = _rms(xr[...], g_ref[...]) * (1.0 + scr[...]) + shr[...]
        o_ref[...] = _dot(h.astype(BF16), w_ref[...]).astype(o_ref.dtype)

    @pl.when(i < nct)
    def _():
        run(xc_ref, shc_ref, scc_ref)

    @pl.when(i >= nct)
    def _():
        run(x_ref, sh_ref, sc_ref)


def _in_projection(ctx, x, g, shc, scc, sh, sc, w):
    bsz, seq, d = x.shape
    tm = SEQ_TILE
    nct = ctx.shape[1] // tm
    nt = nct + seq // tm
    width = w.shape[1]
    tn = P_NTILE
    return pl.pallas_call(
        functools.partial(_inproj_kernel, nct),
        grid=(width // tn, bsz, nt),
        in_specs=[pl.BlockSpec((None, tm, d), lambda j, b, i: (b, jnp.minimum(i, nct - 1), 0)),
                  pl.BlockSpec((None, tm, d), lambda j, b, i: (b, jnp.maximum(i - nct, 0), 0)),
                  pl.BlockSpec((1, d), lambda j, b, i: (0, 0)),
                  pl.BlockSpec((1, d), lambda j, b, i: (0, 0)),
                  pl.BlockSpec((1, d), lambda j, b, i: (0, 0)),
                  pl.BlockSpec((None, 1, d), lambda j, b, i: (b, 0, 0)),
                  pl.BlockSpec((None, 1, d), lambda j, b, i: (b, 0, 0)),
                  pl.BlockSpec((d, tn), lambda j, b, i: (0, j), pipeline_mode=pl.Buffered(1))],
        out_specs=pl.BlockSpec((None, tm, tn), lambda j, b, i: (b, i, j)),
        out_shape=jax.ShapeDtypeStruct((bsz, nt * tm, width), BF16),
        compiler_params=_params(("parallel", "parallel", "parallel")),
    )(ctx, x, g, shc, scc, sh, sc, w)


GDN_HALO = 16


def _gdn_front_kernel(nct, pm_ref, pp_ref, pn_ref, cw_ref, ab0_ref, ab1_ref, gp_ref,
                      qkv_ref, gb_ref, ext):
    i = pl.program_id(1)
    n = pl.num_programs(1)
    tm = pm_ref.shape[0]
    hl = GDN_HALO
    zero_prev = jnp.logical_or(i == 0, i == nct)
    zero_next = jnp.logical_or(i == nct - 1, i == n - 1)
    ext[hl:hl + tm, :] = pm_ref[...].astype(F32)
    ext[0:hl, :] = jnp.where(zero_prev, 0.0, pp_ref[...].astype(F32))
    ext[hl + tm:2 * hl + tm, :] = jnp.where(zero_next, 0.0, pn_ref[...].astype(F32))
    taps = cw_ref.shape[0]
    nqk = 2 * GDN_HEADS
    for cb in range(qkv_ref.shape[1] // LANE):
        cs = slice(cb * LANE, (cb + 1) * LANE)
        acc = None
        for j in range(taps):
            off = hl - taps // 2 + j
            term = cw_ref[j:j + 1, cs] * ext[off:off + tm, cs]
            acc = term if acc is None else acc + term
        y = _silu(acc)
        if cb < nqk:
            inv = lax.rsqrt(jnp.sum(y * y, axis=-1, keepdims=True) + EPS)
            if cb < GDN_HEADS:
                inv = inv * (GDN_DK ** -0.5)
            y = y * inv
        qkv_ref[:, cs] = y.astype(BF16)
    for d, ab_ref in enumerate((ab0_ref, ab1_ref)):
        a = ab_ref[...].astype(F32)
        lane = lax.broadcasted_iota(I32, a.shape, 1)
        neg_decay = -jnp.exp(gp_ref[d, 0:1, :])
        xx = a + gp_ref[d, 1:2, :]
        softplus = jnp.maximum(xx, 0.0) + jnp.log1p(jnp.exp(-jnp.abs(xx)))
        gate = jnp.where(lane < GDN_HEADS, neg_decay * softplus, jax.nn.sigmoid(a))
        gb_ref[d] = jnp.where(lane < 2 * GDN_HEADS, gate, 0.0)


def _gdn_front(p_all, conv_w, gate_params, nct):
    bsz, ttot, _ = p_all.shape
    tm = SEQ_TILE
    nt = ttot // tm
    cq = P_Z
    hl = GDN_HALO
    r = tm // hl
    nh = ttot // hl
    return pl.pallas_call(
        functools.partial(_gdn_front_kernel, nct),
        grid=(bsz, nt),
        in_specs=[pl.BlockSpec((None, tm, cq), lambda b, i: (b, i, 0)),
                  pl.BlockSpec((None, hl, cq), lambda b, i: (b, jnp.maximum(i * r - 1, 0), 0)),
                  pl.BlockSpec((None, hl, cq), lambda b, i: (b, jnp.minimum((i + 1) * r, nh - 1), 0)),
                  pl.BlockSpec(conv_w.shape, lambda b, i: (0, 0)),
                  pl.BlockSpec((None, tm, LANE), lambda b, i: (b, i, P_AB // LANE)),
                  pl.BlockSpec((None, tm, LANE), lambda b, i: (b, i, P_AB // LANE + 1)),
                  pl.BlockSpec(gate_params.shape, lambda b, i: (0, 0, 0))],
        out_specs=[pl.BlockSpec((None, tm, cq), lambda b, i: (b, i, 0)),
                   pl.BlockSpec((2, None, tm, LANE), lambda b, i: (0, b, i, 0))],
        out_shape=[jax.ShapeDtypeStruct((bsz, ttot, cq), BF16),
                   jax.ShapeDtypeStruct((2, bsz, ttot, LANE), F32)],
        scratch_shapes=[pltpu.VMEM((tm + 2 * hl, cq), F32)],
        compiler_params=_params(("parallel", "parallel")),
    )(p_all, p_all, p_all, conv_w, p_all, p_all, gate_params)


def _gdn_scan_kernel(q_ref, k_ref, v_ref, gb_ref, o_ref, s_scr):
    d = pl.program_id(1)
    s = pl.program_id(2)
    c = GDN_CHUNK
    n_sub = q_ref.shape[0] // c
    dk = GDN_DK

    @pl.when(s == 0)
    def _():
        s_scr[...] = jnp.zeros_like(s_scr)

    ri = lax.broadcasted_iota(I32, (c, c), 0)
    ci = lax.broadcasted_iota(I32, (c, c), 1)
    incl = jnp.where(d == 0, ri - ci, ci - ri) >= 0
    strict = jnp.logical_and(incl, ci != ri)
    m_incl = incl.astype(BF16)
    ones = jnp.ones((c, c), BF16)
    eye = (ri == ci).astype(F32)
    pair_masks = []
    for lb in range(int(math.log2(c))):
        same_pair = jnp.right_shift(ri, lb + 1) == jnp.right_shift(ci, lb + 1)
        same_block = jnp.right_shift(ri, lb) == jnp.right_shift(ci, lb)
        pair_masks.append(jnp.logical_and(same_pair, jnp.logical_not(same_block)))
    heads = range(GDN_HEADS)
    col = lambda a, h: a[:, h:h + 1]

    rows = [pl.ds(pl.multiple_of(jnp.where(d == 0, sub, n_sub - 1 - sub) * c, c), c) for sub in range(n_sub)]
    subs = range(n_sub)
    pairs = [(sub, h) for sub in subs for h in heads]
    hcols = lambda ref, sub, h: ref[rows[sub], h * dk:(h + 1) * dk]
    gb = [gb_ref[rows[sub], :] for sub in subs]
    lane = lax.broadcasted_iota(I32, gb[0].shape, 1)
    gsplit = [_split3(jnp.where(lane < GDN_HEADS, gb[sub], 0.0)) for sub in subs]
    gc = [sum(_dot(m_incl, t) for t in gsplit[sub]) for sub in subs]
    gl = [sum(_dot(ones, t) for t in gsplit[sub]) for sub in subs]
    gc_t = [gc[sub].T for sub in subs]
    e_gc = [jnp.exp(gc[sub]) for sub in subs]
    e_rest = [jnp.exp(gl[sub] - gc[sub]) for sub in subs]
    e_all = [jnp.exp(gl[sub]) for sub in subs]
    qb = [hcols(q_ref, sub, h) for sub, h in pairs]
    kb16 = [hcols(k_ref, sub, h) for sub, h in pairs]
    k = [a.astype(F32) for a in kb16]
    beta = [col(gb[sub], GDN_HEADS + h) for sub, h in pairs]
    npair = range(len(pairs))
    kbeta = [k[i] * beta[i] for i in npair]
    kk = [_dot_nt(kbeta[i].astype(BF16), kb16[i]) for i in npair]
    qk = [_dot_nt(qb[i], kb16[i]) for i in npair]
    dec = [jnp.where(incl, jnp.exp(jnp.where(incl, col(gc[sub], h) - gc_t[sub][h:h + 1, :], 0.0)), 0.0)
           for sub, h in pairs]
    low = [jnp.where(strict, kk[i] * dec[i], 0.0) for i in npair]
    r = [-jnp.where(pair_masks[0], low[i], 0.0) for i in npair]
    for mask in pair_masks[1:]:
        dmat = [(r[i] + eye).astype(BF16) for i in npair]
        cd = [_dot(jnp.where(mask, low[i], 0.0).astype(BF16), dmat[i]) for i in npair]
        r = [r[i] - _dot(dmat[i], cd[i].astype(BF16)) for i in npair]
    rhs = [jnp.concatenate([hcols(v_ref, sub, h).astype(F32) * beta[i], kbeta[i] * col(e_gc[sub], h)], axis=1)
           for i, (sub, h) in enumerate(pairs)]
    uw16 = [(rhs[i] + _dot(r[i].astype(BF16), rhs[i].astype(BF16))).astype(BF16) for i in npair]
    kt = [_dot_tn((k[i] * col(e_rest[sub], h)).astype(BF16), uw16[i])
          for i, (sub, h) in enumerate(pairs)]
    qu = [_dot((qk[i] * dec[i]).astype(BF16), uw16[i]) for i in npair]
    qeff = [(qb[i].astype(F32) * col(e_gc[sub], h) - qu[i][:, dk:]).astype(BF16)
            for i, (sub, h) in enumerate(pairs)]
    st = [s_scr[h] for h in heads]
    for sub in subs:
        st16 = [a.astype(BF16) for a in st]
        for h in heads:
            i = sub * GDN_HEADS + h
            o_ref[rows[sub], h * dk:(h + 1) * dk] = _dot(qeff[i], st16[h]) + qu[i][:, :dk]
        st = [st[h] * e_all[sub][0:1, h:h + 1] + kt[sub * GDN_HEADS + h][:, :dk]
              - _dot(kt[sub * GDN_HEADS + h][:, dk:].astype(BF16), st16[h]) for h in heads]
    for h in heads:
        s_scr[h] = st[h]


GDN_CHUNKS_PER_STEP = 2


def _gdn_scan(qkv_all, gb_all, nct_tokens):
    bsz, ttot, _ = qkv_all.shape
    c = GDN_CHUNK * GDN_CHUNKS_PER_STEP
    assert nct_tokens % c == 0 and ttot % c == 0
    nct = nct_tokens // c
    ns = ttot // c
    nlat = ns - nct
    hv = GDN_HEADS * GDN_DK

    def tmap(d, s):
        rev = jnp.where(s < nct, nct - 1 - s, 2 * nct + nlat - 1 - s)
        return jnp.where(d == 0, s, rev)

    def omap(d, s):
        first = jnp.where(d == 0, 0, nlat - 1)
        return jnp.where(s < nct, first, tmap(d, s) - nct)

    return pl.pallas_call(
        _gdn_scan_kernel,
        grid=(bsz, 2, ns),
        in_specs=[pl.BlockSpec((None, c, hv), lambda b, d, s: (b, tmap(d, s), 0)),
                  pl.BlockSpec((None, c, hv), lambda b, d, s: (b, tmap(d, s), 1)),
                  pl.BlockSpec((None, c, hv), lambda b, d, s: (b, tmap(d, s), 2)),
                  pl.BlockSpec((None, None, c, LANE), lambda b, d, s: (d, b, tmap(d, s), 0))],
        out_specs=pl.BlockSpec((None, None, c, hv), lambda b, d, s: (d, b, omap(d, s), 0)),
        out_shape=jax.ShapeDtypeStruct((2, bsz, nlat * c, hv), F32),
        scratch_shapes=[pltpu.VMEM((GDN_HEADS, GDN_DK, GDN_DK), F32)],
        compiler_params=_params(("parallel", "parallel", "arbitrary")),
    )(qkv_all, qkv_all, qkv_all, gb_all)


def _rope_tile(xr, cos, sin):
    lane = lax.broadcasted_iota(I32, xr.shape, 1)
    first_half = (lane % 32) < 16
    rot = jnp.where(first_half, -pltpu.roll(xr, LANE - 16, 1), pltpu.roll(xr, 16, 1))
    return xr * cos + rot * sin


def _mla_q_kernel(c_ref, ga_ref, w_ref, gn_ref, gr_ref, cos_ref, sin_ref, q_ref):
    cn = _rms(c_ref[...].astype(F32), ga_ref[...]).astype(BF16)
    qf = _dot(cn, w_ref[...])
    tm = qf.shape[0]
    lane = lax.broadcasted_iota(I32, (tm, LANE), 1)
    left = lane < ROPE_DIM
    rope_base = MLA_HEADS * NOPE_DIM
    scale = QK_HEAD ** -0.5 * math.log2(math.e)
    cos = cos_ref[...]
    sin = sin_ref[...]
    for hp in range(MLA_HEADS // 2):
        rt = qf[:, rope_base + hp * LANE:rope_base + (hp + 1) * LANE]
        rsq = rt * rt
        ss_left = jnp.sum(jnp.where(left, rsq, 0.0), axis=-1, keepdims=True)
        ss_right = jnp.sum(jnp.where(left, 0.0, rsq), axis=-1, keepdims=True)
        invs = []
        for par, ss_r in ((0, ss_left), (1, ss_right)):
            h = 2 * hp + par
            nope = qf[:, h * NOPE_DIM:(h + 1) * NOPE_DIM]
            ss = jnp.sum(nope * nope, axis=-1, keepdims=True) + ss_r
            inv = lax.rsqrt(ss * (1.0 / QK_HEAD) + EPS)
            invs.append(inv)
            q_ref[h, :, 0:NOPE_DIM] = (nope * inv * gn_ref[...] * scale).astype(BF16)
        inv_lane = jnp.where(left, invs[0], invs[1])
        xr = _rope_tile(rt * inv_lane * gr_ref[...], cos, sin) * scale
        q_ref[2 * hp, :, NOPE_DIM:QK_PAD] = jnp.where(left, xr, 0.0).astype(BF16)
        q_ref[2 * hp + 1, :, NOPE_DIM:QK_PAD] = jnp.where(left, pltpu.roll(xr, ROPE_DIM, 1), 0.0).astype(BF16)


def _mla_queries(p_all, ga, w, gn, gr, cos, sin, nct_tiles, seq):
    bsz = p_all.shape[0]
    tm = SEQ_TILE
    return pl.pallas_call(
        _mla_q_kernel,
        grid=(bsz, seq // tm),
        in_specs=[pl.BlockSpec((None, tm, Q_LORA), lambda b, i: (b, i + nct_tiles, P_CQ // Q_LORA)),
                  pl.BlockSpec(ga.shape, lambda b, i: (0, 0)),
                  pl.BlockSpec(w.shape, lambda b, i: (0, 0)),
                  pl.BlockSpec(gn.shape, lambda b, i: (0, 0)),
                  pl.BlockSpec(gr.shape, lambda b, i: (0, 0)),
                  pl.BlockSpec((tm, LANE), lambda b, i: (i + nct_tiles, 0)),
                  pl.BlockSpec((tm, LANE), lambda b, i: (i + nct_tiles, 0))],
        out_specs=pl.BlockSpec((None, MLA_HEADS, tm, QK_PAD), lambda b, i: (b, 0, i, 0)),
        out_shape=jax.ShapeDtypeStruct((bsz, MLA_HEADS, seq, QK_PAD), BF16),
        compiler_params=_params(("parallel", "parallel")),
    )(p_all, ga, w, gn, gr, cos, sin)


def _mla_kv_kernel(c_ref, kr_ref, ga_ref, w_ref, gn_ref, gr_ref, cos_ref, sin_ref, k_ref, v_ref):
    cn = _rms(c_ref[...].astype(F32), ga_ref[...]).astype(BF16)
    kv = _dot(cn, w_ref[...])
    kr = kr_ref[...].astype(F32)
    ss_r = jnp.sum(kr * kr, axis=-1, keepdims=True)
    kr_rot = _rope_tile(kr * gr_ref[...], cos_ref[...], sin_ref[...])
    width = NOPE_DIM + LANE
    ones_col = (lax.broadcasted_iota(I32, kr.shape, 1) == 0).astype(BF16)
    for h in range(MLA_HEADS):
        nope = kv[:, h * width:h * width + NOPE_DIM]
        ss = jnp.sum(nope * nope, axis=-1, keepdims=True) + ss_r
        inv = lax.rsqrt(ss * (1.0 / QK_HEAD) + EPS)
        k_ref[h, 0:NOPE_DIM, :] = (nope * inv * gn_ref[...]).T.astype(BF16)
        k_ref[h, NOPE_DIM:QK_PAD, :] = (kr_rot * inv).T.astype(BF16)
        v_ref[h, :, 0:LANE] = kv[:, h * width + NOPE_DIM:(h + 1) * width].astype(BF16)
        v_ref[h, :, LANE:2 * LANE] = ones_col


def _mla_keys_values(p_all, ga, w, gn, gr, cos, sin):
    bsz, ttot, _ = p_all.shape
    tm = SEQ_TILE
    return pl.pallas_call(
        _mla_kv_kernel,
        grid=(bsz, ttot // tm),
        in_specs=[pl.BlockSpec((None, tm, KV_LORA), lambda b, i: (b, i, P_CKV // KV_LORA)),
                  pl.BlockSpec((None, tm, LANE), lambda b, i: (b, i, P_KR // LANE)),
                  pl.BlockSpec(ga.shape, lambda b, i: (0, 0)),
                  pl.BlockSpec(w.shape, lambda b, i: (0, 0)),
                  pl.BlockSpec(gn.shape, lambda b, i: (0, 0)),
                  pl.BlockSpec(gr.shape, lambda b, i: (0, 0)),
                  pl.BlockSpec((tm, LANE), lambda b, i: (i, 0)),
                  pl.BlockSpec((tm, LANE), lambda b, i: (i, 0))],
        out_specs=[pl.BlockSpec((None, MLA_HEADS, QK_PAD, tm), lambda b, i: (b, 0, 0, i)),
                   pl.BlockSpec((None, MLA_HEADS, tm, 2 * LANE), lambda b, i: (b, 0, i, 0))],
        out_shape=[jax.ShapeDtypeStruct((bsz, MLA_HEADS, QK_PAD, ttot), BF16),
                   jax.ShapeDtypeStruct((bsz, MLA_HEADS, ttot, 2 * LANE), BF16)],
        compiler_params=_params(("parallel", "parallel")),
    )(p_all, p_all, ga, w, gn, gr, cos, sin)


FLASH_PARTS = 2
FLASH_ROWS = 64


def _flash_kernel(tk, q_ref, k_ref, v_ref, o_ref, s_a, s_b, p_a, p_b, m_scr, a_scr, acc_scr):
    nk = v_ref.shape[0] // tk
    tq = q_ref.shape[0]
    q = q_ref[...]
    m_scr[...] = jnp.full_like(m_scr, -jnp.inf)
    acc_scr[...] = jnp.zeros_like(acc_scr)

    def scores(j):
        return _dot(q, k_ref[:, pl.ds(pl.multiple_of(j * tk, tk), tk)])

    def update(s_ref, p_ref, j):
        vj = v_ref[pl.ds(pl.multiple_of(j * tk, tk), tk), :]
        half = tq // FLASH_PARTS
        for part in range(FLASH_PARTS):
            for r in range(half // FLASH_ROWS):
                r0 = part * half + r * FLASH_ROWS
                rs = slice(r0, r0 + FLASH_ROWS)
                s = s_ref[rs, :]
                m_prev = m_scr[rs, :]
                m_new = jnp.maximum(m_prev, jnp.max(s, axis=-1, keepdims=True))
                m_scr[rs, :] = m_new
                a_scr[rs, :] = jnp.exp2(m_prev - m_new)
                p_ref[rs, :] = jnp.exp2(s - m_new).astype(BF16)
            hs = slice(part * half, (part + 1) * half)
            acc_scr[hs, :] = a_scr[hs, :] * acc_scr[hs, :] + _dot(p_ref[hs, :], vj)

    s_a[...] = scores(0)

    def body(i, carry):
        s_b[...] = scores(2 * i + 1)
        update(s_a, p_a, 2 * i)
        s_a[...] = scores(2 * i + 2)
        update(s_b, p_b, 2 * i + 1)
        return carry

    lax.fori_loop(0, (nk - 1) // 2, body, 0)
    if nk % 2 == 1:
        update(s_a, p_a, nk - 1)
    else:
        s_b[...] = scores(nk - 1)
        update(s_a, p_a, nk - 2)
        update(s_b, p_b, nk - 1)
    o_ref[...] = (acc_scr[:, 0:LANE] / acc_scr[:, LANE:LANE + 1]).astype(o_ref.dtype)


def _key_tile(ttot):
    for cand in (1280, 1024, 768, 512, 256, 128):
        if ttot % cand == 0:
            return cand
    raise ValueError("key length must be a multiple of 128")


def _attention(q, k, v):
    bsz, heads, seq, _ = q.shape
    ttot = v.shape[2]
    tq = 1024
    nq = seq // tq
    tk = _key_tile(ttot)
    return pl.pallas_call(
        functools.partial(_flash_kernel, tk),
        grid=(bsz, heads, nq),
        in_specs=[pl.BlockSpec((None, None, tq, QK_PAD), lambda b, h, i: (b, h, i, 0)),
                  pl.BlockSpec((None, None, QK_PAD, ttot), lambda b, h, i: (b, h, 0, 0),
                               pipeline_mode=pl.Buffered(1)),
                  pl.BlockSpec((None, None, ttot, 2 * LANE), lambda b, h, i: (b, h, 0, 0),
                               pipeline_mode=pl.Buffered(1))],
        out_specs=pl.BlockSpec((None, tq, LANE), lambda b, h, i: (b, i, h)),
        out_shape=jax.ShapeDtypeStruct((bsz, seq, heads * LANE), BF16),
        scratch_shapes=[pltpu.VMEM((tq, tk), F32), pltpu.VMEM((tq, tk), F32),
                        pltpu.VMEM((tq, tk), BF16), pltpu.VMEM((tq, tk), BF16),
                        pltpu.VMEM((tq, 1), F32), pltpu.VMEM((tq, 1), F32),
                        pltpu.VMEM((tq, 2 * LANE), F32)],
        compiler_params=_params(("parallel", "parallel", "arbitrary")),
    )(q, k, v)


def _pack_halves(x):
    w = x.shape[1] // 2
    bits = lax.bitcast_convert_type(x.astype(BF16).astype(F32), U32)
    return jnp.bitwise_or(jnp.right_shift(bits[:, :w], jnp.uint32(16)), bits[:, w:])


def _unpack_halves(wd):
    lo = lax.bitcast_convert_type(jnp.left_shift(wd, jnp.uint32(16)), F32)
    hi = lax.bitcast_convert_type(jnp.bitwise_and(wd, jnp.uint32(0xFFFF0000)), F32)
    return lo, hi


def _ffn_front(xn, gf_ref, sh_ref, sc_ref, wrh_ref, wrl_ref, hp_ref, lg_ref):
    h2 = _rms(xn, gf_ref[...]) * (1.0 + sc_ref[...]) + sh_ref[...]
    hh = h2.astype(BF16)
    hl = (h2 - hh.astype(F32)).astype(BF16)
    hp_ref[...] = _pack_halves(h2)
    both = _dot(hh, wrl_ref[...])
    lg_ref[...] = both[:, 0:LANE] + both[:, LANE:] + _dot(hl, wrh_ref[...])


def _mix_out_kernel(of_ref, ob_ref, z_ref, ym_ref, x_ref, gog_ref, wout_ref, g1_ref,
                    gf_ref, sh_ref, sc_ref, wrh_ref, wrl_ref, xo_ref, hp_ref, lg_ref, mix):
    dv = GDN_DK
    o = of_ref[...] + ob_ref[...]
    for h in range(GDN_HEADS):
        hs = slice(h * dv, (h + 1) * dv)
        y = _rms(o[:, hs], gog_ref[...])
        mix[:, hs] = (y * _silu(z_ref[:, hs].astype(F32))).astype(BF16)
    hv = GDN_HEADS * dv
    mix[:, hv:] = ym_ref[...]
    xn = x_ref[...] + g1_ref[...] * _dot(mix[...], wout_ref[...])
    xo_ref[...] = xn
    _ffn_front(xn, gf_ref, sh_ref, sc_ref, wrh_ref, wrl_ref, hp_ref, lg_ref)


def _mix_out(o2, p_all, ymla, x, gog, wout, g1, gf, sh2, sc2, wrh, wrl, nct_tiles):
    bsz, seq, d = x.shape
    tm = SEQ_TILE
    hv = GDN_HEADS * GDN_DK
    vec = lambda: pl.BlockSpec((None, 1, d), lambda b, i: (b, 0, 0))
    full = lambda a: pl.BlockSpec(a.shape, lambda b, i: (0,) * a.ndim)
    return pl.pallas_call(
        _mix_out_kernel,
        grid=(bsz, seq // tm),
        in_specs=[pl.BlockSpec((None, None, tm, hv), lambda b, i: (0, b, i, 0)),
                  pl.BlockSpec((None, None, tm, hv), lambda b, i: (1, b, i, 0)),
                  pl.BlockSpec((None, tm, hv), lambda b, i: (b, i + nct_tiles, P_Z // hv)),
                  pl.BlockSpec((None, tm, hv), lambda b, i: (b, i, 0)),
                  pl.BlockSpec((None, tm, d), lambda b, i: (b, i, 0)),
                  full(gog), full(wout), vec(), full(gf), vec(), vec(), full(wrh), full(wrl)],
        out_specs=[pl.BlockSpec((None, tm, d), lambda b, i: (b, i, 0)),
                   pl.BlockSpec((None, tm, d // 2), lambda b, i: (b, i, 0)),
                   pl.BlockSpec((None, tm, LANE), lambda b, i: (b, i, 0))],
        out_shape=[jax.ShapeDtypeStruct((bsz, seq, d), F32),
                   jax.ShapeDtypeStruct((bsz, seq, d // 2), U32),
                   jax.ShapeDtypeStruct((bsz, seq, LANE), F32)],
        scratch_shapes=[pltpu.VMEM((tm, wout.shape[0]), BF16)],
        compiler_params=_params(("parallel", "parallel")),
    )(o2, o2, p_all, ymla, x, gog, wout, g1, gf, sh2, sc2, wrh, wrl)


POOL_HALO = 8


def _pool_kernel(seq, xm_ref, xp_ref, xn_ref, gm_ref, sh1_ref, sc1_ref, band_ref, wp_ref, ps_ref, g1_ref,
                 gf_ref, sh_ref, sc_ref, wrh_ref, wrl_ref, xo_ref, hp_ref, lg_ref, ext):
    i = pl.program_id(1)
    n = pl.num_programs(1)
    tm = xm_ref.shape[0]
    hl = POOL_HALO

    def normed(ref):
        return _rms(ref[...], gm_ref[...]) * (1.0 + sc1_ref[...]) + sh1_ref[...]

    ext[hl:hl + tm, :] = normed(xm_ref)
    ext[0:hl, :] = jnp.where(i == 0, 0.0, normed(xp_ref))
    ext[hl + tm:2 * hl + tm, :] = jnp.where(i == n - 1, 0.0, normed(xn_ref))
    ext[2 * hl + tm:, :] = jnp.zeros((ext.shape[0] - 2 * hl - tm, ext.shape[1]), F32)
    t = i * tm + lax.broadcasted_iota(I32, (tm, 1), 0)
    gw = xm_ref.shape[1] // len(POOL_WINDOWS)
    for gi, win in enumerate(POOL_WINDOWS):
        cs = slice(gi * gw, (gi + 1) * gw)
        half = win // 2
        acc = _dot(band_ref[gi], ext[:, cs].astype(BF16))
        lo = jnp.clip(t - half, 0, seq)
        hi = jnp.clip(t - half + win, 0, seq)
        pooled = acc / (hi - lo).astype(F32) - ext[hl:hl + tm, cs]
        y = _dot(pooled.astype(BF16), wp_ref[gi]) * ps_ref[:, cs]
        xo_ref[:, cs] = xm_ref[:, cs] + g1_ref[:, cs] * y
    _ffn_front(xo_ref[...], gf_ref, sh_ref, sc_ref, wrh_ref, wrl_ref, hp_ref, lg_ref)


def _pool_mixer(x, gm, sh1, sc1, wp, ps, g1, gf, sh2, sc2, wrh, wrl):
    bsz, seq, d = x.shape
    tm = SEQ_TILE
    hl = POOL_HALO
    r = tm // hl
    nh = seq // hl
    vec = lambda: pl.BlockSpec((None, 1, d), lambda b, i: (b, 0, 0))
    full = lambda a: pl.BlockSpec(a.shape, lambda b, i: (0,) * a.ndim)
    ext_rows = -(-(tm + 2 * hl) // LANE) * LANE
    offset = jnp.arange(ext_rows, dtype=I32)[None, :] - hl - jnp.arange(tm, dtype=I32)[:, None]
    band = jnp.stack([((offset >= -(w // 2)) & (offset < w - w // 2)).astype(BF16) for w in POOL_WINDOWS])
    return pl.pallas_call(
        functools.partial(_pool_kernel, seq),
        grid=(bsz, seq // tm),
        in_specs=[pl.BlockSpec((None, tm, d), lambda b, i: (b, i, 0)),
                  pl.BlockSpec((None, hl, d), lambda b, i: (b, jnp.maximum(i * r - 1, 0), 0)),
                  pl.BlockSpec((None, hl, d), lambda b, i: (b, jnp.minimum((i + 1) * r, nh - 1), 0)),
                  full(gm), vec(), vec(), full(band), full(wp), full(ps), vec(),
                  full(gf), vec(), vec(), full(wrh), full(wrl)],
        out_specs=[pl.BlockSpec((None, tm, d), lambda b, i: (b, i, 0)),
                   pl.BlockSpec((None, tm, d // 2), lambda b, i: (b, i, 0)),
                   pl.BlockSpec((None, tm, LANE), lambda b, i: (b, i, 0))],
        out_shape=[jax.ShapeDtypeStruct((bsz, seq, d), F32),
                   jax.ShapeDtypeStruct((bsz, seq, d // 2), U32),
                   jax.ShapeDtypeStruct((bsz, seq, LANE), F32)],
        scratch_shapes=[pltpu.VMEM((ext_rows, d), F32)],
        compiler_params=_params(("parallel", "parallel")),
    )(x, x, x, gm, sh1, sc1, band, wp, ps, g1, gf, sh2, sc2, wrh, wrl)


def _first_max(vals, idx, sentinel):
    m = jnp.max(vals, axis=0, keepdims=True)
    first = jnp.min(jnp.where(vals == m, idx, sentinel), axis=0, keepdims=True)
    return m, first


def _route_kernel(rows, lg_ref, bias_ref, dest_ref, gates_ref, cnt_ref, carry):
    phase = pl.program_id(0)
    step = pl.program_id(1)

    @pl.when(jnp.logical_and(phase == 0, step == 0))
    def _():
        carry[...] = jnp.zeros_like(carry)

    @pl.when(jnp.logical_and(phase == 1, step == 0))
    def _():
        shift = int(math.log2(rows))
        counts_row = carry[...].T.astype(I32)
        padded = jnp.left_shift(jnp.right_shift(counts_row + (rows - 1), shift), shift).astype(F32)
        i_exp = lax.broadcasted_iota(I32, padded.shape, 0)
        j_exp = lax.broadcasted_iota(I32, padded.shape, 1)
        first_row = jnp.sum(jnp.where(j_exp < i_exp, padded, 0.0), axis=1, keepdims=True)
        carry[...] = jnp.broadcast_to(first_row, carry.shape)

    tm = lg_ref.shape[0]
    epg = EXPERTS_PER_GROUP
    scores = jax.nn.sigmoid(lg_ref[...].T[0:N_EXPERTS, :])
    biased = scores + bias_ref[:, 0:1]
    eidx = lax.broadcasted_iota(I32, (epg, tm), 0)
    best_score = best_grp = best_i1 = best_i2 = None
    for g in range(N_GROUPS):
        blk = biased[g * epg:(g + 1) * epg, :]
        m1, i1 = _first_max(blk, eidx, epg)
        m2, i2 = _first_max(jnp.where(eidx == i1, -jnp.inf, blk), eidx, epg)
        gs = m1 + m2
        if g == 0:
            best_score, best_grp, best_i1, best_i2 = gs, jnp.zeros_like(i1), i1, i2
        else:
            better = gs > best_score
            best_score = jnp.where(better, gs, best_score)
            best_grp = jnp.where(better, g, best_grp)
            best_i1 = jnp.where(better, i1, best_i1)
            best_i2 = jnp.where(better, i2, best_i2)
    e1 = best_grp * epg + best_i1
    e2 = best_grp * epg + best_i2
    eall = lax.broadcasted_iota(I32, (LANE, tm), 0)
    oh1 = eall == e1
    oh2 = eall == e2
    oh = jnp.logical_or(oh1, oh2).astype(BF16)
    picked = jnp.sum(oh.astype(F32), axis=1, keepdims=True)

    @pl.when(phase == 0)
    def _():
        carry[...] = carry[...] + picked
        cnt_ref[...] = carry[...]

    @pl.when(phase == 1)
    def _():
        s1 = jnp.sum(jnp.where(oh1[0:N_EXPERTS], scores, 0.0), axis=0, keepdims=True)
        s2 = jnp.sum(jnp.where(oh2[0:N_EXPERTS], scores, 0.0), axis=0, keepdims=True)
        denom = s1 + s2
        earlier = (lax.broadcasted_iota(I32, (tm, tm), 0) < lax.broadcasted_iota(I32, (tm, tm), 1)).astype(BF16)
        row = _dot(oh, earlier) + carry[:, 0:1]
        d1 = jnp.sum(jnp.where(oh1, row, 0.0), axis=0, keepdims=True)
        d2 = jnp.sum(jnp.where(oh2, row, 0.0), axis=0, keepdims=True)
        carry[...] = carry[...] + picked
        dest_ref[...] = jnp.concatenate([d1.astype(I32), d2.astype(I32), jnp.zeros((6, tm), I32)], axis=0)
        gates = jnp.concatenate([s1 / denom, s2 / denom, jnp.zeros((LANE - 2, tm), F32)], axis=0)
        gates_ref[...] = gates.T


def _route(logits, bias_col):
    n = logits.shape[0]
    tm = ROUTE_TILE
    rows = MOE_ROWS
    assert rows & (rows - 1) == 0
    return pl.pallas_call(
        functools.partial(_route_kernel, rows),
        grid=(2, n // tm),
        in_specs=[pl.BlockSpec((tm, LANE), lambda p, t: (t, 0)),
                  pl.BlockSpec(bias_col.shape, lambda p, t: (0, 0))],
        out_specs=[pl.BlockSpec((8, tm), lambda p, t: (0, p * t)),
                   pl.BlockSpec((tm, LANE), lambda p, t: (p * t, 0)),
                   pl.BlockSpec((LANE, LANE), lambda p, t: (0, 0))],
        out_shape=[jax.ShapeDtypeStruct((8, n), I32),
                   jax.ShapeDtypeStruct((n, LANE), F32),
                   jax.ShapeDtypeStruct((LANE, LANE), F32)],
        scratch_shapes=[pltpu.VMEM((LANE, LANE), F32)],
        compiler_params=_params(("arbitrary", "arbitrary")),
    )(logits, bias_col)


def _dispatch_copy(hp_ref, xs_hbm, sem, j, dst):
    return pltpu.make_async_copy(hp_ref.at[pl.ds(j, 1), :], xs_hbm.at[pl.ds(dst, 1), :], sem)


def _zero_block_copy(zbuf, xs_hbm, zsem, start):
    return pltpu.make_async_copy(zbuf, xs_hbm.at[pl.ds(start, zbuf.shape[0]), :], zsem)


def _dispatch_kernel(tm, pad_end_ref, nu_ref, d0_ref, d1_ref, hp_ref, xs_hbm, zbuf, sem, zsem):
    rows = zbuf.shape[0]
    n_blocks = xs_hbm.shape[0] // rows

    @pl.when(pl.program_id(0) == 0)
    def _():
        zbuf[...] = jnp.zeros_like(zbuf)

        def expert_tail(e, carry):
            start = jnp.maximum(pad_end_ref[e] - rows, 0)
            _zero_block_copy(zbuf, xs_hbm, zsem, pl.multiple_of(start, rows)).start()
            return carry

        lax.fori_loop(0, N_EXPERTS, expert_tail, 0)

        def unused_block(b, carry):
            _zero_block_copy(zbuf, xs_hbm, zsem, pl.multiple_of(b * rows, rows)).start()
            return carry

        lax.fori_loop(nu_ref[0], n_blocks, unused_block, 0)

        def drain(b, carry):
            _zero_block_copy(zbuf, xs_hbm, zsem, 0).wait()
            return carry

        lax.fori_loop(0, N_EXPERTS + n_blocks - nu_ref[0], drain, 0)

    def issue(j, carry):
        _dispatch_copy(hp_ref, xs_hbm, sem, j, d0_ref[0, j]).start()
        _dispatch_copy(hp_ref, xs_hbm, sem, j, d1_ref[0, j]).start()
        return carry

    lax.fori_loop(0, tm, issue, 0, unroll=DMA_UNROLL)
    for _ in range(2):
        pltpu.make_async_copy(hp_ref, xs_hbm.at[pl.ds(0, tm), :], sem).wait()


def _dispatch(hp, dest, pad_end, n_used, n_rows):
    n, half = hp.shape
    tm = ROUTE_TILE
    nt = n // tm
    idx = lambda k: pl.BlockSpec((None, None, 1, tm), lambda i, pe, nu: (k, i, 0, 0), memory_space=pltpu.SMEM)
    dest4 = dest.reshape(dest.shape[0], nt, 1, tm)
    grid_spec = pltpu.PrefetchScalarGridSpec(
        num_scalar_prefetch=2,
        grid=(nt,),
        in_specs=[idx(0), idx(1), pl.BlockSpec((tm, half), lambda i, pe, nu: (i, 0))],
        out_specs=pl.BlockSpec(memory_space=pl.ANY),
        scratch_shapes=[pltpu.VMEM((MOE_ROWS, half), U32), pltpu.SemaphoreType.DMA(()),
                        pltpu.SemaphoreType.DMA(())],
    )
    return pl.pallas_call(
        functools.partial(_dispatch_kernel, tm),
        grid_spec=grid_spec,
        out_shape=jax.ShapeDtypeStruct((n_rows, half), U32),
        compiler_params=_params(("arbitrary",)),
    )(pad_end, n_used, dest4, dest4, hp)


def _expert_weight_copies(layer, wgu_hbm, wd_hbm, wgu32, wd32, sem, expert, slot):
    return (pltpu.make_async_copy(wgu_hbm.at[layer, expert], wgu32.at[slot], sem.at[0, slot]),
            pltpu.make_async_copy(wd_hbm.at[layer, expert], wd32.at[slot], sem.at[1, slot]))


def _experts_kernel(layer, be_ref, nu_ref, next_ref, slot_ref, xs_ref, wgu_hbm, wd_hbm, y_ref,
                    wgu32, wd32, wgu16, wd16, sem):
    i = pl.program_id(0)
    expert = be_ref[i]
    copies = functools.partial(_expert_weight_copies, layer, wgu_hbm, wd_hbm, wgu32, wd32, sem)

    @pl.when(i < nu_ref[0])
    def _():
        @pl.when(jnp.logical_or(i == 0, expert != be_ref[jnp.maximum(i - 1, 0)]))
        def _():
            slot = slot_ref[expert]

            @pl.when(i == 0)
            def _():
                for c in copies(expert, slot):
                    c.start()

            for c in copies(expert, slot):
                c.wait()
            wgu16[...] = wgu32[slot].astype(BF16)
            wd16[...] = wd32[slot].astype(BF16)
            following = next_ref[expert]

            @pl.when(following >= 0)
            def _():
                for c in copies(following, 1 - slot):
                    c.start()

        lo, hi = _unpack_halves(xs_ref[...])
        half = lo.shape[1]
        gu = _dot(lo.astype(BF16), wgu16[0:half, :]) + _dot(hi.astype(BF16), wgu16[half:, :])
        f = gu.shape[1] // 2
        hid = _silu(gu[:, :f]) * gu[:, f:]
        y_ref[...] = _pack_halves(_dot(hid.astype(BF16), wd16[...]))

    @pl.when(i >= nu_ref[0])
    def _():
        y_ref[...] = jnp.zeros_like(y_ref)


def _experts(layer, block_expert, n_used, next_expert, weight_slot, xs, w_gate_up, w_down):
    n_rows, half = xs.shape
    rows = MOE_ROWS
    d = 2 * half
    f2 = w_gate_up.shape[3]
    grid_spec = pltpu.PrefetchScalarGridSpec(
        num_scalar_prefetch=4,
        grid=(n_rows // rows,),
        in_specs=[pl.BlockSpec((rows, half), lambda i, *_: (i, 0)),
                  pl.BlockSpec(memory_space=pl.ANY),
                  pl.BlockSpec(memory_space=pl.ANY)],
        out_specs=pl.BlockSpec((rows, half), lambda i, *_: (i, 0)),
        scratch_shapes=[pltpu.VMEM((2, d, f2), F32), pltpu.VMEM((2, f2 // 2, d), F32),
                        pltpu.VMEM((d, f2), BF16), pltpu.VMEM((f2 // 2, d), BF16),
                        pltpu.SemaphoreType.DMA((2, 2))],
    )
    return pl.pallas_call(
        functools.partial(_experts_kernel, layer),
        grid_spec=grid_spec,
        out_shape=jax.ShapeDtypeStruct((n_rows, half), U32),
        compiler_params=_params(("arbitrary",)),
    )(block_expert, n_used, next_expert, weight_slot, xs, w_gate_up, w_down)


def _combine_copy(y_hbm, buf, sem, slot, k, src, j):
    return pltpu.make_async_copy(y_hbm.at[pl.ds(src, 1), :], buf.at[slot, k, pl.ds(j, 1), :], sem.at[slot])


def _combine_kernel(tm, d0_ref, d1_ref, d0n_ref, d1n_ref, y_hbm, gates_ref, x_ref, g2_ref, xo_ref, buf, sem):
    step = pl.program_id(0) * pl.num_programs(1) + pl.program_id(1)
    total = pl.num_programs(0) * pl.num_programs(1)
    slot = step % 2

    def start_gather(r0_ref, r1_ref, s):
        def issue(j, carry):
            _combine_copy(y_hbm, buf, sem, s, 0, r0_ref[0, j], j).start()
            _combine_copy(y_hbm, buf, sem, s, 1, r1_ref[0, j], j).start()
            return carry
        lax.fori_loop(0, tm, issue, 0, unroll=DMA_UNROLL)

    @pl.when(step == 0)
    def _():
        start_gather(d0_ref, d1_ref, 0)

    @pl.when(step + 1 < total)
    def _():
        start_gather(d0n_ref, d1n_ref, 1 - slot)

    for k in range(2):
        pltpu.make_async_copy(y_hbm.at[pl.ds(0, tm), :], buf.at[slot, k], sem.at[slot]).wait()
    lo0, hi0 = _unpack_halves(buf[slot, 0])
    lo1, hi1 = _unpack_halves(buf[slot, 1])
    half = lo0.shape[1]
    ga = gates_ref[:, 0:1]
    gb = gates_ref[:, 1:2]
    xo_ref[:, 0:half] = x_ref[:, 0:half] + g2_ref[:, 0:half] * (ga * lo0 + gb * lo1)
    xo_ref[:, half:] = x_ref[:, half:] + g2_ref[:, half:] * (ga * hi0 + gb * hi1)


def _combine(y, dest, gates, x, g2):
    bsz, seq, d = x.shape
    tm = SEQ_TILE
    nt = seq // tm
    half = d // 2
    last = bsz * nt - 1
    idx = lambda k, ahead: pl.BlockSpec(
        (None, None, 1, tm), lambda b, i: (k, jnp.minimum(b * nt + i + ahead, last), 0, 0),
        memory_space=pltpu.SMEM)
    dest4 = dest.reshape(dest.shape[0], bsz * nt, 1, tm)
    return pl.pallas_call(
        functools.partial(_combine_kernel, tm),
        grid=(bsz, nt),
        in_specs=[idx(0, 0), idx(1, 0), idx(0, 1), idx(1, 1),
                  pl.BlockSpec(memory_space=pl.ANY),
                  pl.BlockSpec((tm, LANE), lambda b, i: (b * nt + i, 0)),
                  pl.BlockSpec((None, tm, d), lambda b, i: (b, i, 0)),
                  pl.BlockSpec((None, 1, d), lambda b, i: (b, 0, 0))],
        out_specs=pl.BlockSpec((None, tm, d), lambda b, i: (b, i, 0)),
        out_shape=jax.ShapeDtypeStruct((bsz, seq, d), F32),
        scratch_shapes=[pltpu.VMEM((2, 2, tm, half), U32), pltpu.SemaphoreType.DMA((2,))],
        compiler_params=_params(("arbitrary", "arbitrary")),
    )(dest4, dest4, dest4, dest4, y, gates, x, g2)


def _moe(layer, x, hp, logits, g2, bias_col, w_gate_up, w_down):
    bsz, seq, d = x.shape
    n = bsz * seq
    rows = MOE_ROWS
    dest, gates, cnt = _route(logits.reshape(n, LANE), bias_col)
    counts = cnt[:N_EXPERTS, 0].astype(I32)
    pad_end = jnp.cumsum((counts + rows - 1) // rows * rows)
    n_rows = -(-(n * TOP_K + N_EXPERTS * (rows - 1)) // rows) * rows
    block_start = jnp.arange(n_rows // rows, dtype=I32) * rows
    block_expert = jnp.minimum(jnp.sum((pad_end[None, :] <= block_start[:, None]).astype(I32), axis=1),
                               N_EXPERTS - 1)
    n_used = pad_end[-1:] // rows
    owns = counts > 0
    eid = jnp.arange(N_EXPERTS, dtype=I32)
    later = jnp.where(owns[None, :] & (eid[None, :] > eid[:, None]), eid[None, :], N_EXPERTS)
    next_expert = jnp.min(later, axis=1)
    next_expert = jnp.where(next_expert < N_EXPERTS, next_expert, -1).astype(I32)
    weight_slot = ((jnp.cumsum(owns.astype(I32)) - 1) % 2).astype(I32)
    xs = _dispatch(hp.reshape(n, d // 2), dest, pad_end, n_used, n_rows)
    y = _experts(layer, block_expert, n_used, next_expert, weight_slot, xs, w_gate_up, w_down)
    return _combine(y, dest, gates, x, g2)


def _permute_w_in(w_in):
    d = w_in.shape[0]
    hq = GDN_HEADS * GDN_DK
    off_a = 4 * hq
    off_cq = off_a + 4 * GDN_HEADS
    off_ckv = off_cq + Q_LORA
    off_kr = off_ckv + KV_LORA
    z = lambda n: jnp.zeros((d, n), w_in.dtype)
    a = lambda k: w_in[:, off_a + k * GDN_HEADS:off_a + (k + 1) * GDN_HEADS]
    pad_ab = LANE - 2 * GDN_HEADS
    cols = [w_in[:, :off_a], w_in[:, off_cq:off_kr + ROPE_DIM], z(LANE - ROPE_DIM),
            a(0), a(2), z(pad_ab), a(1), a(3), z(pad_ab)]
    w = jnp.concatenate(cols, axis=1)
    return jnp.concatenate([w, z(P_WIDTH - w.shape[1])], axis=1).astype(BF16)


def _rope_tables(rows, n_ctx):
    row = jnp.repeat(jnp.arange(rows, dtype=F32), GRID_W)
    col = jnp.tile(jnp.arange(GRID_W, dtype=F32), rows)
    pairs = ROPE_DIM // 4
    inv_freq = ROPE_THETA ** (-jnp.arange(pairs, dtype=F32) / pairs)
    ang_r = row[:, None] * inv_freq
    ang_c = col[:, None] * inv_freq
    ang = jnp.concatenate([ang_r, ang_r, ang_c, ang_c], axis=-1)
    ang = jnp.concatenate([jnp.zeros((n_ctx, ROPE_DIM), F32), ang], axis=0)
    cos, sin = jnp.cos(ang), jnp.sin(ang)
    return jnp.concatenate([cos, cos], axis=1), jnp.concatenate([sin, sin], axis=1)


def kernel(x, c, ctx, c_ctx, w_mod, b_mod, norm_mix_g, norm_ffn_g, w_in, conv_qkv, a_log_fwd, a_log_bwd, dt_bias_fwd, dt_bias_bwd, gdn_out_g, q_a_norm_g, w_uq, kv_a_norm_g, w_ukv, q_norm_g, k_norm_g, w_out, w_pool, pool_scale, w_router, router_bias, w_gate_up, w_down):
    bsz, seq, d = x.shape
    n_ctx = ctx.shape[1]
    depth = w_mod.shape[0]
    assert depth == 2, "the context stream is only read: no layer after the first even layer reads it"
    nct_tiles = n_ctx // SEQ_TILE

    cc = jnp.concatenate([c, c_ctx[None, :], jnp.zeros((8 - bsz - 1, d), F32)], axis=0)
    mod = _modulation(cc, w_mod, b_mod).reshape(depth, 8, 6, d)

    def mods(layer):
        m = mod[layer]
        return [m[:bsz, k][:, None, :] for k in range(6)], [m[bsz:bsz + 1, k] for k in range(6)]

    wr = jnp.pad(w_router, ((0, 0), (0, LANE - N_EXPERTS)))
    wrh = wr.astype(BF16)
    wrl = jnp.concatenate([wrh, (wr - wrh.astype(F32)).astype(BF16)], axis=1)
    bias_col = jnp.broadcast_to(router_bias.astype(F32)[:, None], (N_EXPERTS, LANE))
    row = lambda v: v.astype(F32)[None, :]

    for layer in range(depth):
        j = layer // 2
        (sh1, sc1, g1, sh2, sc2, g2), (csh1, csc1, _, _, _, _) = mods(layer)
        gm = row(norm_mix_g[layer])
        gf = row(norm_ffn_g[layer])
        if layer % 2 == 0:
            p_all = _in_projection(ctx, x, gm, csh1, csc1, sh1, sc1, _permute_w_in(w_in[j]))
            conv_w = conv_qkv[j].astype(F32)
            gate_params = jnp.zeros((2, 8, LANE), F32)
            gate_params = gate_params.at[0, 0, :GDN_HEADS].set(a_log_fwd[j]).at[0, 1, :GDN_HEADS].set(dt_bias_fwd[j])
            gate_params = gate_params.at[1, 0, :GDN_HEADS].set(a_log_bwd[j]).at[1, 1, :GDN_HEADS].set(dt_bias_bwd[j])
            qkv_all, gb_all = _gdn_front(p_all, conv_w, gate_params, nct_tiles)
            o2 = _gdn_scan(qkv_all, gb_all, n_ctx)

            cos, sin = _rope_tables(seq // GRID_W, n_ctx)
            gq = q_norm_g[j].astype(F32)
            gk = k_norm_g[j].astype(F32)
            wq = w_uq[j].reshape(Q_LORA, MLA_HEADS, QK_HEAD)
            wq = jnp.concatenate([wq[:, :, :NOPE_DIM].reshape(Q_LORA, -1),
                                  wq[:, :, NOPE_DIM:].reshape(Q_LORA, -1)], axis=1).astype(BF16)
            q = _mla_queries(p_all, row(q_a_norm_g[j]), wq, gq[None, :NOPE_DIM],
                             jnp.tile(gq[NOPE_DIM:], 2)[None, :], cos, sin, nct_tiles, seq)
            gk_rope = jnp.concatenate([gk[NOPE_DIM:], jnp.zeros((LANE - ROPE_DIM,), F32)])[None, :]
            k_all, v_all = _mla_keys_values(p_all, row(kv_a_norm_g[j]), w_ukv[j].astype(BF16),
                                            gk[None, :NOPE_DIM], gk_rope, cos, sin)
            ymla = _attention(q, k_all, v_all)
            x, hp, logits = _mix_out(o2, p_all, ymla, x, row(gdn_out_g[j]), w_out[j].astype(BF16), g1,
                                     gf, sh2, sc2, wrh, wrl, nct_tiles)
        else:
            x, hp, logits = _pool_mixer(x, gm, sh1, sc1, w_pool[j].astype(BF16), row(pool_scale[j]), g1,
                                        gf, sh2, sc2, wrh, wrl)
        x = _moe(layer, x, hp, logits, g2, bias_col, w_gate_up, w_down)
    return x
```
